```python
import jax
import jax.numpy as jnp
from jax import lax
import numpy as np

D_MODEL = 1024
BATCH = 4
SEQ = 4096
DEPTH = 2

HEAD_DIM = 64
RWKV_W = D_MODEL // 4
RWKV_H = RWKV_W // HEAD_DIM
ATT_W = 3 * D_MODEL // 8
ATT_H = ATT_W // HEAD_DIM
GLA_W = D_MODEL - RWKV_W - ATT_W
GLA_H = GLA_W // HEAD_DIM
RWKV_DECAY_RANK = 32
RWKV_AAA_RANK = 32
RWKV_GATE_RANK = 64
RWKV_VRES_RANK = 16
RWKV_GN_EPS = 64e-5
GLA_GATE_RANK = 16
GLA_GATE_TAU = 16.0
GLA_CHUNK = 64
DILATED_BRANCHES = ((128, 1), (512, 4), (2048, 16))
ATT_BLOCK = 128
ROPE_THETA = 500000.0
ROPE_DIMS = HEAD_DIM // 4
FFN_DENSE = 2816
N_EXPERTS = 8
TOP_K = 2
FFN_EXPERT = 3584
RMS_EPS = 1e-5
NEG_INF = -1e30

RWKV_SPLITS = (RWKV_W, RWKV_W, RWKV_W, RWKV_DECAY_RANK, RWKV_AAA_RANK, RWKV_GATE_RANK)
ATT_SPLITS = (ATT_W, ATT_W, ATT_W)
GLA_SPLITS = (GLA_W, GLA_W, GLA_W, GLA_GATE_RANK, GLA_W)
RWKV_COLS = 3 * RWKV_W + RWKV_DECAY_RANK + RWKV_AAA_RANK + RWKV_GATE_RANK
ATT_COLS = 3 * ATT_W
GLA_COLS = 4 * GLA_W + GLA_GATE_RANK
IN_COLS = RWKV_COLS + ATT_COLS + GLA_COLS

kernel_name = 'hybrid_rwkv7_dilated_gla_moe_trunk'


def rms_norm(x, g):
    xf = x.astype(jnp.float32)
    y = xf * lax.rsqrt(jnp.mean(xf * xf, axis=-1, keepdims=True) + RMS_EPS)
    return (y * g.astype(jnp.float32)).astype(x.dtype)


def _split(t, sizes):
    parts, start = [], 0
    for size in sizes:
        parts.append(t[..., start:start + size])
        start += size
    return parts


def _heads(t):
    return t.reshape(t.shape[:-1] + (t.shape[-1] // HEAD_DIM, HEAD_DIM))


def token_shift(y):
    return jnp.pad(y, ((0, 0), (1, 0), (0, 0)))[:, :-1]


def partial_rotary(t, positions):
    half = ROPE_DIMS // 2
    inv_freq = ROPE_THETA ** (-jnp.arange(half, dtype=jnp.float32) / half)
    ang = positions.astype(jnp.float32)[..., None] * inv_freq
    cos = jnp.cos(ang)[:, :, None, :]
    sin = jnp.sin(ang)[:, :, None, :]
    rot = t[..., :ROPE_DIMS].astype(jnp.float32)
    x1, x2 = rot[..., :half], rot[..., half:]
    rot = jnp.concatenate([x1 * cos - x2 * sin, x2 * cos + x1 * sin], axis=-1)
    return jnp.concatenate([rot.astype(t.dtype), t[..., ROPE_DIMS:]], axis=-1)


def _rwkv7_scan(r, w, k, v, a, b):
    B, S, H, N = r.shape

    def step(state, inp):
        r_t, w_t, k_t, v_t, a_t, b_t = inp
        sa = jnp.einsum('bhvk,bhk->bhv', state, a_t)
        state = (state * w_t[:, :, None, :] + sa[..., None] * b_t[:, :, None, :]
                 + v_t[..., None] * k_t[:, :, None, :])
        return state, jnp.einsum('bhvk,bhk->bhv', state, r_t)

    xs = tuple(t.transpose(1, 0, 2, 3) for t in (r, w, k, v, a, b))
    _, out = lax.scan(step, jnp.zeros((B, H, N, N), jnp.float32), xs)
    return out.transpose(1, 0, 2, 3)


def rwkv7_group(y, mu, w0, w2, a0, a2, g2, k_k, k_a, r_k, ln_g, ln_b, v_first, vres):
    B, S, _ = y.shape
    yf = y.astype(jnp.float32)
    ym = yf + (token_shift(yf) - yf) * mu
    r, k, v, xw, xa, xg = _split(ym, RWKV_SPLITS)
    w_log = -jax.nn.softplus(-(w0 + jnp.tanh(xw) @ w2)) - 0.5
    decay = jnp.exp(-jnp.exp(w_log))
    a = jax.nn.sigmoid(a0 + xa @ a2)
    g = jax.nn.sigmoid(xg) @ g2
    if vres is None:
        v_first = v
    else:
        v0, v1, v2 = vres
        v = v + (v_first - v) * jax.nn.sigmoid(v0 + (v @ v1) @ v2)
    r, k, v, decay, a = (_heads(t) for t in (r, k, v, decay, a))
    kk = k * _heads(k_k)
    kk = kk * lax.rsqrt(jnp.maximum(jnp.sum(kk * kk, axis=-1, keepdims=True), 1e-24))
    k = k * (1.0 + (a - 1.0) * _heads(k_a))
    o = _rwkv7_scan(r, decay, k, v, -kk, kk * a)
    mean = jnp.mean(o, axis=-1, keepdims=True)
    var = jnp.mean(jnp.square(o - mean), axis=-1, keepdims=True)
    o = ((o - mean) * lax.rsqrt(var + RWKV_GN_EPS)).reshape(B, S, RWKV_W) * ln_g + ln_b
    bonus = (jnp.sum(r * k * r_k, axis=-1, keepdims=True) * v).reshape(B, S, RWKV_W)
    return (o + bonus) * g, v_first


def _to_residue_blocks(t, dil, n_blk):
    B, S, H, Dh = t.shape
    L = S // dil
    t = t.reshape(B, L, dil, H, Dh).transpose(0, 2, 1, 3, 4)
    t = jnp.pad(t, ((0, 0), (0, 0), (0, n_blk * ATT_BLOCK - L), (0, 0), (0, 0)))
    return t.reshape(B, dil, n_blk, ATT_BLOCK, H, Dh)


def _with_previous_block(t):
    prev = jnp.pad(t, ((0, 0), (0, 0), (1, 0), (0, 0), (0, 0), (0, 0)))[:, :, :-1]
    return jnp.concatenate([prev, t], axis=3)


def dilated_branch(q, k, v, window, dil):
    B, S, H, Dh = q.shape
    L = S // dil
    n_blk = -(-L // ATT_BLOCK)
    span = window // dil
    qb = _to_residue_blocks(q, dil, n_blk)
    kb = _with_previous_block(_to_residue_blocks(k, dil, n_blk))
    vb = _with_previous_block(_to_residue_blocks(v, dil, n_blk)).astype(jnp.float32)
    qi = jnp.arange(ATT_BLOCK)[:, None]
    kc = jnp.arange(2 * ATT_BLOCK)[None, :]
    rel = qi + ATT_BLOCK - kc
    key_idx = jnp.arange(n_blk)[:, None, None] * ATT_BLOCK + kc[None] - ATT_BLOCK
    valid = (rel >= 0) & (rel <= span) & (key_idx >= 0)
    s = jnp.einsum('brnqhd,brnkhd->brnhqk', qb, kb, preferred_element_type=jnp.float32)
    s = jnp.where(valid[None, None, :, None], s, NEG_INF)
    m = jnp.max(s, axis=-1, keepdims=True)
    p = jnp.exp(s - m)
    den = jnp.sum(p, axis=-1)
    o = jnp.einsum('brnhqk,brnkhd->brnqhd', p, vb) / den.transpose(0, 1, 2, 4, 3)[..., None]
    lse = (m[..., 0] + jnp.log(den)).transpose(0, 1, 2, 4, 3)
    o = o.reshape(B, dil, n_blk * ATT_BLOCK, H, Dh)[:, :, :L]
    o = o.transpose(0, 2, 1, 3, 4).reshape(B, S, H, Dh)
    lse = lse.reshape(B, dil, n_blk * ATT_BLOCK, H)[:, :, :L]
    lse = lse.transpose(0, 2, 1, 3).reshape(B, S, H)
    return o, lse


def dilated_attention_group(y, positions):
    B, S, _ = y.shape
    q, k, v = (_heads(t) for t in _split(y, ATT_SPLITS))
    q = partial_rotary(q, positions) * HEAD_DIM ** -0.5
    k = partial_rotary(k, positions)
    outs, lses = [], []
    for window, dil in DILATED_BRANCHES:
        o, lse = dilated_branch(q, k, v, window, dil)
        outs.append(o)
        lses.append(lse)
    alpha = jax.nn.softmax(jnp.stack(lses), axis=0)
    out = jnp.einsum('gbsh,gbshd->bshd', alpha, jnp.stack(outs))
    return out.reshape(B, S, ATT_W)


def _gla_chunked(q, k, v, lg):
    B, S, H, K = q.shape
    V = v.shape[-1]
    C = GLA_CHUNK
    N = S // C
    q, k, v, lg = (t.reshape(B, N, C, H, t.shape[-1]) for t in (q, k, v, lg))
    b = jnp.cumsum(lg, axis=2)
    b_last = b[:, :, -1]
    q_in = q * jnp.exp(b)
    k_in = k * jnp.exp(-b)
    k_end = k * jnp.exp(b_last[:, :, None] - b)
    causal = jnp.tril(jnp.ones((C, C), dtype=bool))
    att = jnp.where(causal, jnp.einsum('bnihk,bnjhk->bnhij', q_in, k_in), 0.0)
    o_intra = jnp.einsum('bnhij,bnjhv->bnihv', att, v)
    kv = jnp.einsum('bnjhk,bnjhv->bnhkv', k_end, v)

    def step(state, inp):
        kv_n, decay_n = inp
        return state * decay_n[..., None] + kv_n, state

    _, s_before = lax.scan(step, jnp.zeros((B, H, K, V), jnp.float32),
                           (kv.transpose(1, 0, 2, 3, 4), jnp.exp(b_last).transpose(1, 0, 2, 3)))
    o_inter = jnp.einsum('bnihk,nbhkv->bnihv', q_in, s_before)
    return (o_intra + o_inter).reshape(B, S, H, V)


def gla_group(y, gate_up, gate_b, norm_g):
    B, S, _ = y.shape
    yf = y.astype(jnp.float32)
    q, k, v, gd, og = _split(yf, GLA_SPLITS)
    lg = jax.nn.log_sigmoid(gd @ gate_up + gate_b) / GLA_GATE_TAU
    o = _gla_chunked(_heads(q) * HEAD_DIM ** -0.5, _heads(k), _heads(v), _heads(lg))
    o = o * lax.rsqrt(jnp.mean(o * o, axis=-1, keepdims=True) + RMS_EPS) * norm_g
    return o.reshape(B, S, GLA_W) * jax.nn.silu(og)


def swiglu(h, w_gate, w_up, w_down):
    return (jax.nn.silu(h @ w_gate) * (h @ w_up)) @ w_down


def moe_swiglu(h, w_router, w_gate, w_up, w_down):
    logits = jnp.einsum('bsd,de->bse', h, w_router, preferred_element_type=jnp.float32)
    top_val, top_idx = lax.top_k(logits, TOP_K)
    gates = jax.nn.softmax(top_val, axis=-1)
    combine = jnp.sum(jax.nn.one_hot(top_idx, N_EXPERTS, dtype=jnp.float32) * gates[..., None], axis=-2)
    combine = combine.astype(h.dtype)
    out = jnp.zeros_like(h)
    for e in range(N_EXPERTS):
        out = out + combine[..., e:e + 1] * swiglu(h, w_gate[e], w_up[e], w_down[e])
    return out


def setup_inputs(seed: int = 0) -> dict:
    key = jax.random.key(seed)
    ks = jax.random.split(key, 32)

    def nrm(i, shape, scale):
        return jax.random.normal(ks[i], shape, jnp.float32) * scale

    L = DEPTH
    n_dense = (DEPTH + 1) // 2
    n_moe = DEPTH // 2
    D = D_MODEL
    x = nrm(0, (BATCH, SEQ, D), 1.0)
    positions = (jax.random.randint(ks[1], (BATCH, 1), 0, 8192, dtype=jnp.int32)
                 + jnp.arange(SEQ, dtype=jnp.int32)[None, :])
    return {
        'x': x,
        'positions': positions,
        'mix_norm_g': 1.0 + nrm(2, (L, D), 0.02),
        'w_in': nrm(3, (L, D, IN_COLS), D ** -0.5),
        'rwkv_mu': jax.random.uniform(ks[4], (L, RWKV_COLS), jnp.float32),
        'rwkv_w0': nrm(5, (L, RWKV_W), 0.5),
        'rwkv_w2': nrm(6, (L, RWKV_DECAY_RANK, RWKV_W), 0.1),
        'rwkv_a0': nrm(7, (L, RWKV_W), 0.1),
        'rwkv_a2': nrm(8, (L, RWKV_AAA_RANK, RWKV_W), RWKV_AAA_RANK ** -0.5),
        'rwkv_g2': nrm(9, (L, RWKV_GATE_RANK, RWKV_W), RWKV_GATE_RANK ** -0.5),
        'rwkv_k_k': 0.85 + nrm(10, (L, RWKV_W), 0.05),
        'rwkv_k_a': 1.0 + nrm(11, (L, RWKV_W), 0.05),
        'rwkv_r_k': nrm(12, (L, RWKV_H, HEAD_DIM), 0.1),
        'rwkv_ln_g': 1.0 + nrm(13, (L, RWKV_W), 0.02),
        'rwkv_ln_b': nrm(14, (L, RWKV_W), 0.02),
        'rwkv_v0': 1.0 + nrm(15, (L - 1, RWKV_W), 0.1),
        'rwkv_v1': nrm(16, (L - 1, RWKV_W, RWKV_VRES_RANK), RWKV_W ** -0.5),
        'rwkv_v2': nrm(17, (L - 1, RWKV_VRES_RANK, RWKV_W), 0.1 * RWKV_VRES_RANK ** -0.5),
        'gla_gate_up': nrm(18, (L, GLA_GATE_RANK, GLA_W), GLA_GATE_RANK ** -0.5),
        'gla_gate_b': nrm(19, (L, GLA_W), 0.1),
        'gla_norm_g': 1.0 + nrm(20, (L, HEAD_DIM), 0.02),
        'w_out': nrm(21, (L, D, D), D ** -0.5),
        'ffn_norm_g': 1.0 + nrm(22, (L, D), 0.02),
        'ffn_w_gate': nrm(23, (n_dense, D, FFN_DENSE), D ** -0.5),
        'ffn_w_up': nrm(24, (n_dense, D, FFN_DENSE), D ** -0.5),
        'ffn_w_down': nrm(25, (n_dense, FFN_DENSE, D), FFN_DENSE ** -0.5),
        'moe_router': nrm(26, (n_moe, D, N_EXPERTS), D ** -0.5),
        'moe_w_gate': nrm(27, (n_moe, N_EXPERTS, D, FFN_EXPERT), D ** -0.5),
        'moe_w_up': nrm(28, (n_moe, N_EXPERTS, D, FFN_EXPERT), D ** -0.5),
        'moe_w_down': nrm(29, (n_moe, N_EXPERTS, FFN_EXPERT, D), FFN_EXPERT ** -0.5),
        'final_norm_g': 1.0 + nrm(30, (D,), 0.02),
    }


def reference(x, positions, mix_norm_g, w_in, rwkv_mu, rwkv_w0, rwkv_w2, rwkv_a0, rwkv_a2, rwkv_g2,
              rwkv_k_k, rwkv_k_a, rwkv_r_k, rwkv_ln_g, rwkv_ln_b, rwkv_v0, rwkv_v1, rwkv_v2,
              gla_gate_up, gla_gate_b, gla_norm_g, w_out, ffn_norm_g, ffn_w_gate, ffn_w_up, ffn_w_down,
              moe_router, moe_w_gate, moe_w_up, moe_w_down, final_norm_g):
    h = x
    v_first = None
    for layer in range(DEPTH):
        u = rms_norm(h, mix_norm_g[layer])
        proj = u @ w_in[layer]
        y_rwkv = proj[..., :RWKV_COLS]
        y_att = proj[..., RWKV_COLS:RWKV_COLS + ATT_COLS]
        y_gla = proj[..., RWKV_COLS + ATT_COLS:]
        vres = None if layer == 0 else (rwkv_v0[layer - 1], rwkv_v1[layer - 1], rwkv_v2[layer - 1])
        o_rwkv, v_first = rwkv7_group(y_rwkv, rwkv_mu[layer], rwkv_w0[layer], rwkv_w2[layer],
                                      rwkv_a0[layer], rwkv_a2[layer], rwkv_g2[layer], rwkv_k_k[layer],
                                      rwkv_k_a[layer], rwkv_r_k[layer], rwkv_ln_g[layer], rwkv_ln_b[layer],
                                      v_first, vres)
        o_att = dilated_attention_group(y_att, positions)
        o_gla = gla_group(y_gla, gla_gate_up[layer], gla_gate_b[layer], gla_norm_g[layer])
        mixed = jnp.concatenate([o_rwkv, o_att, o_gla], axis=-1).astype(h.dtype)
        h = h + mixed @ w_out[layer]
        u = rms_norm(h, ffn_norm_g[layer])
        i = layer // 2
        if layer % 2 == 0:
            h = h + swiglu(u, ffn_w_gate[i], ffn_w_up[i], ffn_w_down[i])
        else:
            h = h + moe_swiglu(u, moe_router[i], moe_w_gate[i], moe_w_up[i], moe_w_down[i])
    return rms_norm(h, final_norm_g)
```

```python
import functools

import jax
import jax.numpy as jnp
from jax import lax
from jax.experimental import pallas as pl
from jax.experimental.pallas import tpu as pltpu

D_MODEL = 1024
DEPTH = 2
HEAD_DIM = 64
RWKV_W = 256
RWKV_H = 4
ATT_W = 384
ATT_H = 6
GLA_W = 384
GLA_H = 6
RWKV_DECAY_RANK = 32
RWKV_AAA_RANK = 32
RWKV_GATE_RANK = 64
RWKV_VRES_RANK = 16
RWKV_GN_EPS = 64e-5
GLA_GATE_RANK = 16
GLA_GATE_TAU = 16.0
GLA_CHUNK = 64
DILATED_BRANCHES = ((128, 1), (512, 4), (2048, 16))
ATT_BLOCK = 128
ROPE_THETA = 500000.0
ROPE_DIMS = 16
ROPE_HALF = 8
FFN_DENSE = 2816
N_EXPERTS = 8
FFN_EXPERT = 3584
RMS_EPS = 1e-5
NEG_INF = -1e30

RWKV_SPLITS = (RWKV_W, RWKV_W, RWKV_W, RWKV_DECAY_RANK, RWKV_AAA_RANK, RWKV_GATE_RANK)
RWKV_COLS = 3 * RWKV_W + RWKV_DECAY_RANK + RWKV_AAA_RANK + RWKV_GATE_RANK
ATT_COLS = 3 * ATT_W
GLA_COLS = 4 * GLA_W + GLA_GATE_RANK
LANES = 128
GLA_COLS_PAD = 4 * GLA_W + LANES

VMEM_LIMIT = 56 * 1024 * 1024


def _cparams(*sem):
    return pltpu.CompilerParams(dimension_semantics=sem, vmem_limit_bytes=VMEM_LIMIT)


def _rms(x, g):
    return x * lax.rsqrt(jnp.mean(x * x, axis=-1, keepdims=True) + RMS_EPS) * g


def _rope_table_kernel(pos_ref, invf_ref, cos_ref, sa_ref, sb_ref):
    ang = pos_ref[...].astype(jnp.float32) * invf_ref[...]
    lane = lax.broadcasted_iota(jnp.int32, ang.shape, 1) % HEAD_DIM
    c = jnp.cos(ang)
    s = jnp.sin(ang)
    cos_ref[...] = jnp.where(lane < ROPE_DIMS, c, 1.0)
    sa_ref[...] = jnp.where((lane >= ROPE_HALF) & (lane < ROPE_DIMS), s, 0.0)
    sb_ref[...] = jnp.where(lane < ROPE_HALF, -s, 0.0)


def _rope_tables(positions):
    m = positions.size
    tm = 2048
    lane = jnp.arange(LANES) % ROPE_HALF
    invf = (ROPE_THETA ** (-lane.astype(jnp.float32) / ROPE_HALF)).reshape(1, LANES)
    out = jax.ShapeDtypeStruct((m, LANES), jnp.float32)
    row = pl.BlockSpec((tm, LANES), lambda i: (i, 0))
    return pl.pallas_call(
        _rope_table_kernel,
        grid=(m // tm,),
        in_specs=[pl.BlockSpec((tm, 1), lambda i: (i, 0)), pl.BlockSpec((1, LANES), lambda i: (0, 0))],
        out_specs=[row, row, row],
        out_shape=[out, out, out],
        compiler_params=_cparams("parallel"),
        name="rope_tables",
    )(positions.reshape(m, 1), invf)


def _inproj_kernel(x_ref, g_ref, wr_ref, wa_ref, wg_ref, cos_ref, sa_ref, sb_ref, yr_ref, ya_ref, yg_ref):
    xb = _rms(x_ref[...], g_ref[...]).astype(jnp.bfloat16)
    yr_ref[...] = jnp.dot(xb, wr_ref[...], preferred_element_type=jnp.float32)
    yg_ref[...] = jnp.dot(xb, wg_ref[...], preferred_element_type=jnp.float32)
    ya = jnp.dot(xb, wa_ref[...], preferred_element_type=jnp.float32)
    cos, sa, sb = cos_ref[...], sa_ref[...], sb_ref[...]
    for j in range(2 * ATT_W // LANES):
        blk = ya[:, j * LANES:(j + 1) * LANES]
        rot = blk * cos + pltpu.roll(blk, ROPE_HALF, 1) * sa + pltpu.roll(blk, LANES - ROPE_HALF, 1) * sb
        if j < ATT_W // LANES:
            rot = rot * HEAD_DIM ** -0.5
        ya_ref[:, j * LANES:(j + 1) * LANES] = rot.astype(ya_ref.dtype)
    ya_ref[:, 2 * ATT_W:] = ya[:, 2 * ATT_W:].astype(ya_ref.dtype)


def _inproj(h, g, wr, wa, wg, cos, sa, sb):
    m = h.shape[0]
    tm = 512
    row = lambda w: pl.BlockSpec((tm, w), lambda i: (i, 0))
    full = lambda a: pl.BlockSpec(a.shape, lambda i: (0, 0))
    return pl.pallas_call(
        _inproj_kernel,
        grid=(m // tm,),
        in_specs=[row(D_MODEL), full(g), full(wr), full(wa), full(wg), row(LANES), row(LANES), row(LANES)],
        out_specs=[row(RWKV_COLS), row(ATT_COLS), row(GLA_COLS_PAD)],
        out_shape=[jax.ShapeDtypeStruct((m, RWKV_COLS), jnp.float32),
                   jax.ShapeDtypeStruct((m, ATT_COLS), jnp.float32),
                   jax.ShapeDtypeStruct((m, GLA_COLS_PAD), jnp.float32)],
        compiler_params=_cparams("parallel"),
        name="inproj",
    )(h, g, wr, wa, wg, cos, sa, sb)


def _outproj_kernel(mix_ref, w_ref, h_ref, o_ref):
    o_ref[...] = h_ref[...] + jnp.dot(mix_ref[...].astype(jnp.bfloat16), w_ref[...],
                                      preferred_element_type=jnp.float32)


def _outproj(mixed, w, h):
    m = h.shape[0]
    tm = 512
    row = pl.BlockSpec((tm, D_MODEL), lambda i: (i, 0))
    return pl.pallas_call(
        _outproj_kernel,
        grid=(m // tm,),
        in_specs=[row, pl.BlockSpec(w.shape, lambda i: (0, 0)), row],
        out_specs=row,
        out_shape=jax.ShapeDtypeStruct(h.shape, jnp.float32),
        compiler_params=_cparams("parallel"),
        name="outproj",
    )(mixed, w, h)


def _ffn_kernel(x_ref, g_ref, wg_ref, wu_ref, wd_ref, o_ref, xn_ref, acc_ref):
    f = pl.program_id(1)

    @pl.when(f == 0)
    def _():
        xn_ref[...] = _rms(x_ref[...], g_ref[...]).astype(xn_ref.dtype)
        acc_ref[...] = jnp.zeros_like(acc_ref)

    xn = xn_ref[...]
    gate = jnp.dot(xn, wg_ref[...], preferred_element_type=jnp.float32)
    up = jnp.dot(xn, wu_ref[...], preferred_element_type=jnp.float32)
    act = (gate * jax.nn.sigmoid(gate) * up).astype(jnp.bfloat16)
    acc_ref[...] += jnp.dot(act, wd_ref[...], preferred_element_type=jnp.float32)

    @pl.when(f == pl.num_programs(1) - 1)
    def _():
        o_ref[...] = x_ref[...] + acc_ref[...]


def _ffn(h, g, wg, wu, wd):
    m = h.shape[0]
    tm, tf = 512, FFN_DENSE // 2
    row = pl.BlockSpec((tm, D_MODEL), lambda i, f: (i, 0))
    return pl.pallas_call(
        _ffn_kernel,
        grid=(m // tm, FFN_DENSE // tf),
        in_specs=[row, pl.BlockSpec(g.shape, lambda i, f: (0, 0)),
                  pl.BlockSpec((D_MODEL, tf), lambda i, f: (0, f)),
                  pl.BlockSpec((D_MODEL, tf), lambda i, f: (0, f)),
                  pl.BlockSpec((tf, D_MODEL), lambda i, f: (f, 0))],
        out_specs=row,
        out_shape=jax.ShapeDtypeStruct(h.shape, jnp.float32),
        scratch_shapes=[pltpu.VMEM((tm, D_MODEL), jnp.bfloat16), pltpu.VMEM((tm, D_MODEL), jnp.float32)],
        compiler_params=_cparams("parallel", "arbitrary"),
        name="ffn_dense",
    )(h, g, wg, wu, wd)


def _moe_kernel(x_ref, g_ref, wr_ref, wg_ref, wu_ref, wd_ref, fg_ref, o_ref, xn_ref, comb_ref, acc_ref,
                *, final_norm):
    e = pl.program_id(1)
    f = pl.program_id(2)

    @pl.when((e == 0) & (f == 0))
    def _():
        xn = _rms(x_ref[...], g_ref[...])
        xn_ref[...] = xn.astype(xn_ref.dtype)
        acc_ref[...] = jnp.zeros_like(acc_ref)
        logits = jnp.dot(xn, wr_ref[...], preferred_element_type=jnp.float32, precision=lax.Precision.HIGHEST)
        lane = lax.broadcasted_iota(jnp.int32, logits.shape, 1)
        lg = jnp.where(lane < N_EXPERTS, logits, NEG_INF)
        m1 = jnp.max(lg, axis=-1, keepdims=True)
        i1 = jnp.min(jnp.where(lg == m1, lane, LANES), axis=-1, keepdims=True)
        lg2 = jnp.where(lane == i1, NEG_INF, lg)
        m2 = jnp.max(lg2, axis=-1, keepdims=True)
        i2 = jnp.min(jnp.where(lg2 == m2, lane, LANES), axis=-1, keepdims=True)
        e2 = jnp.exp(m2 - m1)
        comb_ref[...] = jnp.where(lane == i1, 1.0 / (1.0 + e2), jnp.where(lane == i2, e2 / (1.0 + e2), 0.0))

    comb = comb_ref[...]
    lane = lax.broadcasted_iota(jnp.int32, comb.shape, 1)
    c = jnp.sum(jnp.where(lane == e, comb, 0.0), axis=-1, keepdims=True)
    xn = xn_ref[...]
    gate = jnp.dot(xn, wg_ref[0], preferred_element_type=jnp.float32)
    up = jnp.dot(xn, wu_ref[0], preferred_element_type=jnp.float32)
    act = (gate * jax.nn.sigmoid(gate) * up * c).astype(jnp.bfloat16)
    acc_ref[...] += jnp.dot(act, wd_ref[0], preferred_element_type=jnp.float32)

    @pl.when((e == pl.num_programs(1) - 1) & (f == pl.num_programs(2) - 1))
    def _():
        y = x_ref[...] + acc_ref[...]
        o_ref[...] = _rms(y, fg_ref[...]) if final_norm else y


def _moe(h, g, w_router, wg, wu, wd, final_g, final_norm):
    m = h.shape[0]
    tm, tf = 1024, 512
    row = pl.BlockSpec((tm, D_MODEL), lambda i, e, f: (i, 0))
    vec = pl.BlockSpec((1, D_MODEL), lambda i, e, f: (0, 0))
    return pl.pallas_call(
        functools.partial(_moe_kernel, final_norm=final_norm),
        grid=(m // tm, N_EXPERTS, FFN_EXPERT // tf),
        in_specs=[row, vec, pl.BlockSpec(w_router.shape, lambda i, e, f: (0, 0)),
                  pl.BlockSpec((1, D_MODEL, tf), lambda i, e, f: (e, 0, f)),
                  pl.BlockSpec((1, D_MODEL, tf), lambda i, e, f: (e, 0, f)),
                  pl.BlockSpec((1, tf, D_MODEL), lambda i, e, f: (e, f, 0)),
                  vec],
        out_specs=row,
        out_shape=jax.ShapeDtypeStruct(h.shape, jnp.float32),
        scratch_shapes=[pltpu.VMEM((tm, D_MODEL), jnp.bfloat16), pltpu.VMEM((tm, LANES), jnp.float32),
                        pltpu.VMEM((tm, D_MODEL), jnp.float32)],
        compiler_params=_cparams("parallel", "arbitrary", "arbitrary"),
        name="moe",
    )(h, g, w_router, wg, wu, wd, final_g)


def _heads(t):
    return t.reshape(t.shape[:-1] + (t.shape[-1] // HEAD_DIM, HEAD_DIM))


def _split(t, sizes):
    parts, start = [], 0
    for size in sizes:
        parts.append(t[..., start:start + size])
        start += size
    return parts


def _token_shift(y):
    return jnp.pad(y, ((0, 0), (1, 0), (0, 0)))[:, :-1]


def _rwkv7_scan(r, w, k, v, a, b):
    B, S, H, N = r.shape

    def step(state, inp):
        r_t, w_t, k_t, v_t, a_t, b_t = inp
        sa = jnp.einsum('bhvk,bhk->bhv', state, a_t)
        state = (state * w_t[:, :, None, :] + sa[..., None] * b_t[:, :, None, :]
                 + v_t[..., None] * k_t[:, :, None, :])
        return state, jnp.einsum('bhvk,bhk->bhv', state, r_t)

    xs = tuple(t.transpose(1, 0, 2, 3) for t in (r, w, k, v, a, b))
    _, out = lax.scan(step, jnp.zeros((B, H, N, N), jnp.float32), xs)
    return out.transpose(1, 0, 2, 3)


def _rwkv7_group(y, mu, w0, w2, a0, a2, g2, k_k, k_a, r_k, ln_g, ln_b, v_first, vres):
    B, S, _ = y.shape
    yf = y.astype(jnp.float32)
    ym = yf + (_token_shift(yf) - yf) * mu
    r, k, v, xw, xa, xg = _split(ym, RWKV_SPLITS)
    w_log = -jax.nn.softplus(-(w0 + jnp.tanh(xw) @ w2)) - 0.5
    decay = jnp.exp(-jnp.exp(w_log))
    a = jax.nn.sigmoid(a0 + xa @ a2)
    g = jax.nn.sigmoid(xg) @ g2
    if vres is None:
        v_first = v
    else:
        v0, v1, v2 = vres
        v = v + (v_first - v) * jax.nn.sigmoid(v0 + (v @ v1) @ v2)
    r, k, v, decay, a = (_heads(t) for t in (r, k, v, decay, a))
    kk = k * _heads(k_k)
    kk = kk * lax.rsqrt(jnp.maximum(jnp.sum(kk * kk, axis=-1, keepdims=True), 1e-24))
    k = k * (1.0 + (a - 1.0) * _heads(k_a))
    o = _rwkv7_scan(r, decay, k, v, -kk, kk * a)
    mean = jnp.mean(o, axis=-1, keepdims=True)
    var = jnp.mean(jnp.square(o - mean), axis=-1, keepdims=True)
    o = ((o - mean) * lax.rsqrt(var + RWKV_GN_EPS)).reshape(B, S, RWKV_W) * ln_g + ln_b
    bonus = (jnp.sum(r * k * r_k, axis=-1, keepdims=True) * v).reshape(B, S, RWKV_W)
    return (o + bonus) * g, v_first


def _to_residue_blocks(t, dil, n_blk):
    B, S, H, Dh = t.shape
    L = S // dil
    t = t.reshape(B, L, dil, H, Dh).transpose(0, 2, 1, 3, 4)
    t = jnp.pad(t, ((0, 0), (0, 0), (0, n_blk * ATT_BLOCK - L), (0, 0), (0, 0)))
    return t.reshape(B, dil, n_blk, ATT_BLOCK, H, Dh)


def _with_previous_block(t):
    prev = jnp.pad(t, ((0, 0), (0, 0), (1, 0), (0, 0), (0, 0), (0, 0)))[:, :, :-1]
    return jnp.concatenate([prev, t], axis=3)


def _dilated_branch(q, k, v, window, dil):
    B, S, H, Dh = q.shape
    L = S // dil
    n_blk = -(-L // ATT_BLOCK)
    span = window // dil
    qb = _to_residue_blocks(q, dil, n_blk)
    kb = _with_previous_block(_to_residue_blocks(k, dil, n_blk))
    vb = _with_previous_block(_to_residue_blocks(v, dil, n_blk)).astype(jnp.float32)
    qi = jnp.arange(ATT_BLOCK)[:, None]
    kc = jnp.arange(2 * ATT_BLOCK)[None, :]
    rel = qi + ATT_BLOCK - kc
    key_idx = jnp.arange(n_blk)[:, None, None] * ATT_BLOCK + kc[None] - ATT_BLOCK
    valid = (rel >= 0) & (rel <= span) & (key_idx >= 0)
    s = jnp.einsum('brnqhd,brnkhd->brnhqk', qb, kb, preferred_element_type=jnp.float32)
    s = jnp.where(valid[None, None, :, None], s, NEG_INF)
    m = jnp.max(s, axis=-1, keepdims=True)
    p = jnp.exp(s - m)
    den = jnp.sum(p, axis=-1)
    o = jnp.einsum('brnhqk,brnkhd->brnqhd', p, vb) / den.transpose(0, 1, 2, 4, 3)[..., None]
    lse = (m[..., 0] + jnp.log(den)).transpose(0, 1, 2, 4, 3)
    o = o.reshape(B, dil, n_blk * ATT_BLOCK, H, Dh)[:, :, :L]
    o = o.transpose(0, 2, 1, 3, 4).reshape(B, S, H, Dh)
    lse = lse.reshape(B, dil, n_blk * ATT_BLOCK, H)[:, :, :L]
    lse = lse.transpose(0, 2, 1, 3).reshape(B, S, H)
    return o, lse


def _dilated_attention_group(y):
    B, S, _ = y.shape
    q, k, v = (_heads(t) for t in _split(y, (ATT_W, ATT_W, ATT_W)))
    outs, lses = [], []
    for window, dil in DILATED_BRANCHES:
        o, lse = _dilated_branch(q, k, v, window, dil)
        outs.append(o)
        lses.append(lse)
    alpha = jax.nn.softmax(jnp.stack(lses), axis=0)
    out = jnp.einsum('gbsh,gbshd->bshd', alpha, jnp.stack(outs))
    return out.reshape(B, S, ATT_W)


def _gla_chunked(q, k, v, lg):
    B, S, H, K = q.shape
    V = v.shape[-1]
    C = GLA_CHUNK
    N = S // C
    q, k, v, lg = (t.reshape(B, N, C, H, t.shape[-1]) for t in (q, k, v, lg))
    b = jnp.cumsum(lg, axis=2)
    b_last = b[:, :, -1]
    q_in = q * jnp.exp(b)
    k_in = k * jnp.exp(-b)
    k_end = k * jnp.exp(b_last[:, :, None] - b)
    causal = jnp.tril(jnp.ones((C, C), dtype=bool))
    att = jnp.where(causal, jnp.einsum('bnihk,bnjhk->bnhij', q_in, k_in), 0.0)
    o_intra = jnp.einsum('bnhij,bnjhv->bnihv', att, v)
    kv = jnp.einsum('bnjhk,bnjhv->bnhkv', k_end, v)

    def step(state, inp):
        kv_n, decay_n = inp
        return state * decay_n[..., None] + kv_n, state

    _, s_before = lax.scan(step, jnp.zeros((B, H, K, V), jnp.float32),
                           (kv.transpose(1, 0, 2, 3, 4), jnp.exp(b_last).transpose(1, 0, 2, 3)))
    o_inter = jnp.einsum('bnihk,nbhkv->bnihv', q_in, s_before)
    return (o_intra + o_inter).reshape(B, S, H, V)


def _gla_group(y, gate_up, gate_b, norm_g):
    B, S, _ = y.shape
    q, k, v, og = (y[..., i * GLA_W:(i + 1) * GLA_W] for i in range(4))
    gd = y[..., 4 * GLA_W:4 * GLA_W + GLA_GATE_RANK]
    lg = jax.nn.log_sigmoid(gd @ gate_up + gate_b) / GLA_GATE_TAU
    o = _gla_chunked(_heads(q) * HEAD_DIM ** -0.5, _heads(k), _heads(v), _heads(lg))
    o = o * lax.rsqrt(jnp.mean(o * o, axis=-1, keepdims=True) + RMS_EPS) * norm_g
    return o.reshape(B, S, GLA_W) * jax.nn.silu(og)


def kernel(x, positions, mix_norm_g, w_in, rwkv_mu, rwkv_w0, rwkv_w2, rwkv_a0, rwkv_a2, rwkv_g2, rwkv_k_k, rwkv_k_a, rwkv_r_k, rwkv_ln_g, rwkv_ln_b, rwkv_v0, rwkv_v1, rwkv_v2, gla_gate_up, gla_gate_b, gla_norm_g, w_out, ffn_norm_g, ffn_w_gate, ffn_w_up, ffn_w_down, moe_router, moe_w_gate, moe_w_up, moe_w_down, final_norm_g):
    B, S, D = x.shape
    M = B * S
    bf = jnp.bfloat16
    h = x.reshape(M, D)
    cos, sa, sb = _rope_tables(positions)
    v_first = None
    for layer in range(DEPTH):
        w = w_in[layer]
        wr = w[:, :RWKV_COLS].astype(bf)
        wa = w[:, RWKV_COLS:RWKV_COLS + ATT_COLS].astype(bf)
        wgl = w[:, RWKV_COLS + ATT_COLS:]
        wgl = jnp.concatenate([wgl[:, :3 * GLA_W], wgl[:, 3 * GLA_W + GLA_GATE_RANK:],
                               wgl[:, 3 * GLA_W:3 * GLA_W + GLA_GATE_RANK],
                               jnp.zeros((D, LANES - GLA_GATE_RANK), w.dtype)], axis=1).astype(bf)
        y_rwkv, y_att, y_gla = _inproj(h, mix_norm_g[layer].reshape(1, D), wr, wa, wgl, cos, sa, sb)
        vres = None if layer == 0 else (rwkv_v0[layer - 1], rwkv_v1[layer - 1], rwkv_v2[layer - 1])
        o_rwkv, v_first = _rwkv7_group(y_rwkv.reshape(B, S, -1), rwkv_mu[layer], rwkv_w0[layer], rwkv_w2[layer],
                                       rwkv_a0[layer], rwkv_a2[layer], rwkv_g2[layer], rwkv_k_k[layer],
                                       rwkv_k_a[layer], rwkv_r_k[layer], rwkv_ln_g[layer], rwkv_ln_b[layer],
                                       v_first, vres)
        o_att = _dilated_attention_group(y_att.reshape(B, S, -1))
        o_gla = _gla_group(y_gla.reshape(B, S, -1), gla_gate_up[layer], gla_gate_b[layer], gla_norm_g[layer])
        mixed = jnp.concatenate([o_rwkv, o_att, o_gla], axis=-1).reshape(M, D)
        h = _outproj(mixed, w_out[layer].astype(bf), h)
        i = layer // 2
        g = ffn_norm_g[layer].reshape(1, D)
        if layer % 2 == 0:
            h = _ffn(h, g, ffn_w_gate[i].astype(bf), ffn_w_up[i].astype(bf), ffn_w_down[i].astype(bf))
        else:
            wrt = jnp.pad(moe_router[i], ((0, 0), (0, LANES - N_EXPERTS)))
            h = _moe(h, g, wrt, moe_w_gate[i].astype(bf), moe_w_up[i].astype(bf), moe_w_down[i].astype(bf),
                     final_norm_g.reshape(1, D), layer == DEPTH - 1)
    return h.reshape(B, S, D)
```

```python
import functools

import jax
import jax.numpy as jnp
from jax import lax
from jax.experimental import pallas as pl
from jax.experimental.pallas import tpu as pltpu

D_MODEL = 1024
DEPTH = 2
HEAD_DIM = 64
RWKV_W = 256
RWKV_H = 4
ATT_W = 384
GLA_W = 384
GLA_H = 6
RWKV_DECAY_RANK = 32
RWKV_AAA_RANK = 32
RWKV_GATE_RANK = 64
RWKV_GN_EPS = 64e-5
GLA_GATE_RANK = 16
GLA_GATE_TAU = 16.0
CHUNK = 64
DILATIONS = (1, 4, 16)
ATT_BLOCK = 128
ROPE_THETA = 500000.0
ROPE_DIMS = 16
ROPE_HALF = 8
FFN_DENSE = 2816
N_EXPERTS = 8
FFN_EXPERT = 3584
RMS_EPS = 1e-5
NEG_INF = -1e30

RWKV_COLS = 3 * RWKV_W + RWKV_DECAY_RANK + RWKV_AAA_RANK + RWKV_GATE_RANK
ATT_COLS = 3 * ATT_W
LANES = 128
GLA_COLS_PAD = 4 * GLA_W + LANES

VMEM_LIMIT = 56 * 1024 * 1024

F32 = jnp.float32
BF16 = jnp.bfloat16
NN = (((1,), (0,)), ((), ()))
NT = (((1,), (1,)), ((), ()))
TN = (((0,), (0,)), ((), ()))


def _cparams(*sem):
    return pltpu.CompilerParams(dimension_semantics=sem, vmem_limit_bytes=VMEM_LIMIT)


def _rms(x, g):
    return x * lax.rsqrt(jnp.mean(x * x, axis=-1, keepdims=True) + RMS_EPS) * g


def _sigmoid(x):
    return 1.0 / (1.0 + jnp.exp(-x))


def _log_sigmoid(x):
    return jnp.minimum(x, 0.0) - jnp.log(1.0 + jnp.exp(-jnp.abs(x)))


def _split_hi_lo(a):
    hi = a.astype(BF16)
    return hi, (a - hi.astype(F32)).astype(BF16)


def _dot3(a, b, dims=NN):
    a_hi, a_lo = _split_hi_lo(a)
    b_hi, b_lo = _split_hi_lo(b)
    d = lambda x, y: lax.dot_general(x, y, dims, preferred_element_type=F32)
    return d(a_hi, b_hi) + d(a_lo, b_hi) + d(a_hi, b_lo)


def _dot_exact_lhs(l_bf16, x):
    x_hi, x_lo = _split_hi_lo(x)
    return jnp.dot(l_bf16, x_hi, preferred_element_type=F32) + jnp.dot(l_bf16, x_lo, preferred_element_type=F32)


def _dot_exact_rhs(x, r_bf16):
    x_hi, x_lo = _split_hi_lo(x)
    return jnp.dot(x_hi, r_bf16, preferred_element_type=F32) + jnp.dot(x_lo, r_bf16, preferred_element_type=F32)


def _head_ones(width):
    ri = lax.broadcasted_iota(jnp.int32, (width, width), 0)
    ci = lax.broadcasted_iota(jnp.int32, (width, width), 1)
    return (ri // HEAD_DIM == ci // HEAD_DIM).astype(BF16)


def _chunk_selectors(tile):
    ti = lax.broadcasted_iota(jnp.int32, (tile, tile), 0)
    tj = lax.broadcasted_iota(jnp.int32, (tile, tile), 1)
    same_chunk = ti // CHUNK == tj // CHUNK
    return (same_chunk & (tj <= ti)).astype(BF16), same_chunk.astype(BF16)


def _rope_table_kernel(pos_ref, invf_ref, cos_ref, sa_ref, sb_ref):
    ang = pos_ref[...].astype(F32) * invf_ref[...]
    lane = lax.broadcasted_iota(jnp.int32, ang.shape, 1) % HEAD_DIM
    c = jnp.cos(ang)
    s = jnp.sin(ang)
    cos_ref[...] = jnp.where(lane < ROPE_DIMS, c, 1.0)
    sa_ref[...] = jnp.where((lane >= ROPE_HALF) & (lane < ROPE_DIMS), s, 0.0)
    sb_ref[...] = jnp.where(lane < ROPE_HALF, -s, 0.0)


def _rope_tables(positions):
    m = positions.size
    tm = 2048
    lane = jnp.arange(LANES) % ROPE_HALF
    invf = (ROPE_THETA ** (-lane.astype(F32) / ROPE_HALF)).reshape(1, LANES)
    out = jax.ShapeDtypeStruct((m, LANES), F32)
    row = pl.BlockSpec((tm, LANES), lambda i: (i, 0))
    return pl.pallas_call(
        _rope_table_kernel,
        grid=(m // tm,),
        in_specs=[pl.BlockSpec((tm, 1), lambda i: (i, 0)), pl.BlockSpec((1, LANES), lambda i: (0, 0))],
        out_specs=[row, row, row],
        out_shape=[out, out, out],
        compiler_params=_cparams("parallel"),
        name="rope_tables",
    )(positions.reshape(m, 1), invf)


def _inproj_kernel(x_ref, g_ref, wr_ref, wa_ref, wg_ref, cos_ref, sa_ref, sb_ref, yr_ref, ya_ref, yg_ref):
    xb = _rms(x_ref[...], g_ref[...]).astype(BF16)
    yr_ref[...] = jnp.dot(xb, wr_ref[...], preferred_element_type=F32)
    yg_ref[...] = jnp.dot(xb, wg_ref[...], preferred_element_type=F32)
    ya = jnp.dot(xb, wa_ref[...], preferred_element_type=F32)
    cos, sa, sb = cos_ref[...], sa_ref[...], sb_ref[...]
    for j in range(2 * ATT_W // LANES):
        blk = ya[:, j * LANES:(j + 1) * LANES]
        rot = blk * cos + pltpu.roll(blk, ROPE_HALF, 1) * sa + pltpu.roll(blk, LANES - ROPE_HALF, 1) * sb
        if j < ATT_W // LANES:
            rot = rot * HEAD_DIM ** -0.5
        ya_ref[:, j * LANES:(j + 1) * LANES] = rot.astype(ya_ref.dtype)
    ya_ref[:, 2 * ATT_W:] = ya[:, 2 * ATT_W:].astype(ya_ref.dtype)


def _inproj(h, g, wr, wa, wg, cos, sa, sb):
    m = h.shape[0]
    tm = 512
    row = lambda w: pl.BlockSpec((tm, w), lambda i: (i, 0))
    full = lambda a: pl.BlockSpec(a.shape, lambda i: (0, 0))
    return pl.pallas_call(
        _inproj_kernel,
        grid=(m // tm,),
        in_specs=[row(D_MODEL), full(g), full(wr), full(wa), full(wg), row(LANES), row(LANES), row(LANES)],
        out_specs=[row(RWKV_COLS), row(ATT_COLS), row(GLA_COLS_PAD)],
        out_shape=[jax.ShapeDtypeStruct((m, RWKV_COLS), F32),
                   jax.ShapeDtypeStruct((m, ATT_COLS), BF16),
                   jax.ShapeDtypeStruct((m, GLA_COLS_PAD), F32)],
        compiler_params=_cparams("parallel"),
        name="inproj",
    )(h, g, wr, wa, wg, cos, sa, sb)


def _rwkv_kernel(*refs, has_vres, tile):
    if has_vres:
        (y_ref, vf_ref, mu_ref, w0_ref, w2_ref, a0_ref, a2_ref, g2_ref, kk_ref, ka_ref, rk_ref, lng_ref, lnb_ref,
         v0_ref, v1_ref, v2_ref, o_ref,
         state_ref, prev_ref, at_ref, rt_ref, bt_ref, kt_ref, bh_ref, kh_ref, vv_ref, gam_ref, os_ref) = refs
    else:
        (y_ref, mu_ref, w0_ref, w2_ref, a0_ref, a2_ref, g2_ref, kk_ref, ka_ref, rk_ref, lng_ref, lnb_ref,
         o_ref, vfo_ref,
         state_ref, prev_ref, at_ref, rt_ref, bt_ref, kt_ref, bh_ref, kh_ref, vv_ref, gam_ref, os_ref) = refs

    @pl.when(pl.program_id(1) == 0)
    def _():
        state_ref[...] = jnp.zeros_like(state_ref)
        prev_ref[...] = jnp.zeros_like(prev_ref)

    y = y_ref[...]
    row = lax.broadcasted_iota(jnp.int32, (tile, 1), 0)
    ysh = jnp.where(row == 0, prev_ref[...], pltpu.roll(y, 1, 0))
    prev_ref[...] = y[tile - 1:tile, :]
    ym = y + (ysh - y) * mu_ref[...]
    r = ym[:, 0:RWKV_W]
    k = ym[:, RWKV_W:2 * RWKV_W]
    v = ym[:, 2 * RWKV_W:3 * RWKV_W]
    x6 = ym[:, 3 * RWKV_W:]
    wl = w0_ref[...] + _dot3(jnp.tanh(x6), w2_ref[...])
    lw = -jnp.exp(_log_sigmoid(wl) - 0.5)
    a = _sigmoid(a0_ref[...] + _dot3(x6, a2_ref[...]))
    g = _dot3(_sigmoid(x6), g2_ref[...])
    if has_vres:
        v = v + (vf_ref[...] - v) * _sigmoid(v0_ref[...] + _dot3(_dot3(v, v1_ref[...]), v2_ref[...]))
    else:
        vfo_ref[...] = v

    head_ones = _head_ones(RWKV_W)
    kk = k * kk_ref[...]
    kk = kk * lax.rsqrt(jnp.maximum(_dot_exact_rhs(kk * kk, head_ones), 1e-24))
    k2 = k * (1.0 + (a - 1.0) * ka_ref[...])
    bonus = _dot_exact_rhs(r * k2 * rk_ref[...], head_ones) * v

    prefix_sel, total_sel = _chunk_selectors(tile)
    c = _dot_exact_lhs(prefix_sel, lw)
    c_last = _dot_exact_lhs(total_sel, lw)
    b = kk * a
    e_neg = jnp.exp(-c)
    e_end = jnp.exp(c_last - c)
    rt_ref[...] = r * jnp.exp(c)
    at_ref[...] = -kk * jnp.exp(c - lw)
    bt_ref[...] = b * e_neg
    kt_ref[...] = k2 * e_neg
    bh_ref[...] = b * e_end
    kh_ref[...] = k2 * e_end
    vv_ref[...] = v
    gam_ref[...] = jnp.exp(c_last)

    qi = lax.broadcasted_iota(jnp.int32, (CHUNK, CHUNK), 0)
    qj = lax.broadcasted_iota(jnp.int32, (CHUNK, CHUNK), 1)
    strict = qj < qi
    incl = qj <= qi
    eye = qj == qi
    blk16 = qi // 16 == qj // 16
    blk32 = qi // 32 == qj // 32

    def chunk_body(ci, carry):
        rows = pl.ds(pl.multiple_of(ci * CHUNK, CHUNK), CHUNK)
        for h in range(RWKV_H):
            sl = slice(h * HEAD_DIM, (h + 1) * HEAD_DIM)
            at, rt, bt, kt = at_ref[rows, sl], rt_ref[rows, sl], bt_ref[rows, sl], kt_ref[rows, sl]
            bh, kh, vv = bh_ref[rows, sl], kh_ref[rows, sl], vv_ref[rows, sl]
            gam = gam_ref[rows, sl][0:1, :]
            x = jnp.concatenate([at, rt], axis=0)
            zb = _dot3(x, bt, NT)
            zk = _dot3(x, kt, NT)
            a_ab = jnp.where(strict, zb[:CHUNK], 0.0)
            a_ak = jnp.where(strict, zk[:CHUNK], 0.0)
            p_b = jnp.where(incl, zb[CHUNK:], 0.0)
            p_k = jnp.where(incl, zk[CHUNK:], 0.0)
            d1 = jnp.where(blk16, a_ab, 0.0)
            d2 = _dot3(d1, d1)
            d4 = _dot3(d2, d2)
            d8 = _dot3(d4, d4)
            t = jnp.where(eye, 1.0, d1)
            t = t + _dot3(t, d2)
            t = t + _dot3(t, d4)
            t = t + _dot3(t, d8)
            t = t + _dot3(_dot3(t, jnp.where(blk32 & ~blk16, a_ab, 0.0)), t)
            t = t + _dot3(_dot3(t, jnp.where(blk32, 0.0, a_ab)), t)
            w = _dot3(t, at)
            u_loc = _dot3(t, _dot3(a_ak, vv))
            q_t = rt + _dot3(p_b, w)
            o_loc = _dot3(p_b, u_loc) + _dot3(p_k, vv)
            g_mat = _dot3(w, bh, TN) + jnp.where(eye, gam, 0.0)
            h_mat = _dot3(u_loc, bh, TN) + _dot3(vv, kh, TN)
            s = state_ref[h]
            os_ref[rows, sl] = _dot3(q_t, s, NT) + o_loc
            state_ref[h] = _dot3(s, g_mat) + h_mat
        return carry

    lax.fori_loop(0, tile // CHUNK, chunk_body, 0)

    o = os_ref[...]
    mean = _dot_exact_rhs(o, head_ones) * (1.0 / HEAD_DIM)
    oc = o - mean
    var = _dot_exact_rhs(oc * oc, head_ones) * (1.0 / HEAD_DIM)
    on = oc * lax.rsqrt(var + RWKV_GN_EPS) * lng_ref[...] + lnb_ref[...]
    o_ref[...] = (on + bonus) * g


def _rwkv_mixer(y, seq_len, mu, w0, w2, a0, a2, g2, k_k, k_a, r_k, ln_g, ln_b, v_first, vres):
    m = y.shape[0]
    tile = 512
    nt = seq_len // tile
    has_vres = vres is not None
    row = lambda wd: pl.BlockSpec((tile, wd), lambda b, j: (b * nt + j, 0))
    full = lambda arr: pl.BlockSpec(arr.shape, lambda b, j: (0, 0))
    vec = lambda t: t.reshape(1, -1)
    pad_rows = lambda w, start: jnp.zeros((LANES, RWKV_W), F32).at[start:start + w.shape[0]].set(w)
    consts = [vec(mu), vec(w0), pad_rows(w2, 0), vec(a0), pad_rows(a2, RWKV_DECAY_RANK),
              pad_rows(g2, RWKV_DECAY_RANK + RWKV_AAA_RANK), vec(k_k), vec(k_a), vec(r_k), vec(ln_g), vec(ln_b)]
    args = [y] + ([v_first] if has_vres else []) + consts
    in_specs = [row(RWKV_COLS)] + ([row(RWKV_W)] if has_vres else []) + [full(c) for c in consts]
    out_shape = [jax.ShapeDtypeStruct((m, RWKV_W), F32)]
    out_specs = [row(RWKV_W)]
    if has_vres:
        v0, v1, v2 = vres
        extra = [vec(v0), jnp.pad(v1, ((0, 0), (0, LANES - v1.shape[1]))),
                 jnp.pad(v2, ((0, LANES - v2.shape[0]), (0, 0)))]
        args += extra
        in_specs += [full(c) for c in extra]
    else:
        out_shape.append(jax.ShapeDtypeStruct((m, RWKV_W), F32))
        out_specs.append(row(RWKV_W))
    tile_buf = pltpu.VMEM((tile, RWKV_W), F32)
    outs = pl.pallas_call(
        functools.partial(_rwkv_kernel, has_vres=has_vres, tile=tile),
        grid=(m // seq_len, nt),
        in_specs=in_specs, out_specs=out_specs, out_shape=out_shape,
        scratch_shapes=[pltpu.VMEM((RWKV_H, HEAD_DIM, HEAD_DIM), F32), pltpu.VMEM((1, RWKV_COLS), F32)]
                       + [tile_buf] * 9,
        compiler_params=_cparams("parallel", "arbitrary"),
        name="rwkv7",
    )(*args)
    return (outs[0], v_first) if has_vres else (outs[0], outs[1])


def _att_kernel(q_ref, kc_ref, kp_ref, vc_ref, vp_ref, o_ref, l_ref, *, tq):
    first_tile = pl.program_id(2) == 0
    qi = lax.broadcasted_iota(jnp.int32, (ATT_BLOCK, 2 * ATT_BLOCK), 0)
    kc = lax.broadcasted_iota(jnp.int32, (ATT_BLOCK, 2 * ATT_BLOCK), 1)
    band = (kc >= qi) & (kc <= qi + ATT_BLOCK)
    lane = lax.broadcasted_iota(jnp.int32, (ATT_BLOCK, LANES), 1)
    head0 = lane < HEAD_DIM
    for jq in range(tq // ATT_BLOCK):
        cur = slice(jq * ATT_BLOCK, (jq + 1) * ATT_BLOCK)
        prev = slice((jq - 1) * ATT_BLOCK, jq * ATT_BLOCK)
        valid = band & ((kc >= ATT_BLOCK) | jnp.logical_not(first_tile)) if jq == 0 else band
        for hp in range(ATT_W // LANES):
            ls = slice(hp * LANES, (hp + 1) * LANES)
            q = q_ref[0, cur, ls]
            k_prev = kp_ref[0, :, ls] if jq == 0 else kc_ref[0, prev, ls]
            v_prev = vp_ref[0, :, ls] if jq == 0 else vc_ref[0, prev, ls]
            k = jnp.concatenate([k_prev, kc_ref[0, cur, ls]], axis=0)
            v = jnp.concatenate([v_prev, vc_ref[0, cur, ls]], axis=0)
            outs, lses = [], []
            for hh in range(2):
                qh = jnp.where(head0 if hh == 0 else jnp.logical_not(head0), q, jnp.zeros_like(q))
                s = lax.dot_general(qh, k, NT, preferred_element_type=F32)
                s = jnp.where(valid, s, NEG_INF)
                mx = jnp.max(s, axis=-1, keepdims=True)
                p = jnp.exp(s - mx)
                den = jnp.sum(p, axis=-1, keepdims=True)
                outs.append(jnp.dot(p.astype(BF16), v, preferred_element_type=F32) / den)
                lses.append(mx + jnp.log(den))
            o_ref[0, cur, ls] = jnp.where(head0, outs[0], outs[1])
            l_ref[0, cur, ls] = jnp.where(head0, lses[0], lses[1])


def _att_branch(y_att, batch, seq_len, dil):
    m = y_att.shape[0]
    sub_len = seq_len // dil
    tq = min(sub_len, 512)
    bpt = tq // ATT_BLOCK
    y3 = y_att.reshape(batch, sub_len, dil * ATT_COLS)
    cur = lambda c: pl.BlockSpec((1, tq, ATT_W), lambda b, r, i: (b, i, r * 3 + c))
    prev = lambda c: pl.BlockSpec((1, ATT_BLOCK, ATT_W), lambda b, r, i: (b, jnp.maximum(i * bpt - 1, 0), r * 3 + c))
    out = pl.BlockSpec((1, tq, ATT_W), lambda b, r, i: (b, i, r))
    shp = jax.ShapeDtypeStruct((batch, sub_len, dil * ATT_W), F32)
    o, lse = pl.pallas_call(
        functools.partial(_att_kernel, tq=tq),
        grid=(batch, dil, sub_len // tq),
        in_specs=[cur(0), cur(1), prev(1), cur(2), prev(2)],
        out_specs=[out, out], out_shape=[shp, shp],
        compiler_params=_cparams("parallel", "parallel", "parallel"),
        name=f"att_d{dil}",
    )(y3, y3, y3, y3, y3)
    return o.reshape(m, ATT_W), lse.reshape(m, ATT_W)


def _gla_kernel(y_ref, gu_ref, gb_ref, ng_ref, o_ref, state_ref, qi_ref, ki_ref, ke_ref, gam_ref, os_ref, *, tile):
    @pl.when(pl.program_id(1) == 0)
    def _():
        state_ref[...] = jnp.zeros_like(state_ref)

    q = y_ref[:, 0:GLA_W] * HEAD_DIM ** -0.5
    k = y_ref[:, GLA_W:2 * GLA_W]
    og = y_ref[:, 3 * GLA_W:4 * GLA_W]
    lg = _log_sigmoid(_dot3(y_ref[:, 4 * GLA_W:], gu_ref[...]) + gb_ref[...]) * (1.0 / GLA_GATE_TAU)
    prefix_sel, total_sel = _chunk_selectors(tile)
    b = _dot_exact_lhs(prefix_sel, lg)
    b_last = _dot_exact_lhs(total_sel, lg)
    qi_ref[...] = (q * jnp.exp(b)).astype(BF16)
    ki_ref[...] = (k * jnp.exp(-b)).astype(BF16)
    ke_ref[...] = (k * jnp.exp(b_last - b)).astype(BF16)
    gam_ref[...] = jnp.exp(b_last)

    ci = lax.broadcasted_iota(jnp.int32, (CHUNK, CHUNK), 0)
    cj = lax.broadcasted_iota(jnp.int32, (CHUNK, CHUNK), 1)
    causal = cj <= ci

    def chunk_body(c, carry):
        rows = pl.ds(pl.multiple_of(c * CHUNK, CHUNK), CHUNK)
        for h in range(GLA_H):
            sl = slice(h * HEAD_DIM, (h + 1) * HEAD_DIM)
            q_in, k_in, k_end = qi_ref[rows, sl], ki_ref[rows, sl], ke_ref[rows, sl]
            v = y_ref[rows, 2 * GLA_W + h * HEAD_DIM:2 * GLA_W + (h + 1) * HEAD_DIM].astype(BF16)
            gam = gam_ref[rows, sl][0:1, :]
            st = state_ref[h]
            att = jnp.where(causal, lax.dot_general(q_in, k_in, NT, preferred_element_type=F32), 0.0)
            os_ref[rows, sl] = (jnp.dot(att.astype(BF16), v, preferred_element_type=F32)
                                + lax.dot_general(q_in, st.astype(BF16), NT, preferred_element_type=F32))
            state_ref[h] = st * gam + lax.dot_general(v, k_end, TN, preferred_element_type=F32)
        return carry

    lax.fori_loop(0, tile // CHUNK, chunk_body, 0)

    o = os_ref[...]
    ms = _dot_exact_rhs(o * o, _head_ones(GLA_W)) * (1.0 / HEAD_DIM)
    o_ref[...] = o * lax.rsqrt(ms + RMS_EPS) * ng_ref[...] * (og * _sigmoid(og))


def _gla_mixer(y, seq_len, gate_up, gate_b, norm_g):
    m = y.shape[0]
    tile = 512
    nt = seq_len // tile
    gu = jnp.pad(gate_up, ((0, LANES - gate_up.shape[0]), (0, 0)))
    gb = gate_b.reshape(1, GLA_W)
    ng = jnp.tile(norm_g, GLA_H).reshape(1, GLA_W)
    full = lambda a: pl.BlockSpec(a.shape, lambda b, j: (0, 0))
    return pl.pallas_call(
        functools.partial(_gla_kernel, tile=tile),
        grid=(m // seq_len, nt),
        in_specs=[pl.BlockSpec((tile, GLA_COLS_PAD), lambda b, j: (b * nt + j, 0)), full(gu), full(gb), full(ng)],
        out_specs=pl.BlockSpec((tile, GLA_W), lambda b, j: (b * nt + j, 0)),
        out_shape=jax.ShapeDtypeStruct((m, GLA_W), F32),
        scratch_shapes=[pltpu.VMEM((GLA_H, HEAD_DIM, HEAD_DIM), F32)] + [pltpu.VMEM((tile, GLA_W), BF16)] * 3
                       + [pltpu.VMEM((tile, GLA_W), F32)] * 2,
        compiler_params=_cparams("parallel", "arbitrary"),
        name="gla",
    )(y, gu, gb, ng)


def _outproj_kernel(or_ref, o1_ref, l1_ref, o2_ref, l2_ref, o3_ref, l3_ref, og_ref, w_ref, h_ref, o_ref):
    l1, l2, l3 = l1_ref[...], l2_ref[...], l3_ref[...]
    lmax = jnp.maximum(jnp.maximum(l1, l2), l3)
    e1, e2, e3 = jnp.exp(l1 - lmax), jnp.exp(l2 - lmax), jnp.exp(l3 - lmax)
    att = (e1 * o1_ref[...] + e2 * o2_ref[...] + e3 * o3_ref[...]) / (e1 + e2 + e3)
    acc = jnp.dot(or_ref[...].astype(BF16), w_ref[0:RWKV_W, :], preferred_element_type=F32)
    acc += jnp.dot(att.astype(BF16), w_ref[RWKV_W:RWKV_W + ATT_W, :], preferred_element_type=F32)
    acc += jnp.dot(og_ref[...].astype(BF16), w_ref[RWKV_W + ATT_W:, :], preferred_element_type=F32)
    o_ref[...] = h_ref[...] + acc


def _outproj(o_rwkv, att_parts, o_gla, w, h):
    m = h.shape[0]
    tm = 512
    row = lambda wd: pl.BlockSpec((tm, wd), lambda i: (i, 0))
    return pl.pallas_call(
        _outproj_kernel,
        grid=(m // tm,),
        in_specs=[row(RWKV_W)] + [row(ATT_W)] * 6 + [row(GLA_W), pl.BlockSpec(w.shape, lambda i: (0, 0)),
                                                    row(D_MODEL)],
        out_specs=row(D_MODEL),
        out_shape=jax.ShapeDtypeStruct(h.shape, F32),
        compiler_params=_cparams("parallel"),
        name="outproj",
    )(o_rwkv, *att_parts, o_gla, w, h)


def _ffn_kernel(x_ref, g_ref, wg_ref, wu_ref, wd_ref, o_ref, xn_ref, acc_ref):
    f = pl.program_id(1)

    @pl.when(f == 0)
    def _():
        xn_ref[...] = _rms(x_ref[...], g_ref[...]).astype(xn_ref.dtype)
        acc_ref[...] = jnp.zeros_like(acc_ref)

    xn = xn_ref[...]
    gate = jnp.dot(xn, wg_ref[...], preferred_element_type=F32)
    up = jnp.dot(xn, wu_ref[...], preferred_element_type=F32)
    act = (gate * _sigmoid(gate) * up).astype(BF16)
    acc_ref[...] += jnp.dot(act, wd_ref[...], preferred_element_type=F32)

    @pl.when(f == pl.num_programs(1) - 1)
    def _():
        o_ref[...] = x_ref[...] + acc_ref[...]


def _ffn(h, g, wg, wu, wd):
    m = h.shape[0]
    tm, tf = 512, FFN_DENSE // 2
    row = pl.BlockSpec((tm, D_MODEL), lambda i, f: (i, 0))
    return pl.pallas_call(
        _ffn_kernel,
        grid=(m // tm, FFN_DENSE // tf),
        in_specs=[row, pl.BlockSpec(g.shape, lambda i, f: (0, 0)),
                  pl.BlockSpec((D_MODEL, tf), lambda i, f: (0, f)),
                  pl.BlockSpec((D_MODEL, tf), lambda i, f: (0, f)),
                  pl.BlockSpec((tf, D_MODEL), lambda i, f: (f, 0))],
        out_specs=row,
        out_shape=jax.ShapeDtypeStruct(h.shape, F32),
        scratch_shapes=[pltpu.VMEM((tm, D_MODEL), BF16), pltpu.VMEM((tm, D_MODEL), F32)],
        compiler_params=_cparams("parallel", "arbitrary"),
        name="ffn_dense",
    )(h, g, wg, wu, wd)


def _moe_kernel(x_ref, g_ref, wr_ref, wg_ref, wu_ref, wd_ref, fg_ref, o_ref, xn_ref, comb_ref, acc_ref,
                *, final_norm):
    e = pl.program_id(1)
    f = pl.program_id(2)

    @pl.when((e == 0) & (f == 0))
    def _():
        xn = _rms(x_ref[...], g_ref[...])
        xn_ref[...] = xn.astype(xn_ref.dtype)
        acc_ref[...] = jnp.zeros_like(acc_ref)
        logits = jnp.dot(xn, wr_ref[...], preferred_element_type=F32, precision=lax.Precision.HIGHEST)
        lane = lax.broadcasted_iota(jnp.int32, logits.shape, 1)
        lg = jnp.where(lane < N_EXPERTS, logits, NEG_INF)
        m1 = jnp.max(lg, axis=-1, keepdims=True)
        i1 = jnp.min(jnp.where(lg == m1, lane, LANES), axis=-1, keepdims=True)
        lg2 = jnp.where(lane == i1, NEG_INF, lg)
        m2 = jnp.max(lg2, axis=-1, keepdims=True)
        i2 = jnp.min(jnp.where(lg2 == m2, lane, LANES), axis=-1, keepdims=True)
        e2 = jnp.exp(m2 - m1)
        comb_ref[...] = jnp.where(lane == i1, 1.0 / (1.0 + e2), jnp.where(lane == i2, e2 / (1.0 + e2), 0.0))

    comb = comb_ref[...]
    lane = lax.broadcasted_iota(jnp.int32, comb.shape, 1)
    c = jnp.sum(jnp.where(lane == e, comb, 0.0), axis=-1, keepdims=True)
    xn = xn_ref[...]
    gate = jnp.dot(xn, wg_ref[0], preferred_element_type=F32)
    up = jnp.dot(xn, wu_ref[0], preferred_element_type=F32)
    act = (gate * _sigmoid(gate) * up * c).astype(BF16)
    acc_ref[...] += jnp.dot(act, wd_ref[0], preferred_element_type=F32)

    @pl.when((e == pl.num_programs(1) - 1) & (f == pl.num_programs(2) - 1))
    def _():
        y = x_ref[...] + acc_ref[...]
        o_ref[...] = _rms(y, fg_ref[...]) if final_norm else y


def _moe(h, g, w_router, wg, wu, wd, final_g, final_norm):
    m = h.shape[0]
    tm, tf = 1024, 512
    row = pl.BlockSpec((tm, D_MODEL), lambda i, e, f: (i, 0))
    vec = pl.BlockSpec((1, D_MODEL), lambda i, e, f: (0, 0))
    return pl.pallas_call(
        functools.partial(_moe_kernel, final_norm=final_norm),
        grid=(m // tm, N_EXPERTS, FFN_EXPERT // tf),
        in_specs=[row, vec, pl.BlockSpec(w_router.shape, lambda i, e, f: (0, 0)),
                  pl.BlockSpec((1, D_MODEL, tf), lambda i, e, f: (e, 0, f)),
                  pl.BlockSpec((1, D_MODEL, tf), lambda i, e, f: (e, 0, f)),
                  pl.BlockSpec((1, tf, D_MODEL), lambda i, e, f: (e, f, 0)),
                  vec],
        out_specs=row,
        out_shape=jax.ShapeDtypeStruct(h.shape, F32),
        scratch_shapes=[pltpu.VMEM((tm, D_MODEL), BF16), pltpu.VMEM((tm, LANES), F32),
                        pltpu.VMEM((tm, D_MODEL), F32)],
        compiler_params=_cparams("parallel", "arbitrary", "arbitrary"),
        name="moe",
    )(h, g, w_router, wg, wu, wd, final_g)


def kernel(x, positions, mix_norm_g, w_in, rwkv_mu, rwkv_w0, rwkv_w2, rwkv_a0, rwkv_a2, rwkv_g2, rwkv_k_k, rwkv_k_a, rwkv_r_k, rwkv_ln_g, rwkv_ln_b, rwkv_v0, rwkv_v1, rwkv_v2, gla_gate_up, gla_gate_b, gla_norm_g, w_out, ffn_norm_g, ffn_w_gate, ffn_w_up, ffn_w_down, moe_router, moe_w_gate, moe_w_up, moe_w_down, final_norm_g):
    B, S, D = x.shape
    M = B * S
    h = x.reshape(M, D)
    cos, sa, sb = _rope_tables(positions)
    v_first = None
    for layer in range(DEPTH):
        w = w_in[layer]
        wr = w[:, :RWKV_COLS].astype(BF16)
        wa = w[:, RWKV_COLS:RWKV_COLS + ATT_COLS].astype(BF16)
        wgl = w[:, RWKV_COLS + ATT_COLS:]
        wgl = jnp.concatenate([wgl[:, :3 * GLA_W], wgl[:, 3 * GLA_W + GLA_GATE_RANK:],
                               wgl[:, 3 * GLA_W:3 * GLA_W + GLA_GATE_RANK],
                               jnp.zeros((D, LANES - GLA_GATE_RANK), w.dtype)], axis=1).astype(BF16)
        y_rwkv, y_att, y_gla = _inproj(h, mix_norm_g[layer].reshape(1, D), wr, wa, wgl, cos, sa, sb)
        vres = None if layer == 0 else (rwkv_v0[layer - 1], rwkv_v1[layer - 1], rwkv_v2[layer - 1])
        o_rwkv, v_first = _rwkv_mixer(y_rwkv, S, rwkv_mu[layer], rwkv_w0[layer], rwkv_w2[layer], rwkv_a0[layer],
                                      rwkv_a2[layer], rwkv_g2[layer], rwkv_k_k[layer], rwkv_k_a[layer],
                                      rwkv_r_k[layer], rwkv_ln_g[layer], rwkv_ln_b[layer], v_first, vres)
        att_parts = [t for dil in DILATIONS for t in _att_branch(y_att, B, S, dil)]
        o_gla = _gla_mixer(y_gla, S, gla_gate_up[layer], gla_gate_b[layer], gla_norm_g[layer])
        h = _outproj(o_rwkv, att_parts, o_gla, w_out[layer].astype(BF16), h)
        i = layer // 2
        g = ffn_norm_g[layer].reshape(1, D)
        if layer % 2 == 0:
            h = _ffn(h, g, ffn_w_gate[i].astype(BF16), ffn_w_up[i].astype(BF16), ffn_w_down[i].astype(BF16))
        else:
            wrt = jnp.pad(moe_router[i], ((0, 0), (0, LANES - N_EXPERTS)))
            h = _moe(h, g, wrt, moe_w_gate[i].astype(BF16), moe_w_up[i].astype(BF16), moe_w_down[i].astype(BF16),
                     final_norm_g.reshape(1, D), layer == DEPTH - 1)
    return h.reshape(B, S, D)
```

```python
import functools

import jax
import jax.numpy as jnp
from jax import lax
from jax.experimental import pallas as pl
from jax.experimental.pallas import tpu as pltpu

D_MODEL = 1024
DEPTH = 2
HEAD_DIM = 64
RWKV_W = 256
RWKV_H = 4
ATT_W = 384
GLA_W = 384
GLA_H = 6
RWKV_DECAY_RANK = 32
RWKV_AAA_RANK = 32
RWKV_GATE_RANK = 64
RWKV_GN_EPS = 64e-5
GLA_GATE_RANK = 16
GLA_GATE_TAU = 16.0
CHUNK = 64
DILATIONS = (1, 4, 16)
ATT_BLOCK = 128
ROPE_THETA = 500000.0
ROPE_DIMS = 16
ROPE_HALF = 8
FFN_DENSE = 2816
N_EXPERTS = 8
FFN_EXPERT = 3584
RMS_EPS = 1e-5
NEG_INF = -1e30

RWKV_COLS = 3 * RWKV_W + RWKV_DECAY_RANK + RWKV_AAA_RANK + RWKV_GATE_RANK
ATT_COLS = 3 * ATT_W
LANES = 128
GLA_COLS_PAD = 4 * GLA_W + LANES

VMEM_LIMIT = 56 * 1024 * 1024

F32 = jnp.float32
BF16 = jnp.bfloat16
NN = (((1,), (0,)), ((), ()))
NT = (((1,), (1,)), ((), ()))
TN = (((0,), (0,)), ((), ()))
BNN = (((2,), (1,)), ((0,), (0,)))
BNT = (((2,), (2,)), ((0,), (0,)))
BTN = (((1,), (1,)), ((0,), (0,)))


def _cparams(*sem):
    return pltpu.CompilerParams(dimension_semantics=sem, vmem_limit_bytes=VMEM_LIMIT)


def _rms(x, g):
    return x * lax.rsqrt(jnp.mean(x * x, axis=-1, keepdims=True) + RMS_EPS) * g


def _sigmoid(x):
    return 1.0 / (1.0 + jnp.exp(-x))


def _log_sigmoid(x):
    return jnp.minimum(x, 0.0) - jnp.log(1.0 + jnp.exp(-jnp.abs(x)))


def _split_hi_lo(a):
    hi = a.astype(BF16)
    return hi, (a - hi.astype(F32)).astype(BF16)


def _dot3(a, b, dims=NN):
    a_hi, a_lo = _split_hi_lo(a)
    b_hi, b_lo = _split_hi_lo(b)
    d = lambda x, y: lax.dot_general(x, y, dims, preferred_element_type=F32)
    return d(a_hi, b_hi) + d(a_lo, b_hi) + d(a_hi, b_lo)


def _dot_exact_lhs(l_bf16, x):
    x_hi, x_lo = _split_hi_lo(x)
    return jnp.dot(l_bf16, x_hi, preferred_element_type=F32) + jnp.dot(l_bf16, x_lo, preferred_element_type=F32)


def _dot_exact_rhs(x, r_bf16):
    x_hi, x_lo = _split_hi_lo(x)
    return jnp.dot(x_hi, r_bf16, preferred_element_type=F32) + jnp.dot(x_lo, r_bf16, preferred_element_type=F32)


def _bdot(a, b, dims, passes):
    d = lambda x, y: lax.dot_general(x, y, dims, preferred_element_type=F32)
    if passes == 1:
        return d(a.astype(BF16), b.astype(BF16))
    a_hi, a_lo = _split_hi_lo(a)
    b_hi, b_lo = _split_hi_lo(b)
    return d(a_hi, b_hi) + d(a_lo, b_hi) + d(a_hi, b_lo)


def _head_ones(width):
    idx = jnp.arange(width) // HEAD_DIM
    return (idx[:, None] == idx[None, :]).astype(BF16)


def _chunk_selectors(tile):
    t = jnp.arange(tile)
    same_chunk = t[:, None] // CHUNK == t[None, :] // CHUNK
    return (same_chunk & (t[None, :] <= t[:, None])).astype(BF16), same_chunk.astype(BF16)


def _rope_table_kernel(pos_ref, invf_ref, cos_ref, sa_ref, sb_ref):
    ang = pos_ref[...].astype(F32) * invf_ref[...]
    lane = lax.broadcasted_iota(jnp.int32, ang.shape, 1) % HEAD_DIM
    c = jnp.cos(ang)
    s = jnp.sin(ang)
    cos_ref[...] = jnp.where(lane < ROPE_DIMS, c, 1.0)
    sa_ref[...] = jnp.where((lane >= ROPE_HALF) & (lane < ROPE_DIMS), s, 0.0)
    sb_ref[...] = jnp.where(lane < ROPE_HALF, -s, 0.0)


def _rope_tables(positions):
    m = positions.size
    tm = 2048
    lane = jnp.arange(LANES) % ROPE_HALF
    invf = (ROPE_THETA ** (-lane.astype(F32) / ROPE_HALF)).reshape(1, LANES)
    out = jax.ShapeDtypeStruct((m, LANES), F32)
    row = pl.BlockSpec((tm, LANES), lambda i: (i, 0))
    return pl.pallas_call(
        _rope_table_kernel,
        grid=(m // tm,),
        in_specs=[pl.BlockSpec((tm, 1), lambda i: (i, 0)), pl.BlockSpec((1, LANES), lambda i: (0, 0))],
        out_specs=[row, row, row],
        out_shape=[out, out, out],
        compiler_params=_cparams("parallel"),
        name="rope_tables",
    )(positions.reshape(m, 1), invf)


def _inproj_kernel(x_ref, g_ref, wr_ref, wa_ref, wg_ref, cos_ref, sa_ref, sb_ref, yr_ref, ya_ref, yg_ref):
    xb = _rms(x_ref[...], g_ref[...]).astype(BF16)
    yr_ref[...] = jnp.dot(xb, wr_ref[...], preferred_element_type=F32)
    yg_ref[...] = jnp.dot(xb, wg_ref[...], preferred_element_type=F32)
    ya = jnp.dot(xb, wa_ref[...], preferred_element_type=F32)
    cos, sa, sb = cos_ref[...], sa_ref[...], sb_ref[...]
    for j in range(2 * ATT_W // LANES):
        blk = ya[:, j * LANES:(j + 1) * LANES]
        rot = blk * cos + pltpu.roll(blk, ROPE_HALF, 1) * sa + pltpu.roll(blk, LANES - ROPE_HALF, 1) * sb
        if j < ATT_W // LANES:
            rot = rot * HEAD_DIM ** -0.5
        ya_ref[:, j * LANES:(j + 1) * LANES] = rot.astype(ya_ref.dtype)
    ya_ref[:, 2 * ATT_W:] = ya[:, 2 * ATT_W:].astype(ya_ref.dtype)


def _inproj(h, g, wr, wa, wg, cos, sa, sb):
    m = h.shape[0]
    tm = 512
    row = lambda w: pl.BlockSpec((tm, w), lambda i: (i, 0))
    full = lambda a: pl.BlockSpec(a.shape, lambda i: (0, 0))
    return pl.pallas_call(
        _inproj_kernel,
        grid=(m // tm,),
        in_specs=[row(D_MODEL), full(g), full(wr), full(wa), full(wg), row(LANES), row(LANES), row(LANES)],
        out_specs=[row(RWKV_COLS), row(ATT_COLS), row(GLA_COLS_PAD)],
        out_shape=[jax.ShapeDtypeStruct((m, RWKV_COLS), F32),
                   jax.ShapeDtypeStruct((m, ATT_COLS), BF16),
                   jax.ShapeDtypeStruct((m, GLA_COLS_PAD), F32)],
        compiler_params=_cparams("parallel"),
        name="inproj",
    )(h, g, wr, wa, wg, cos, sa, sb)


RWKV_GROUP = 4


def _rwkv_kernel(*refs, has_vres, tile):
    nc = tile // CHUNK
    if has_vres:
        (y_ref, vf_ref, mu_ref, w0_ref, w2_ref, a0_ref, a2_ref, g2_ref, kk_ref, ka_ref, rk_ref, lng_ref, lnb_ref,
         psel_ref, tsel_ref, hones_ref, v0_ref, v1_ref, v2_ref, o_ref, *scratch) = refs
    else:
        (y_ref, mu_ref, w0_ref, w2_ref, a0_ref, a2_ref, g2_ref, kk_ref, ka_ref, rk_ref, lng_ref, lnb_ref,
         psel_ref, tsel_ref, hones_ref, o_ref, vfo_ref, *scratch) = refs
    (state_ref, prev_ref, at_ref, rt_ref, bt_ref, kt_ref, bh_ref, kh_ref, vv_ref, gam_ref,
     qt_ref, ol_ref, gm_ref, hm_ref, os_ref) = scratch

    @pl.when(pl.program_id(1) == 0)
    def _():
        state_ref[...] = jnp.zeros_like(state_ref)
        prev_ref[...] = jnp.zeros_like(prev_ref)

    y = y_ref[...]
    row = lax.broadcasted_iota(jnp.int32, (tile, 1), 0)
    ysh = jnp.where(row == 0, prev_ref[...], pltpu.roll(y, 1, 0))
    prev_ref[...] = y[tile - 1:tile, :]
    ym = y + (ysh - y) * mu_ref[...]
    r = ym[:, 0:RWKV_W]
    k = ym[:, RWKV_W:2 * RWKV_W]
    v = ym[:, 2 * RWKV_W:3 * RWKV_W]
    x6 = ym[:, 3 * RWKV_W:]
    wl = w0_ref[...] + _dot3(jnp.tanh(x6), w2_ref[...])
    lw = -jnp.exp(_log_sigmoid(wl) - 0.5)
    a = _sigmoid(a0_ref[...] + _dot3(x6, a2_ref[...]))
    g = _dot3(_sigmoid(x6), g2_ref[...])
    if has_vres:
        v = v + (vf_ref[...] - v) * _sigmoid(v0_ref[...] + _dot3(_dot3(v, v1_ref[...]), v2_ref[...]))
    else:
        vfo_ref[...] = v

    head_ones = hones_ref[...]
    kk = k * kk_ref[...]
    kk = kk * lax.rsqrt(jnp.maximum(_dot_exact_rhs(kk * kk, head_ones), 1e-24))
    k2 = k * (1.0 + (a - 1.0) * ka_ref[...])
    bonus = _dot_exact_rhs(r * k2 * rk_ref[...], head_ones) * v

    c = _dot_exact_lhs(psel_ref[...], lw)
    c_last = _dot_exact_lhs(tsel_ref[...], lw)
    b = kk * a
    e_neg = jnp.exp(-c)
    e_end = jnp.exp(c_last - c)

    def put(ref, val):
        for h in range(RWKV_H):
            ref[h] = val[:, h * HEAD_DIM:(h + 1) * HEAD_DIM].reshape(nc, CHUNK, HEAD_DIM)

    put(rt_ref, r * jnp.exp(c))
    put(at_ref, -kk * jnp.exp(c - lw))
    put(bt_ref, b * e_neg)
    put(kt_ref, k2 * e_neg)
    put(bh_ref, b * e_end)
    put(kh_ref, k2 * e_end)
    put(vv_ref, v)
    put(gam_ref, jnp.exp(c_last))

    n = RWKV_H * RWKV_GROUP
    qi = lax.broadcasted_iota(jnp.int32, (n, CHUNK, CHUNK), 1)
    qj = lax.broadcasted_iota(jnp.int32, (n, CHUNK, CHUNK), 2)
    strict = qj < qi
    incl = qj <= qi
    eye = qj == qi
    blk16 = qi // 16 == qj // 16
    blk32 = qi // 32 == qj // 32

    def group_body(gi, carry):
        cs = pl.ds(gi * RWKV_GROUP, RWKV_GROUP)
        ld = lambda ref: ref[:, cs].reshape(n, CHUNK, HEAD_DIM)
        at, rt, bt, kt, bh, kh, vv = (ld(x) for x in (at_ref, rt_ref, bt_ref, kt_ref, bh_ref, kh_ref, vv_ref))
        gam = ld(gam_ref)[:, 0:1, :]
        a_ab = jnp.where(strict, _bdot(at, bt, BNT, 1), 0.0)
        a_ak = jnp.where(strict, _bdot(at, kt, BNT, 1), 0.0)
        p_b = jnp.where(incl, _bdot(rt, bt, BNT, 1), 0.0)
        p_k = jnp.where(incl, _bdot(rt, kt, BNT, 1), 0.0)
        d1 = jnp.where(blk16, a_ab, 0.0)
        d2 = _bdot(d1, d1, BNN, 3)
        d4 = _bdot(d2, d2, BNN, 3)
        d8 = _bdot(d4, d4, BNN, 3)
        t = jnp.where(eye, 1.0, d1)
        t = t + _bdot(t, d2, BNN, 3)
        t = t + _bdot(t, d4, BNN, 3)
        t = t + _bdot(t, d8, BNN, 3)
        t = t + _bdot(_bdot(t, jnp.where(blk32 & ~blk16, a_ab, 0.0), BNN, 3), t, BNN, 3)
        t = t + _bdot(_bdot(t, jnp.where(blk32, 0.0, a_ab), BNN, 3), t, BNN, 3)
        w = _bdot(t, at, BNN, 1)
        u_loc = _bdot(t, _bdot(a_ak, vv, BNN, 1), BNN, 1)

        def st(ref, val):
            ref[:, cs] = val.reshape(RWKV_H, RWKV_GROUP, CHUNK, HEAD_DIM)

        st(qt_ref, rt + _bdot(p_b, w, BNN, 1))
        st(ol_ref, _bdot(p_b, u_loc, BNN, 1) + _bdot(p_k, vv, BNN, 1))
        st(gm_ref, _bdot(w, bh, BTN, 1) + jnp.where(eye, gam, 0.0))
        st(hm_ref, _bdot(u_loc, bh, BTN, 1) + _bdot(vv, kh, BTN, 1))
        return carry

    lax.fori_loop(0, nc // RWKV_GROUP, group_body, 0)

    def chunk_body(ci, s):
        o = _bdot(qt_ref[:, ci], s, BNT, 3) + ol_ref[:, ci]
        rows = pl.ds(pl.multiple_of(ci * CHUNK, CHUNK), CHUNK)
        for h in range(RWKV_H):
            os_ref[rows, h * HEAD_DIM:(h + 1) * HEAD_DIM] = o[h]
        return _bdot(s, gm_ref[:, ci], BNN, 3) + hm_ref[:, ci]

    state_ref[...] = lax.fori_loop(0, nc, chunk_body, state_ref[...])

    o = os_ref[...]
    mean = _dot_exact_rhs(o, head_ones) * (1.0 / HEAD_DIM)
    oc = o - mean
    var = _dot_exact_rhs(oc * oc, head_ones) * (1.0 / HEAD_DIM)
    on = oc * lax.rsqrt(var + RWKV_GN_EPS) * lng_ref[...] + lnb_ref[...]
    o_ref[...] = (on + bonus) * g


def _rwkv_mixer(y, seq_len, mu, w0, w2, a0, a2, g2, k_k, k_a, r_k, ln_g, ln_b, v_first, vres):
    m = y.shape[0]
    tile = 512
    nt = seq_len // tile
    has_vres = vres is not None
    row = lambda wd: pl.BlockSpec((tile, wd), lambda b, j: (b * nt + j, 0))
    full = lambda arr: pl.BlockSpec(arr.shape, lambda b, j: (0, 0))
    vec = lambda t: t.reshape(1, -1)
    pad_rows = lambda w, start: jnp.zeros((LANES, RWKV_W), F32).at[start:start + w.shape[0]].set(w)
    consts = [vec(mu), vec(w0), pad_rows(w2, 0), vec(a0), pad_rows(a2, RWKV_DECAY_RANK),
              pad_rows(g2, RWKV_DECAY_RANK + RWKV_AAA_RANK), vec(k_k), vec(k_a), vec(r_k), vec(ln_g), vec(ln_b),
              *_chunk_selectors(tile), _head_ones(RWKV_W)]
    args = [y] + ([v_first] if has_vres else []) + consts
    in_specs = [row(RWKV_COLS)] + ([row(RWKV_W)] if has_vres else []) + [full(c) for c in consts]
    out_shape = [jax.ShapeDtypeStruct((m, RWKV_W), F32)]
    out_specs = [row(RWKV_W)]
    if has_vres:
        v0, v1, v2 = vres
        extra = [vec(v0), jnp.pad(v1, ((0, 0), (0, LANES - v1.shape[1]))),
                 jnp.pad(v2, ((0, LANES - v2.shape[0]), (0, 0)))]
        args += extra
        in_specs += [full(c) for c in extra]
    else:
        out_shape.append(jax.ShapeDtypeStruct((m, RWKV_W), F32))
        out_specs.append(row(RWKV_W))
    unit_buf = pltpu.VMEM((RWKV_H, tile // CHUNK, CHUNK, HEAD_DIM), F32)
    outs = pl.pallas_call(
        functools.partial(_rwkv_kernel, has_vres=has_vres, tile=tile),
        grid=(m // seq_len, nt),
        in_specs=in_specs, out_specs=out_specs, out_shape=out_shape,
        scratch_shapes=[pltpu.VMEM((RWKV_H, HEAD_DIM, HEAD_DIM), F32), pltpu.VMEM((1, RWKV_COLS), F32)]
                       + [unit_buf] * 12 + [pltpu.VMEM((tile, RWKV_W), F32)],
        compiler_params=_cparams("parallel", "arbitrary"),
        name="rwkv7",
    )(*args)
    return (outs[0], v_first) if has_vres else (outs[0], outs[1])


def _att_kernel(q_ref, kc_ref, kp_ref, vc_ref, vp_ref, o_ref, l_ref, *, tq):
    first_tile = pl.program_id(2) == 0
    qi = lax.broadcasted_iota(jnp.int32, (ATT_BLOCK, 2 * ATT_BLOCK), 0)
    kc = lax.broadcasted_iota(jnp.int32, (ATT_BLOCK, 2 * ATT_BLOCK), 1)
    band = (kc >= qi) & (kc <= qi + ATT_BLOCK)
    lane = lax.broadcasted_iota(jnp.int32, (ATT_BLOCK, LANES), 1)
    head0 = lane < HEAD_DIM
    for jq in range(tq // ATT_BLOCK):
        cur = slice(jq * ATT_BLOCK, (jq + 1) * ATT_BLOCK)
        prev = slice((jq - 1) * ATT_BLOCK, jq * ATT_BLOCK)
        valid = band & ((kc >= ATT_BLOCK) | jnp.logical_not(first_tile)) if jq == 0 else band
        for hp in range(ATT_W // LANES):
            ls = slice(hp * LANES, (hp + 1) * LANES)
            q = q_ref[0, cur, ls]
            k_prev = kp_ref[0, :, ls] if jq == 0 else kc_ref[0, prev, ls]
            v_prev = vp_ref[0, :, ls] if jq == 0 else vc_ref[0, prev, ls]
            k = jnp.concatenate([k_prev, kc_ref[0, cur, ls]], axis=0)
            v = jnp.concatenate([v_prev, vc_ref[0, cur, ls]], axis=0)
            outs, lses = [], []
            for hh in range(2):
                qh = jnp.where(head0 if hh == 0 else jnp.logical_not(head0), q, jnp.zeros_like(q))
                s = lax.dot_general(qh, k, NT, preferred_element_type=F32)
                s = jnp.where(valid, s, NEG_INF)
                mx = jnp.max(s, axis=-1, keepdims=True)
                p = jnp.exp(s - mx)
                den = jnp.sum(p, axis=-1, keepdims=True)
                outs.append(jnp.dot(p.astype(BF16), v, preferred_element_type=F32) / den)
                lses.append(mx + jnp.log(den))
            o_ref[0, cur, ls] = jnp.where(head0, outs[0], outs[1])
            l_ref[0, cur, ls] = jnp.where(head0, lses[0], lses[1])


def _att_branch(y_att, batch, seq_len, dil):
    m = y_att.shape[0]
    sub_len = seq_len // dil
    tq = min(sub_len, 512)
    bpt = tq // ATT_BLOCK
    y3 = y_att.reshape(batch, sub_len, dil * ATT_COLS)
    cur = lambda c: pl.BlockSpec((1, tq, ATT_W), lambda b, r, i: (b, i, r * 3 + c))
    prev = lambda c: pl.BlockSpec((1, ATT_BLOCK, ATT_W), lambda b, r, i: (b, jnp.maximum(i * bpt - 1, 0), r * 3 + c))
    out = pl.BlockSpec((1, tq, ATT_W), lambda b, r, i: (b, i, r))
    shp = jax.ShapeDtypeStruct((batch, sub_len, dil * ATT_W), F32)
    o, lse = pl.pallas_call(
        functools.partial(_att_kernel, tq=tq),
        grid=(batch, dil, sub_len // tq),
        in_specs=[cur(0), cur(1), prev(1), cur(2), prev(2)],
        out_specs=[out, out], out_shape=[shp, shp],
        compiler_params=_cparams("parallel", "parallel", "parallel"),
        name=f"att_d{dil}",
    )(y3, y3, y3, y3, y3)
    return o.reshape(m, ATT_W), lse.reshape(m, ATT_W)


def _gla_kernel(y_ref, gu_ref, gb_ref, ng_ref, psel_ref, tsel_ref, hones_ref, o_ref, state_ref, qi_ref, ki_ref, ke_ref, gam_ref, os_ref, *, tile):
    @pl.when(pl.program_id(1) == 0)
    def _():
        state_ref[...] = jnp.zeros_like(state_ref)

    q = y_ref[:, 0:GLA_W] * HEAD_DIM ** -0.5
    k = y_ref[:, GLA_W:2 * GLA_W]
    og = y_ref[:, 3 * GLA_W:4 * GLA_W]
    lg = _log_sigmoid(_dot3(y_ref[:, 4 * GLA_W:], gu_ref[...]) + gb_ref[...]) * (1.0 / GLA_GATE_TAU)
    b = _dot_exact_lhs(psel_ref[...], lg)
    b_last = _dot_exact_lhs(tsel_ref[...], lg)
    qi_ref[...] = (q * jnp.exp(b)).astype(BF16)
    ki_ref[...] = (k * jnp.exp(-b)).astype(BF16)
    ke_ref[...] = (k * jnp.exp(b_last - b)).astype(BF16)
    gam_ref[...] = jnp.exp(b_last)

    ci = lax.broadcasted_iota(jnp.int32, (CHUNK, CHUNK), 0)
    cj = lax.broadcasted_iota(jnp.int32, (CHUNK, CHUNK), 1)
    causal = cj <= ci

    def chunk_body(c, carry):
        rows = pl.ds(pl.multiple_of(c * CHUNK, CHUNK), CHUNK)
        for h in range(GLA_H):
            sl = slice(h * HEAD_DIM, (h + 1) * HEAD_DIM)
            q_in, k_in, k_end = qi_ref[rows, sl], ki_ref[rows, sl], ke_ref[rows, sl]
            v = y_ref[rows, 2 * GLA_W + h * HEAD_DIM:2 * GLA_W + (h + 1) * HEAD_DIM].astype(BF16)
            gam = gam_ref[rows, sl][0:1, :]
            st = state_ref[h]
            att = jnp.where(causal, lax.dot_general(q_in, k_in, NT, preferred_element_type=F32), 0.0)
            os_ref[rows, sl] = (jnp.dot(att.astype(BF16), v, preferred_element_type=F32)
                                + lax.dot_general(q_in, st.astype(BF16), NT, preferred_element_type=F32))
            state_ref[h] = st * gam + lax.dot_general(v, k_end, TN, preferred_element_type=F32)
        return carry

    lax.fori_loop(0, tile // CHUNK, chunk_body, 0)

    o = os_ref[...]
    ms = _dot_exact_rhs(o * o, hones_ref[...]) * (1.0 / HEAD_DIM)
    o_ref[...] = o * lax.rsqrt(ms + RMS_EPS) * ng_ref[...] * (og * _sigmoid(og))


def _gla_mixer(y, seq_len, gate_up, gate_b, norm_g):
    m = y.shape[0]
    tile = 512
    nt = seq_len // tile
    gu = jnp.pad(gate_up, ((0, LANES - gate_up.shape[0]), (0, 0)))
    gb = gate_b.reshape(1, GLA_W)
    ng = jnp.tile(norm_g, GLA_H).reshape(1, GLA_W)
    consts = [gu, gb, ng, *_chunk_selectors(tile), _head_ones(GLA_W)]
    full = lambda a: pl.BlockSpec(a.shape, lambda b, j: (0, 0))
    return pl.pallas_call(
        functools.partial(_gla_kernel, tile=tile),
        grid=(m // seq_len, nt),
        in_specs=[pl.BlockSpec((tile, GLA_COLS_PAD), lambda b, j: (b * nt + j, 0))] + [full(c) for c in consts],
        out_specs=pl.BlockSpec((tile, GLA_W), lambda b, j: (b * nt + j, 0)),
        out_shape=jax.ShapeDtypeStruct((m, GLA_W), F32),
        scratch_shapes=[pltpu.VMEM((GLA_H, HEAD_DIM, HEAD_DIM), F32)] + [pltpu.VMEM((tile, GLA_W), BF16)] * 3
                       + [pltpu.VMEM((tile, GLA_W), F32)] * 2,
        compiler_params=_cparams("parallel", "arbitrary"),
        name="gla",
    )(y, *consts)


def _outproj_kernel(or_ref, o1_ref, l1_ref, o2_ref, l2_ref, o3_ref, l3_ref, og_ref, w_ref, h_ref, o_ref):
    l1, l2, l3 = l1_ref[...], l2_ref[...], l3_ref[...]
    lmax = jnp.maximum(jnp.maximum(l1, l2), l3)
    e1, e2, e3 = jnp.exp(l1 - lmax), jnp.exp(l2 - lmax), jnp.exp(l3 - lmax)
    att = (e1 * o1_ref[...] + e2 * o2_ref[...] + e3 * o3_ref[...]) / (e1 + e2 + e3)
    acc = jnp.dot(or_ref[...].astype(BF16), w_ref[0:RWKV_W, :], preferred_element_type=F32)
    acc += jnp.dot(att.astype(BF16), w_ref[RWKV_W:RWKV_W + ATT_W, :], preferred_element_type=F32)
    acc += jnp.dot(og_ref[...].astype(BF16), w_ref[RWKV_W + ATT_W:, :], preferred_element_type=F32)
    o_ref[...] = h_ref[...] + acc


def _outproj(o_rwkv, att_parts, o_gla, w, h):
    m = h.shape[0]
    tm = 512
    row = lambda wd: pl.BlockSpec((tm, wd), lambda i: (i, 0))
    return pl.pallas_call(
        _outproj_kernel,
        grid=(m // tm,),
        in_specs=[row(RWKV_W)] + [row(ATT_W)] * 6 + [row(GLA_W), pl.BlockSpec(w.shape, lambda i: (0, 0)),
                                                    row(D_MODEL)],
        out_specs=row(D_MODEL),
        out_shape=jax.ShapeDtypeStruct(h.shape, F32),
        compiler_params=_cparams("parallel"),
        name="outproj",
    )(o_rwkv, *att_parts, o_gla, w, h)


def _ffn_kernel(x_ref, g_ref, wg_ref, wu_ref, wd_ref, o_ref, xn_ref, acc_ref):
    f = pl.program_id(1)

    @pl.when(f == 0)
    def _():
        xn_ref[...] = _rms(x_ref[...], g_ref[...]).astype(xn_ref.dtype)
        acc_ref[...] = jnp.zeros_like(acc_ref)

    xn = xn_ref[...]
    gate = jnp.dot(xn, wg_ref[...], preferred_element_type=F32)
    up = jnp.dot(xn, wu_ref[...], preferred_element_type=F32)
    act = (gate * _sigmoid(gate) * up).astype(BF16)
    acc_ref[...] += jnp.dot(act, wd_ref[...], preferred_element_type=F32)

    @pl.when(f == pl.num_programs(1) - 1)
    def _():
        o_ref[...] = x_ref[...] + acc_ref[...]


def _ffn(h, g, wg, wu, wd):
    m = h.shape[0]
    tm, tf = 512, FFN_DENSE // 2
    row = pl.BlockSpec((tm, D_MODEL), lambda i, f: (i, 0))
    return pl.pallas_call(
        _ffn_kernel,
        grid=(m // tm, FFN_DENSE // tf),
        in_specs=[row, pl.BlockSpec(g.shape, lambda i, f: (0, 0)),
                  pl.BlockSpec((D_MODEL, tf), lambda i, f: (0, f)),
                  pl.BlockSpec((D_MODEL, tf), lambda i, f: (0, f)),
                  pl.BlockSpec((tf, D_MODEL), lambda i, f: (f, 0))],
        out_specs=row,
        out_shape=jax.ShapeDtypeStruct(h.shape, F32),
        scratch_shapes=[pltpu.VMEM((tm, D_MODEL), BF16), pltpu.VMEM((tm, D_MODEL), F32)],
        compiler_params=_cparams("parallel", "arbitrary"),
        name="ffn_dense",
    )(h, g, wg, wu, wd)


def _moe_kernel(x_ref, g_ref, wr_ref, wg_ref, wu_ref, wd_ref, fg_ref, o_ref, xn_ref, comb_ref, acc_ref,
                *, final_norm):
    e = pl.program_id(1)
    f = pl.program_id(2)

    @pl.when((e == 0) & (f == 0))
    def _():
        xn = _rms(x_ref[...], g_ref[...])
        xn_ref[...] = xn.astype(xn_ref.dtype)
        acc_ref[...] = jnp.zeros_like(acc_ref)
        logits = jnp.dot(xn, wr_ref[...], preferred_element_type=F32, precision=lax.Precision.HIGHEST)
        lane = lax.broadcasted_iota(jnp.int32, logits.shape, 1)
        lg = jnp.where(lane < N_EXPERTS, logits, NEG_INF)
        m1 = jnp.max(lg, axis=-1, keepdims=True)
        i1 = jnp.min(jnp.where(lg == m1, lane, LANES), axis=-1, keepdims=True)
        lg2 = jnp.where(lane == i1, NEG_INF, lg)
        m2 = jnp.max(lg2, axis=-1, keepdims=True)
        i2 = jnp.min(jnp.where(lg2 == m2, lane, LANES), axis=-1, keepdims=True)
        e2 = jnp.exp(m2 - m1)
        comb_ref[...] = jnp.where(lane == i1, 1.0 / (1.0 + e2), jnp.where(lane == i2, e2 / (1.0 + e2), 0.0))

    comb = comb_ref[...]
    lane = lax.broadcasted_iota(jnp.int32, comb.shape, 1)
    c = jnp.sum(jnp.where(lane == e, comb, 0.0), axis=-1, keepdims=True)
    xn = xn_ref[...]
    gate = jnp.dot(xn, wg_ref[0], preferred_element_type=F32)
    up = jnp.dot(xn, wu_ref[0], preferred_element_type=F32)
    act = (gate * _sigmoid(gate) * up * c).astype(BF16)
    acc_ref[...] += jnp.dot(act, wd_ref[0], preferred_element_type=F32)

    @pl.when((e == pl.num_programs(1) - 1) & (f == pl.num_programs(2) - 1))
    def _():
        y = x_ref[...] + acc_ref[...]
        o_ref[...] = _rms(y, fg_ref[...]) if final_norm else y


def _moe(h, g, w_router, wg, wu, wd, final_g, final_norm):
    m = h.shape[0]
    tm, tf = 1024, 512
    row = pl.BlockSpec((tm, D_MODEL), lambda i, e, f: (i, 0))
    vec = pl.BlockSpec((1, D_MODEL), lambda i, e, f: (0, 0))
    return pl.pallas_call(
        functools.partial(_moe_kernel, final_norm=final_norm),
        grid=(m // tm, N_EXPERTS, FFN_EXPERT // tf),
        in_specs=[row, vec, pl.BlockSpec(w_router.shape, lambda i, e, f: (0, 0)),
                  pl.BlockSpec((1, D_MODEL, tf), lambda i, e, f: (e, 0, f)),
                  pl.BlockSpec((1, D_MODEL, tf), lambda i, e, f: (e, 0, f)),
                  pl.BlockSpec((1, tf, D_MODEL), lambda i, e, f: (e, f, 0)),
                  vec],
        out_specs=row,
        out_shape=jax.ShapeDtypeStruct(h.shape, F32),
        scratch_shapes=[pltpu.VMEM((tm, D_MODEL), BF16), pltpu.VMEM((tm, LANES), F32),
                        pltpu.VMEM((tm, D_MODEL), F32)],
        compiler_params=_cparams("parallel", "arbitrary", "arbitrary"),
        name="moe",
    )(h, g, w_router, wg, wu, wd, final_g)


def kernel(x, positions, mix_norm_g, w_in, rwkv_mu, rwkv_w0, rwkv_w2, rwkv_a0, rwkv_a2, rwkv_g2, rwkv_k_k, rwkv_k_a, rwkv_r_k, rwkv_ln_g, rwkv_ln_b, rwkv_v0, rwkv_v1, rwkv_v2, gla_gate_up, gla_gate_b, gla_norm_g, w_out, ffn_norm_g, ffn_w_gate, ffn_w_up, ffn_w_down, moe_router, moe_w_gate, moe_w_up, moe_w_down, final_norm_g):
    B, S, D = x.shape
    M = B * S
    h = x.reshape(M, D)
    cos, sa, sb = _rope_tables(positions)
    v_first = None
    for layer in range(DEPTH):
        w = w_in[layer]
        wr = w[:, :RWKV_COLS].astype(BF16)
        wa = w[:, RWKV_COLS:RWKV_COLS + ATT_COLS].astype(BF16)
        wgl = w[:, RWKV_COLS + ATT_COLS:]
        wgl = jnp.concatenate([wgl[:, :3 * GLA_W], wgl[:, 3 * GLA_W + GLA_GATE_RANK:],
                               wgl[:, 3 * GLA_W:3 * GLA_W + GLA_GATE_RANK],
                               jnp.zeros((D, LANES - GLA_GATE_RANK), w.dtype)], axis=1).astype(BF16)
        y_rwkv, y_att, y_gla = _inproj(h, mix_norm_g[layer].reshape(1, D), wr, wa, wgl, cos, sa, sb)
        vres = None if layer == 0 else (rwkv_v0[layer - 1], rwkv_v1[layer - 1], rwkv_v2[layer - 1])
        o_rwkv, v_first = _rwkv_mixer(y_rwkv, S, rwkv_mu[layer], rwkv_w0[layer], rwkv_w2[layer], rwkv_a0[layer],
                                      rwkv_a2[layer], rwkv_g2[layer], rwkv_k_k[layer], rwkv_k_a[layer],
                                      rwkv_r_k[layer], rwkv_ln_g[layer], rwkv_ln_b[layer], v_first, vres)
        att_parts = [t for dil in DILATIONS for t in _att_branch(y_att, B, S, dil)]
        o_gla = _gla_mixer(y_gla, S, gla_gate_up[layer], gla_gate_b[layer], gla_norm_g[layer])
        h = _outproj(o_rwkv, att_parts, o_gla, w_out[layer].astype(BF16), h)
        i = layer // 2
        g = ffn_norm_g[layer].reshape(1, D)
        if layer % 2 == 0:
            h = _ffn(h, g, ffn_w_gate[i].astype(BF16), ffn_w_up[i].astype(BF16), ffn_w_down[i].astype(BF16))
        else:
            wrt = jnp.pad(moe_router[i], ((0, 0), (0, LANES - N_EXPERTS)))
            h = _moe(h, g, wrt, moe_w_gate[i].astype(BF16), moe_w_up[i].astype(BF16), moe_w_down[i].astype(BF16),
                     final_norm_g.reshape(1, D), layer == DEPTH - 1)
    return h.reshape(B, S, D)
```

```python
import functools

import jax
import jax.numpy as jnp
from jax import lax
from jax.experimental import pallas as pl
from jax.experimental.pallas import tpu as pltpu

D_MODEL = 1024
DEPTH = 2
HEAD_DIM = 64
RWKV_W = 256
RWKV_H = 4
ATT_W = 384
GLA_W = 384
GLA_H = 6
RWKV_DECAY_RANK = 32
RWKV_AAA_RANK = 32
RWKV_GATE_RANK = 64
RWKV_GN_EPS = 64e-5
GLA_GATE_RANK = 16
GLA_GATE_TAU = 16.0
CHUNK = 64
DILATIONS = (1, 4, 16)
ATT_BLOCK = 128
ROPE_THETA = 500000.0
ROPE_DIMS = 16
ROPE_HALF = 8
FFN_DENSE = 2816
N_EXPERTS = 8
FFN_EXPERT = 3584
RMS_EPS = 1e-5
NEG_INF = -1e30

RWKV_COLS = 3 * RWKV_W + RWKV_DECAY_RANK + RWKV_AAA_RANK + RWKV_GATE_RANK
ATT_COLS = 3 * ATT_W
LANES = 128
GLA_COLS_PAD = 4 * GLA_W + LANES

VMEM_LIMIT = 56 * 1024 * 1024

F32 = jnp.float32
BF16 = jnp.bfloat16
NN = (((1,), (0,)), ((), ()))
NT = (((1,), (1,)), ((), ()))
TN = (((0,), (0,)), ((), ()))
BNN = (((2,), (1,)), ((0,), (0,)))
BNT = (((2,), (2,)), ((0,), (0,)))
BTN = (((1,), (1,)), ((0,), (0,)))


def _cparams(*sem):
    return pltpu.CompilerParams(dimension_semantics=sem, vmem_limit_bytes=VMEM_LIMIT)


def _rms(x, g):
    return x * lax.rsqrt(jnp.mean(x * x, axis=-1, keepdims=True) + RMS_EPS) * g


def _sigmoid(x):
    return 1.0 / (1.0 + jnp.exp(-x))


def _log_sigmoid(x):
    return jnp.minimum(x, 0.0) - jnp.log(1.0 + jnp.exp(-jnp.abs(x)))


def _split_hi_lo(a):
    hi = a.astype(BF16)
    return hi, (a - hi.astype(F32)).astype(BF16)


def _dot3(a, b, dims=NN):
    a_hi, a_lo = _split_hi_lo(a)
    b_hi, b_lo = _split_hi_lo(b)
    d = lambda x, y: lax.dot_general(x, y, dims, preferred_element_type=F32)
    return d(a_hi, b_hi) + d(a_lo, b_hi) + d(a_hi, b_lo)


def _dot_exact_lhs(l_bf16, x):
    x_hi, x_lo = _split_hi_lo(x)
    return jnp.dot(l_bf16, x_hi, preferred_element_type=F32) + jnp.dot(l_bf16, x_lo, preferred_element_type=F32)


def _dot_exact_rhs(x, r_bf16):
    x_hi, x_lo = _split_hi_lo(x)
    return jnp.dot(x_hi, r_bf16, preferred_element_type=F32) + jnp.dot(x_lo, r_bf16, preferred_element_type=F32)


def _bdot(a, b, dims, passes):
    d = lambda x, y: lax.dot_general(x, y, dims, preferred_element_type=F32)
    if passes == 1:
        return d(a.astype(BF16), b.astype(BF16))
    a_hi, a_lo = _split_hi_lo(a)
    b_hi, b_lo = _split_hi_lo(b)
    return d(a_hi, b_hi) + d(a_lo, b_hi) + d(a_hi, b_lo)


def _head_ones(width):
    idx = jnp.arange(width) // HEAD_DIM
    return (idx[:, None] == idx[None, :]).astype(BF16)


def _chunk_selectors(tile):
    t = jnp.arange(tile)
    same_chunk = t[:, None] // CHUNK == t[None, :] // CHUNK
    return (same_chunk & (t[None, :] <= t[:, None])).astype(BF16), same_chunk.astype(BF16)


def _rope_table_kernel(pos_ref, invf_ref, cos_ref, sa_ref, sb_ref):
    ang = pos_ref[...].astype(F32) * invf_ref[...]
    lane = lax.broadcasted_iota(jnp.int32, ang.shape, 1) % HEAD_DIM
    c = jnp.cos(ang)
    s = jnp.sin(ang)
    cos_ref[...] = jnp.where(lane < ROPE_DIMS, c, 1.0)
    sa_ref[...] = jnp.where((lane >= ROPE_HALF) & (lane < ROPE_DIMS), s, 0.0)
    sb_ref[...] = jnp.where(lane < ROPE_HALF, -s, 0.0)


def _rope_tables(positions):
    m = positions.size
    tm = 2048
    lane = jnp.arange(LANES) % ROPE_HALF
    invf = (ROPE_THETA ** (-lane.astype(F32) / ROPE_HALF)).reshape(1, LANES)
    out = jax.ShapeDtypeStruct((m, LANES), F32)
    row = pl.BlockSpec((tm, LANES), lambda i: (i, 0))
    return pl.pallas_call(
        _rope_table_kernel,
        grid=(m // tm,),
        in_specs=[pl.BlockSpec((tm, 1), lambda i: (i, 0)), pl.BlockSpec((1, LANES), lambda i: (0, 0))],
        out_specs=[row, row, row],
        out_shape=[out, out, out],
        compiler_params=_cparams("parallel"),
        name="rope_tables",
    )(positions.reshape(m, 1), invf)


def _inproj_kernel(x_ref, g_ref, wr_ref, wa_ref, wg_ref, cos_ref, sa_ref, sb_ref, yr_ref, ya_ref, yg_ref):
    xb = _rms(x_ref[...], g_ref[...]).astype(BF16)
    yr_ref[...] = jnp.dot(xb, wr_ref[...], preferred_element_type=F32)
    yg_ref[...] = jnp.dot(xb, wg_ref[...], preferred_element_type=F32)
    ya = jnp.dot(xb, wa_ref[...], preferred_element_type=F32)
    cos, sa, sb = cos_ref[...], sa_ref[...], sb_ref[...]
    for j in range(2 * ATT_W // LANES):
        blk = ya[:, j * LANES:(j + 1) * LANES]
        rot = blk * cos + pltpu.roll(blk, ROPE_HALF, 1) * sa + pltpu.roll(blk, LANES - ROPE_HALF, 1) * sb
        if j < ATT_W // LANES:
            rot = rot * HEAD_DIM ** -0.5
        ya_ref[:, j * LANES:(j + 1) * LANES] = rot.astype(ya_ref.dtype)
    ya_ref[:, 2 * ATT_W:] = ya[:, 2 * ATT_W:].astype(ya_ref.dtype)


def _inproj(h, g, wr, wa, wg, cos, sa, sb):
    m = h.shape[0]
    tm = 512
    row = lambda w: pl.BlockSpec((tm, w), lambda i: (i, 0))
    full = lambda a: pl.BlockSpec(a.shape, lambda i: (0, 0))
    return pl.pallas_call(
        _inproj_kernel,
        grid=(m // tm,),
        in_specs=[row(D_MODEL), full(g), full(wr), full(wa), full(wg), row(LANES), row(LANES), row(LANES)],
        out_specs=[row(RWKV_COLS), row(ATT_COLS), row(GLA_COLS_PAD)],
        out_shape=[jax.ShapeDtypeStruct((m, RWKV_COLS), F32),
                   jax.ShapeDtypeStruct((m, ATT_COLS), BF16),
                   jax.ShapeDtypeStruct((m, GLA_COLS_PAD), F32)],
        compiler_params=_cparams("parallel"),
        name="inproj",
    )(h, g, wr, wa, wg, cos, sa, sb)


RWKV_GROUP = 4


def _rwkv_kernel(*refs, has_vres, tile):
    nc = tile // CHUNK
    if has_vres:
        (y_ref, vf_ref, mu_ref, w0_ref, w2_ref, a0_ref, a2_ref, g2_ref, kk_ref, ka_ref, rk_ref, lng_ref, lnb_ref,
         psel_ref, tsel_ref, hones_ref, v0_ref, v1_ref, v2_ref, o_ref, *scratch) = refs
    else:
        (y_ref, mu_ref, w0_ref, w2_ref, a0_ref, a2_ref, g2_ref, kk_ref, ka_ref, rk_ref, lng_ref, lnb_ref,
         psel_ref, tsel_ref, hones_ref, o_ref, vfo_ref, *scratch) = refs
    (state_ref, prev_ref, at_ref, rt_ref, bt_ref, kt_ref, bh_ref, kh_ref, vv_ref, gam_ref,
     qt_ref, ol_ref, gm_ref, hm_ref, os_ref) = scratch

    @pl.when(pl.program_id(1) == 0)
    def _():
        state_ref[...] = jnp.zeros_like(state_ref)
        prev_ref[...] = jnp.zeros_like(prev_ref)

    y = y_ref[...]
    row = lax.broadcasted_iota(jnp.int32, (tile, 1), 0)
    ysh = jnp.where(row == 0, prev_ref[...], pltpu.roll(y, 1, 0))
    prev_ref[...] = y[tile - 1:tile, :]
    ym = y + (ysh - y) * mu_ref[...]
    r = ym[:, 0:RWKV_W]
    k = ym[:, RWKV_W:2 * RWKV_W]
    v = ym[:, 2 * RWKV_W:3 * RWKV_W]
    x6 = ym[:, 3 * RWKV_W:]
    wl = w0_ref[...] + _dot3(jnp.tanh(x6), w2_ref[...])
    lw = -jnp.exp(_log_sigmoid(wl) - 0.5)
    a = _sigmoid(a0_ref[...] + _dot3(x6, a2_ref[...]))
    g = _dot3(_sigmoid(x6), g2_ref[...])
    if has_vres:
        v = v + (vf_ref[...] - v) * _sigmoid(v0_ref[...] + _dot3(_dot3(v, v1_ref[...]), v2_ref[...]))
    else:
        vfo_ref[...] = v

    head_ones = hones_ref[...]
    kk = k * kk_ref[...]
    kk = kk * lax.rsqrt(jnp.maximum(_dot_exact_rhs(kk * kk, head_ones), 1e-24))
    k2 = k * (1.0 + (a - 1.0) * ka_ref[...])
    bonus = _dot_exact_rhs(r * k2 * rk_ref[...], head_ones) * v

    c = _dot_exact_lhs(psel_ref[...], lw)
    c_last = _dot_exact_lhs(tsel_ref[...], lw)
    b = kk * a
    e_neg = jnp.exp(-c)
    e_end = jnp.exp(c_last - c)

    def put(ref, val):
        for h in range(RWKV_H):
            ref[h] = val[:, h * HEAD_DIM:(h + 1) * HEAD_DIM].reshape(nc, CHUNK, HEAD_DIM)

    put(rt_ref, r * jnp.exp(c))
    put(at_ref, -kk * jnp.exp(c - lw))
    put(bt_ref, b * e_neg)
    put(kt_ref, k2 * e_neg)
    put(bh_ref, b * e_end)
    put(kh_ref, k2 * e_end)
    put(vv_ref, v)
    put(gam_ref, jnp.exp(c_last))

    n = RWKV_H * RWKV_GROUP
    qi = lax.broadcasted_iota(jnp.int32, (n, CHUNK, CHUNK), 1)
    qj = lax.broadcasted_iota(jnp.int32, (n, CHUNK, CHUNK), 2)
    strict = qj < qi
    incl = qj <= qi
    eye = qj == qi
    blk16 = qi // 16 == qj // 16
    blk32 = qi // 32 == qj // 32

    def group_body(gi, carry):
        cs = pl.ds(gi * RWKV_GROUP, RWKV_GROUP)
        ld = lambda ref: ref[:, cs].reshape(n, CHUNK, HEAD_DIM)
        at, rt, bt, kt, bh, kh, vv = (ld(x) for x in (at_ref, rt_ref, bt_ref, kt_ref, bh_ref, kh_ref, vv_ref))
        gam = ld(gam_ref)[:, 0:1, :]
        a_ab = jnp.where(strict, _bdot(at, bt, BNT, 1), 0.0)
        a_ak = jnp.where(strict, _bdot(at, kt, BNT, 1), 0.0)
        p_b = jnp.where(incl, _bdot(rt, bt, BNT, 1), 0.0)
        p_k = jnp.where(incl, _bdot(rt, kt, BNT, 1), 0.0)
        d1 = jnp.where(blk16, a_ab, 0.0)
        d2 = _bdot(d1, d1, BNN, 3)
        d4 = _bdot(d2, d2, BNN, 3)
        d8 = _bdot(d4, d4, BNN, 3)
        t = jnp.where(eye, 1.0, d1)
        t = t + _bdot(t, d2, BNN, 3)
        t = t + _bdot(t, d4, BNN, 3)
        t = t + _bdot(t, d8, BNN, 3)
        t = t + _bdot(_bdot(t, jnp.where(blk32 & ~blk16, a_ab, 0.0), BNN, 3), t, BNN, 3)
        t = t + _bdot(_bdot(t, jnp.where(blk32, 0.0, a_ab), BNN, 3), t, BNN, 3)
        w = _bdot(t, at, BNN, 1)
        u_loc = _bdot(t, _bdot(a_ak, vv, BNN, 1), BNN, 1)

        def st(ref, val):
            ref[:, cs] = val.reshape(RWKV_H, RWKV_GROUP, CHUNK, HEAD_DIM)

        st(qt_ref, rt + _bdot(p_b, w, BNN, 1))
        st(ol_ref, _bdot(p_b, u_loc, BNN, 1) + _bdot(p_k, vv, BNN, 1))
        st(gm_ref, _bdot(w, bh, BTN, 1) + jnp.where(eye, gam, 0.0))
        st(hm_ref, _bdot(u_loc, bh, BTN, 1) + _bdot(vv, kh, BTN, 1))
        return carry

    lax.fori_loop(0, nc // RWKV_GROUP, group_body, 0)

    def chunk_body(ci, s):
        o = _bdot(qt_ref[:, ci], s, BNT, 3) + ol_ref[:, ci]
        rows = pl.ds(pl.multiple_of(ci * CHUNK, CHUNK), CHUNK)
        for h in range(RWKV_H):
            os_ref[rows, h * HEAD_DIM:(h + 1) * HEAD_DIM] = o[h]
        return _bdot(s, gm_ref[:, ci], BNN, 3) + hm_ref[:, ci]

    state_ref[...] = lax.fori_loop(0, nc, chunk_body, state_ref[...])

    o = os_ref[...]
    mean = _dot_exact_rhs(o, head_ones) * (1.0 / HEAD_DIM)
    oc = o - mean
    var = _dot_exact_rhs(oc * oc, head_ones) * (1.0 / HEAD_DIM)
    on = oc * lax.rsqrt(var + RWKV_GN_EPS) * lng_ref[...] + lnb_ref[...]
    o_ref[...] = (on + bonus) * g


def _rwkv_mixer(y, seq_len, mu, w0, w2, a0, a2, g2, k_k, k_a, r_k, ln_g, ln_b, v_first, vres):
    m = y.shape[0]
    tile = 512
    nt = seq_len // tile
    has_vres = vres is not None
    row = lambda wd: pl.BlockSpec((tile, wd), lambda b, j: (b * nt + j, 0))
    full = lambda arr: pl.BlockSpec(arr.shape, lambda b, j: (0, 0))
    vec = lambda t: t.reshape(1, -1)
    pad_rows = lambda w, start: jnp.zeros((LANES, RWKV_W), F32).at[start:start + w.shape[0]].set(w)
    consts = [vec(mu), vec(w0), pad_rows(w2, 0), vec(a0), pad_rows(a2, RWKV_DECAY_RANK),
              pad_rows(g2, RWKV_DECAY_RANK + RWKV_AAA_RANK), vec(k_k), vec(k_a), vec(r_k), vec(ln_g), vec(ln_b),
              *_chunk_selectors(tile), _head_ones(RWKV_W)]
    args = [y] + ([v_first] if has_vres else []) + consts
    in_specs = [row(RWKV_COLS)] + ([row(RWKV_W)] if has_vres else []) + [full(c) for c in consts]
    out_shape = [jax.ShapeDtypeStruct((m, RWKV_W), F32)]
    out_specs = [row(RWKV_W)]
    if has_vres:
        v0, v1, v2 = vres
        extra = [vec(v0), jnp.pad(v1, ((0, 0), (0, LANES - v1.shape[1]))),
                 jnp.pad(v2, ((0, LANES - v2.shape[0]), (0, 0)))]
        args += extra
        in_specs += [full(c) for c in extra]
    else:
        out_shape.append(jax.ShapeDtypeStruct((m, RWKV_W), F32))
        out_specs.append(row(RWKV_W))
    unit_buf = pltpu.VMEM((RWKV_H, tile // CHUNK, CHUNK, HEAD_DIM), F32)
    outs = pl.pallas_call(
        functools.partial(_rwkv_kernel, has_vres=has_vres, tile=tile),
        grid=(m // seq_len, nt),
        in_specs=in_specs, out_specs=out_specs, out_shape=out_shape,
        scratch_shapes=[pltpu.VMEM((RWKV_H, HEAD_DIM, HEAD_DIM), F32), pltpu.VMEM((1, RWKV_COLS), F32)]
                       + [unit_buf] * 12 + [pltpu.VMEM((tile, RWKV_W), F32)],
        compiler_params=_cparams("parallel", "arbitrary"),
        name="rwkv7",
    )(*args)
    return (outs[0], v_first) if has_vres else (outs[0], outs[1])


ATT_PAD = ATT_BLOCK * max(DILATIONS)
ATT_UNROLL = 4


def _att_kernel(q_ref, k_ref, v_ref, o_ref, qs_ref, ks_ref, vs_ref, m_ref, n_ref, d_ref, *, seq_len):
    qs_ref[...] = q_ref[...].astype(F32)
    ks_ref[0:ATT_PAD, :] = jnp.zeros((ATT_PAD, LANES), F32)
    vs_ref[0:ATT_PAD, :] = jnp.zeros((ATT_PAD, LANES), F32)
    ks_ref[ATT_PAD:, :] = k_ref[...].astype(F32)
    vs_ref[ATT_PAD:, :] = v_ref[...].astype(F32)

    qi = lax.broadcasted_iota(jnp.int32, (ATT_BLOCK, 2 * ATT_BLOCK), 0)
    kc = lax.broadcasted_iota(jnp.int32, (ATT_BLOCK, 2 * ATT_BLOCK), 1)
    band = (kc >= qi) & (kc <= qi + ATT_BLOCK)
    head0 = lax.broadcasted_iota(jnp.int32, (ATT_BLOCK, LANES), 1) < HEAD_DIM

    for g, dil in enumerate(DILATIONS):
        nb = seq_len // dil // ATT_BLOCK

        def body(it, carry, g=g, dil=dil, nb=nb):
            r = it // nb
            jq = it % nb
            start = r + jq * (ATT_BLOCK * dil)
            rows = pl.ds(start, ATT_BLOCK, stride=dil)
            krows = pl.ds(start + (ATT_PAD - ATT_BLOCK * dil), 2 * ATT_BLOCK, stride=dil)
            q = qs_ref[rows, :].astype(BF16)
            k = ks_ref[krows, :].astype(BF16)
            v = vs_ref[krows, :].astype(BF16)
            valid = band & ((kc >= ATT_BLOCK) | (jq > 0))
            mxs, pvs, dens = [], [], []
            for hh in range(2):
                qh = jnp.where(head0 if hh == 0 else jnp.logical_not(head0), q, jnp.zeros_like(q))
                s = jnp.where(valid, lax.dot_general(qh, k, NT, preferred_element_type=F32), NEG_INF)
                mx = jnp.max(s, axis=-1, keepdims=True)
                p = jnp.exp(s - mx)
                mxs.append(mx)
                dens.append(jnp.sum(p, axis=-1, keepdims=True))
                pvs.append(jnp.dot(p.astype(BF16), v, preferred_element_type=F32))
            mx = jnp.where(head0, mxs[0], mxs[1])
            den = jnp.where(head0, dens[0], dens[1])
            pv = jnp.where(head0, pvs[0], pvs[1])
            if g > 0:
                m_old = m_ref[rows, :]
                m_new = jnp.maximum(m_old, mx)
                a_old = jnp.exp(m_old - m_new)
                a_new = jnp.exp(mx - m_new)
                pv = a_old * n_ref[rows, :] + a_new * pv
                den = a_old * d_ref[rows, :] + a_new * den
                mx = m_new
            if g == len(DILATIONS) - 1:
                o_ref[rows, :] = pv / den
            else:
                m_ref[rows, :] = mx
                n_ref[rows, :] = pv
                d_ref[rows, :] = den
            return carry

        lax.fori_loop(0, dil * nb, body, 0, unroll=ATT_UNROLL)


def _att_mixer(y_att, seq_len):
    m = y_att.shape[0]
    npair = ATT_W // LANES
    col = lambda c: pl.BlockSpec((seq_len, LANES), lambda b, hp: (b, c * npair + hp))
    seq_buf = pltpu.VMEM((seq_len, LANES), F32)
    pad_buf = pltpu.VMEM((ATT_PAD + seq_len, LANES), F32)
    return pl.pallas_call(
        functools.partial(_att_kernel, seq_len=seq_len),
        grid=(m // seq_len, npair),
        in_specs=[col(0), col(1), col(2)],
        out_specs=pl.BlockSpec((seq_len, LANES), lambda b, hp: (b, hp)),
        out_shape=jax.ShapeDtypeStruct((m, ATT_W), F32),
        scratch_shapes=[seq_buf, pad_buf, pad_buf, seq_buf, seq_buf, seq_buf],
        compiler_params=_cparams("parallel", "parallel"),
        name="dilated_att",
    )(y_att, y_att, y_att)


def _gla_kernel(y_ref, gu_ref, gb_ref, ng_ref, psel_ref, tsel_ref, hones_ref, o_ref, state_ref, qi_ref, ki_ref, ke_ref, gam_ref, os_ref, *, tile):
    @pl.when(pl.program_id(1) == 0)
    def _():
        state_ref[...] = jnp.zeros_like(state_ref)

    q = y_ref[:, 0:GLA_W] * HEAD_DIM ** -0.5
    k = y_ref[:, GLA_W:2 * GLA_W]
    og = y_ref[:, 3 * GLA_W:4 * GLA_W]
    lg = _log_sigmoid(_dot3(y_ref[:, 4 * GLA_W:], gu_ref[...]) + gb_ref[...]) * (1.0 / GLA_GATE_TAU)
    b = _dot_exact_lhs(psel_ref[...], lg)
    b_last = _dot_exact_lhs(tsel_ref[...], lg)
    qi_ref[...] = (q * jnp.exp(b)).astype(BF16)
    ki_ref[...] = (k * jnp.exp(-b)).astype(BF16)
    ke_ref[...] = (k * jnp.exp(b_last - b)).astype(BF16)
    gam_ref[...] = jnp.exp(b_last)

    ci = lax.broadcasted_iota(jnp.int32, (CHUNK, CHUNK), 0)
    cj = lax.broadcasted_iota(jnp.int32, (CHUNK, CHUNK), 1)
    causal = cj <= ci

    def chunk_body(c, carry):
        rows = pl.ds(pl.multiple_of(c * CHUNK, CHUNK), CHUNK)
        for h in range(GLA_H):
            sl = slice(h * HEAD_DIM, (h + 1) * HEAD_DIM)
            q_in, k_in, k_end = qi_ref[rows, sl], ki_ref[rows, sl], ke_ref[rows, sl]
            v = y_ref[rows, 2 * GLA_W + h * HEAD_DIM:2 * GLA_W + (h + 1) * HEAD_DIM].astype(BF16)
            gam = gam_ref[rows, sl][0:1, :]
            st = state_ref[h]
            att = jnp.where(causal, lax.dot_general(q_in, k_in, NT, preferred_element_type=F32), 0.0)
            os_ref[rows, sl] = (jnp.dot(att.astype(BF16), v, preferred_element_type=F32)
                                + lax.dot_general(q_in, st.astype(BF16), NT, preferred_element_type=F32))
            state_ref[h] = st * gam + lax.dot_general(v, k_end, TN, preferred_element_type=F32)
        return carry

    lax.fori_loop(0, tile // CHUNK, chunk_body, 0)

    o = os_ref[...]
    ms = _dot_exact_rhs(o * o, hones_ref[...]) * (1.0 / HEAD_DIM)
    o_ref[...] = o * lax.rsqrt(ms + RMS_EPS) * ng_ref[...] * (og * _sigmoid(og))


def _gla_mixer(y, seq_len, gate_up, gate_b, norm_g):
    m = y.shape[0]
    tile = 512
    nt = seq_len // tile
    gu = jnp.pad(gate_up, ((0, LANES - gate_up.shape[0]), (0, 0)))
    gb = gate_b.reshape(1, GLA_W)
    ng = jnp.tile(norm_g, GLA_H).reshape(1, GLA_W)
    consts = [gu, gb, ng, *_chunk_selectors(tile), _head_ones(GLA_W)]
    full = lambda a: pl.BlockSpec(a.shape, lambda b, j: (0, 0))
    return pl.pallas_call(
        functools.partial(_gla_kernel, tile=tile),
        grid=(m // seq_len, nt),
        in_specs=[pl.BlockSpec((tile, GLA_COLS_PAD), lambda b, j: (b * nt + j, 0))] + [full(c) for c in consts],
        out_specs=pl.BlockSpec((tile, GLA_W), lambda b, j: (b * nt + j, 0)),
        out_shape=jax.ShapeDtypeStruct((m, GLA_W), F32),
        scratch_shapes=[pltpu.VMEM((GLA_H, HEAD_DIM, HEAD_DIM), F32)] + [pltpu.VMEM((tile, GLA_W), BF16)] * 3
                       + [pltpu.VMEM((tile, GLA_W), F32)] * 2,
        compiler_params=_cparams("parallel", "arbitrary"),
        name="gla",
    )(y, *consts)


def _outproj_kernel(or_ref, oa_ref, og_ref, w_ref, h_ref, o_ref):
    acc = jnp.dot(or_ref[...].astype(BF16), w_ref[0:RWKV_W, :], preferred_element_type=F32)
    acc += jnp.dot(oa_ref[...].astype(BF16), w_ref[RWKV_W:RWKV_W + ATT_W, :], preferred_element_type=F32)
    acc += jnp.dot(og_ref[...].astype(BF16), w_ref[RWKV_W + ATT_W:, :], preferred_element_type=F32)
    o_ref[...] = h_ref[...] + acc


def _outproj(o_rwkv, o_att, o_gla, w, h):
    m = h.shape[0]
    tm = 512
    row = lambda wd: pl.BlockSpec((tm, wd), lambda i: (i, 0))
    return pl.pallas_call(
        _outproj_kernel,
        grid=(m // tm,),
        in_specs=[row(RWKV_W), row(ATT_W), row(GLA_W), pl.BlockSpec(w.shape, lambda i: (0, 0)), row(D_MODEL)],
        out_specs=row(D_MODEL),
        out_shape=jax.ShapeDtypeStruct(h.shape, F32),
        compiler_params=_cparams("parallel"),
        name="outproj",
    )(o_rwkv, o_att, o_gla, w, h)


def _ffn_kernel(x_ref, g_ref, wg_ref, wu_ref, wd_ref, o_ref, xn_ref, acc_ref):
    f = pl.program_id(1)

    @pl.when(f == 0)
    def _():
        xn_ref[...] = _rms(x_ref[...], g_ref[...]).astype(xn_ref.dtype)
        acc_ref[...] = jnp.zeros_like(acc_ref)

    xn = xn_ref[...]
    gate = jnp.dot(xn, wg_ref[...], preferred_element_type=F32)
    up = jnp.dot(xn, wu_ref[...], preferred_element_type=F32)
    act = (gate * _sigmoid(gate) * up).astype(BF16)
    acc_ref[...] += jnp.dot(act, wd_ref[...], preferred_element_type=F32)

    @pl.when(f == pl.num_programs(1) - 1)
    def _():
        o_ref[...] = x_ref[...] + acc_ref[...]


def _ffn(h, g, wg, wu, wd):
    m = h.shape[0]
    tm, tf = 512, FFN_DENSE // 2
    row = pl.BlockSpec((tm, D_MODEL), lambda i, f: (i, 0))
    return pl.pallas_call(
        _ffn_kernel,
        grid=(m // tm, FFN_DENSE // tf),
        in_specs=[row, pl.BlockSpec(g.shape, lambda i, f: (0, 0)),
                  pl.BlockSpec((D_MODEL, tf), lambda i, f: (0, f)),
                  pl.BlockSpec((D_MODEL, tf), lambda i, f: (0, f)),
                  pl.BlockSpec((tf, D_MODEL), lambda i, f: (f, 0))],
        out_specs=row,
        out_shape=jax.ShapeDtypeStruct(h.shape, F32),
        scratch_shapes=[pltpu.VMEM((tm, D_MODEL), BF16), pltpu.VMEM((tm, D_MODEL), F32)],
        compiler_params=_cparams("parallel", "arbitrary"),
        name="ffn_dense",
    )(h, g, wg, wu, wd)


MOE_BLOCK = 1024
MOE_SUB = 256
MOE_FT = 896


def _router_kernel(x_ref, g_ref, wr_ref, tri_ref, xnt_ref, rank_ref, gate_ref, cnt_ref):
    xn = _rms(x_ref[...], g_ref[...])
    xnt_ref[...] = xn.T.astype(BF16)
    logits = jnp.dot(xn, wr_ref[...], preferred_element_type=F32, precision=lax.Precision.HIGHEST)
    lane = lax.broadcasted_iota(jnp.int32, logits.shape, 1)
    lg = jnp.where(lane < N_EXPERTS, logits, NEG_INF)
    m1 = jnp.max(lg, axis=-1, keepdims=True)
    i1 = jnp.min(jnp.where(lg == m1, lane, LANES), axis=-1, keepdims=True)
    lg2 = jnp.where(lane == i1, NEG_INF, lg)
    m2 = jnp.max(lg2, axis=-1, keepdims=True)
    i2 = jnp.min(jnp.where(lg2 == m2, lane, LANES), axis=-1, keepdims=True)
    e2 = jnp.exp(m2 - m1)
    hot1 = lane == i1
    hot2 = lane == i2
    onehot = (hot1 | hot2).astype(F32)
    rank = jnp.dot(tri_ref[...], onehot.astype(BF16), preferred_element_type=F32)
    rank = jnp.where(hot1 | hot2, rank, -1.0)
    gate = jnp.where(hot1, 1.0 / (1.0 + e2), jnp.where(hot2, e2 / (1.0 + e2), 0.0))
    rank_ref[0] = rank.T[:N_EXPERTS]
    gate_ref[0] = gate.T[:N_EXPERTS]
    cnt_ref[0] = jnp.broadcast_to(jnp.sum(onehot, axis=0, keepdims=True), (8, LANES)).astype(jnp.int32)


def _route(h, g, w_router):
    m = h.shape[0]
    sb = MOE_BLOCK
    t = jnp.arange(sb)
    tri = (t[None, :] < t[:, None]).astype(BF16)
    full = lambda a: pl.BlockSpec(a.shape, lambda i: (0,) * a.ndim)
    per_expert = pl.BlockSpec((1, N_EXPERTS, sb), lambda i: (i, 0, 0))
    return pl.pallas_call(
        _router_kernel,
        grid=(m // sb,),
        in_specs=[pl.BlockSpec((sb, D_MODEL), lambda i: (i, 0)), full(g), full(w_router), full(tri)],
        out_specs=[pl.BlockSpec((D_MODEL, sb), lambda i: (0, i)), per_expert, per_expert,
                   pl.BlockSpec((1, 8, LANES), lambda i: (i, 0, 0))],
        out_shape=[jax.ShapeDtypeStruct((D_MODEL, m), BF16), jax.ShapeDtypeStruct((m // sb, N_EXPERTS, sb), F32),
                   jax.ShapeDtypeStruct((m // sb, N_EXPERTS, sb), F32),
                   jax.ShapeDtypeStruct((m // sb, 8, LANES), jnp.int32)],
        compiler_params=_cparams("parallel"),
        name="moe_router",
    )(h, g, w_router, tri)


def _moe_kernel(cnt_ref, x_ref, xnt_ref, rank_ref, gate_ref, wg_ref, wu_ref, wd_ref, fg_ref, o_ref,
                xs_ref, y_ref, acc_ref, *, final_norm):
    i, e, f = pl.program_id(0), pl.program_id(1), pl.program_id(2)
    last_f = f == pl.num_programs(2) - 1
    n_sub = (cnt_ref[i * N_EXPERTS + e] + (MOE_SUB - 1)) // MOE_SUB

    @pl.when((e == 0) & (f == 0))
    def _():
        acc_ref[...] = jnp.zeros_like(acc_ref)

    def selection(j):
        want = lax.broadcasted_iota(jnp.int32, (MOE_SUB, MOE_BLOCK), 0) + j * MOE_SUB
        return (rank_ref[0, pl.ds(e, 1), :] == want.astype(F32)).astype(BF16)

    def sub_body(j, carry):
        @pl.when(f == 0)
        def _():
            xs_ref[j] = lax.dot_general(xnt_ref[...], selection(j), NT, preferred_element_type=F32).astype(BF16)
            y_ref[j] = jnp.zeros((D_MODEL, MOE_SUB), F32)

        xs = xs_ref[j]
        gate = jnp.dot(wg_ref[0], xs, preferred_element_type=F32)
        up = jnp.dot(wu_ref[0], xs, preferred_element_type=F32)
        act = (gate * _sigmoid(gate) * up).astype(BF16)
        y_ref[j] += jnp.dot(wd_ref[0], act, preferred_element_type=F32)

        @pl.when(last_f)
        def _():
            back = jnp.dot(y_ref[j].astype(BF16), selection(j), preferred_element_type=F32)
            acc_ref[...] += gate_ref[0, pl.ds(e, 1), :] * back

        return carry

    lax.fori_loop(0, n_sub, sub_body, 0)

    @pl.when((e == N_EXPERTS - 1) & last_f)
    def _():
        y = x_ref[...] + acc_ref[...].T
        o_ref[...] = _rms(y, fg_ref[...]) if final_norm else y


def _moe(h, g, w_router, wg_t, wu_t, wd_t, final_g, final_norm):
    m = h.shape[0]
    sb, ft = MOE_BLOCK, MOE_FT
    xnt, rank, gate, cnt = _route(h, g, w_router)
    counts = cnt[:, 0, :N_EXPERTS].reshape(-1)
    row = pl.BlockSpec((sb, D_MODEL), lambda i, e, f, c: (i, 0))
    per_expert = pl.BlockSpec((1, N_EXPERTS, sb), lambda i, e, f, c: (i, 0, 0))
    grid_spec = pltpu.PrefetchScalarGridSpec(
        num_scalar_prefetch=1,
        grid=(m // sb, N_EXPERTS, FFN_EXPERT // ft),
        in_specs=[row, pl.BlockSpec((D_MODEL, sb), lambda i, e, f, c: (0, i)), per_expert, per_expert,
                  pl.BlockSpec((1, ft, D_MODEL), lambda i, e, f, c: (e, f, 0)),
                  pl.BlockSpec((1, ft, D_MODEL), lambda i, e, f, c: (e, f, 0)),
                  pl.BlockSpec((1, D_MODEL, ft), lambda i, e, f, c: (e, 0, f)),
                  pl.BlockSpec((1, D_MODEL), lambda i, e, f, c: (0, 0))],
        out_specs=row,
        scratch_shapes=[pltpu.VMEM((sb // MOE_SUB, D_MODEL, MOE_SUB), BF16),
                        pltpu.VMEM((sb // MOE_SUB, D_MODEL, MOE_SUB), F32),
                        pltpu.VMEM((D_MODEL, sb), F32)],
    )
    return pl.pallas_call(
        functools.partial(_moe_kernel, final_norm=final_norm),
        grid_spec=grid_spec,
        out_shape=jax.ShapeDtypeStruct(h.shape, F32),
        compiler_params=_cparams("parallel", "arbitrary", "arbitrary"),
        name="moe",
    )(counts, h, xnt, rank, gate, wg_t, wu_t, wd_t, final_g)


def kernel(x, positions, mix_norm_g, w_in, rwkv_mu, rwkv_w0, rwkv_w2, rwkv_a0, rwkv_a2, rwkv_g2, rwkv_k_k, rwkv_k_a, rwkv_r_k, rwkv_ln_g, rwkv_ln_b, rwkv_v0, rwkv_v1, rwkv_v2, gla_gate_up, gla_gate_b, gla_norm_g, w_out, ffn_norm_g, ffn_w_gate, ffn_w_up, ffn_w_down, moe_router, moe_w_gate, moe_w_up, moe_w_down, final_norm_g):
    B, S, D = x.shape
    M = B * S
    h = x.reshape(M, D)
    cos, sa, sb = _rope_tables(positions)
    v_first = None
    for layer in range(DEPTH):
        w = w_in[layer]
        wr = w[:, :RWKV_COLS].astype(BF16)
        wa = w[:, RWKV_COLS:RWKV_COLS + ATT_COLS].astype(BF16)
        wgl = w[:, RWKV_COLS + ATT_COLS:]
        wgl = jnp.concatenate([wgl[:, :3 * GLA_W], wgl[:, 3 * GLA_W + GLA_GATE_RANK:],
                               wgl[:, 3 * GLA_W:3 * GLA_W + GLA_GATE_RANK],
                               jnp.zeros((D, LANES - GLA_GATE_RANK), w.dtype)], axis=1).astype(BF16)
        y_rwkv, y_att, y_gla = _inproj(h, mix_norm_g[layer].reshape(1, D), wr, wa, wgl, cos, sa, sb)
        vres = None if layer == 0 else (rwkv_v0[layer - 1], rwkv_v1[layer - 1], rwkv_v2[layer - 1])
        o_rwkv, v_first = _rwkv_mixer(y_rwkv, S, rwkv_mu[layer], rwkv_w0[layer], rwkv_w2[layer], rwkv_a0[layer],
                                      rwkv_a2[layer], rwkv_g2[layer], rwkv_k_k[layer], rwkv_k_a[layer],
                                      rwkv_r_k[layer], rwkv_ln_g[layer], rwkv_ln_b[layer], v_first, vres)
        o_att = _att_mixer(y_att, S)
        o_gla = _gla_mixer(y_gla, S, gla_gate_up[layer], gla_gate_b[layer], gla_norm_g[layer])
        h = _outproj(o_rwkv, o_att, o_gla, w_out[layer].astype(BF16), h)
        i = layer // 2
        g = ffn_norm_g[layer].reshape(1, D)
        if layer % 2 == 0:
            h = _ffn(h, g, ffn_w_gate[i].astype(BF16), ffn_w_up[i].astype(BF16), ffn_w_down[i].astype(BF16))
        else:
            wrt = jnp.pad(moe_router[i], ((0, 0), (0, LANES - N_EXPERTS)))
            to_t = lambda w: w.transpose(0, 2, 1).astype(BF16)
            h = _moe(h, g, wrt, to_t(moe_w_gate[i]), to_t(moe_w_up[i]), to_t(moe_w_down[i]),
                     final_norm_g.reshape(1, D), layer == DEPTH - 1)
    return h.reshape(B, S, D)
```

```python
import functools

import jax
import jax.numpy as jnp
from jax import lax
from jax.experimental import pallas as pl
from jax.experimental.pallas import tpu as pltpu

D_MODEL = 1024
DEPTH = 2
HEAD_DIM = 64
RWKV_W = 256
RWKV_H = 4
ATT_W = 384
GLA_W = 384
GLA_H = 6
RWKV_DECAY_RANK = 32
RWKV_AAA_RANK = 32
RWKV_GATE_RANK = 64
RWKV_GN_EPS = 64e-5
GLA_GATE_RANK = 16
GLA_GATE_TAU = 16.0
CHUNK = 64
DILATIONS = (1, 4, 16)
ATT_BLOCK = 128
ROPE_THETA = 500000.0
ROPE_DIMS = 16
ROPE_HALF = 8
FFN_DENSE = 2816
N_EXPERTS = 8
FFN_EXPERT = 3584
RMS_EPS = 1e-5
NEG_INF = -1e30

RWKV_COLS = 3 * RWKV_W + RWKV_DECAY_RANK + RWKV_AAA_RANK + RWKV_GATE_RANK
ATT_COLS = 3 * ATT_W
LANES = 128
GLA_COLS_PAD = 4 * GLA_W + LANES

VMEM_LIMIT = 56 * 1024 * 1024

F32 = jnp.float32
BF16 = jnp.bfloat16
NN = (((1,), (0,)), ((), ()))
NT = (((1,), (1,)), ((), ()))
TN = (((0,), (0,)), ((), ()))
BNN = (((2,), (1,)), ((0,), (0,)))
BNT = (((2,), (2,)), ((0,), (0,)))
BTN = (((1,), (1,)), ((0,), (0,)))


def _cparams(*sem):
    return pltpu.CompilerParams(dimension_semantics=sem, vmem_limit_bytes=VMEM_LIMIT)


def _rms(x, g):
    return x * lax.rsqrt(jnp.mean(x * x, axis=-1, keepdims=True) + RMS_EPS) * g


def _sigmoid(x):
    return 1.0 / (1.0 + jnp.exp(-x))


def _log_sigmoid(x):
    return jnp.minimum(x, 0.0) - jnp.log(1.0 + jnp.exp(-jnp.abs(x)))


def _split_hi_lo(a):
    hi = a.astype(BF16)
    return hi, (a - hi.astype(F32)).astype(BF16)


def _dot3(a, b, dims=NN):
    a_hi, a_lo = _split_hi_lo(a)
    b_hi, b_lo = _split_hi_lo(b)
    d = lambda x, y: lax.dot_general(x, y, dims, preferred_element_type=F32)
    return d(a_hi, b_hi) + d(a_lo, b_hi) + d(a_hi, b_lo)


def _dot_exact_rhs(x, r_bf16):
    x_hi, x_lo = _split_hi_lo(x)
    return jnp.dot(x_hi, r_bf16, preferred_element_type=F32) + jnp.dot(x_lo, r_bf16, preferred_element_type=F32)


def _bdot(a, b, dims, passes):
    d = lambda x, y: lax.dot_general(x, y, dims, preferred_element_type=F32)
    if passes == 1:
        return d(a.astype(BF16), b.astype(BF16))
    a_hi, a_lo = _split_hi_lo(a)
    b_hi, b_lo = _split_hi_lo(b)
    return d(a_hi, b_hi) + d(a_lo, b_hi) + d(a_hi, b_lo)


def _head_ones(width):
    idx = jnp.arange(width) // HEAD_DIM
    return (idx[:, None] == idx[None, :]).astype(BF16)


def _chunk_selectors():
    t = jnp.arange(CHUNK)
    return (t[None, :] <= t[:, None]).astype(BF16), jnp.ones((CHUNK, CHUNK), BF16)


def _chunk_sums(sel, x):
    nc = x.shape[0] // CHUNK
    hi, lo = _split_hi_lo(x.reshape(nc, CHUNK, x.shape[1]))
    s3 = jnp.broadcast_to(sel[None], (nc, CHUNK, CHUNK))
    d = lambda a, b: lax.dot_general(a, b, BNN, preferred_element_type=F32)
    return (d(s3, hi) + d(s3, lo)).reshape(x.shape)


def _rope_table_kernel(pos_ref, invf_ref, cos_ref, sa_ref, sb_ref):
    ang = pos_ref[...].astype(F32) * invf_ref[...]
    lane = lax.broadcasted_iota(jnp.int32, ang.shape, 1) % HEAD_DIM
    c = jnp.cos(ang)
    s = jnp.sin(ang)
    cos_ref[...] = jnp.where(lane < ROPE_DIMS, c, 1.0)
    sa_ref[...] = jnp.where((lane >= ROPE_HALF) & (lane < ROPE_DIMS), s, 0.0)
    sb_ref[...] = jnp.where(lane < ROPE_HALF, -s, 0.0)


def _rope_tables(positions):
    m = positions.size
    tm = 2048
    lane = jnp.arange(LANES) % ROPE_HALF
    invf = (ROPE_THETA ** (-lane.astype(F32) / ROPE_HALF)).reshape(1, LANES)
    out = jax.ShapeDtypeStruct((m, LANES), F32)
    row = pl.BlockSpec((tm, LANES), lambda i: (i, 0))
    return pl.pallas_call(
        _rope_table_kernel,
        grid=(m // tm,),
        in_specs=[pl.BlockSpec((tm, 1), lambda i: (i, 0)), pl.BlockSpec((1, LANES), lambda i: (0, 0))],
        out_specs=[row, row, row],
        out_shape=[out, out, out],
        compiler_params=_cparams("parallel"),
        name="rope_tables",
    )(positions.reshape(m, 1), invf)


def _inproj_kernel(x_ref, g_ref, wr_ref, wa_ref, wg_ref, cos_ref, sa_ref, sb_ref, yr_ref, ya_ref, yg_ref):
    xb = _rms(x_ref[...], g_ref[...]).astype(BF16)
    yr_ref[...] = jnp.dot(xb, wr_ref[...], preferred_element_type=F32)
    yg_ref[...] = jnp.dot(xb, wg_ref[...], preferred_element_type=F32)
    ya = jnp.dot(xb, wa_ref[...], preferred_element_type=F32)
    cos, sa, sb = cos_ref[...], sa_ref[...], sb_ref[...]
    for j in range(2 * ATT_W // LANES):
        blk = ya[:, j * LANES:(j + 1) * LANES]
        rot = blk * cos + pltpu.roll(blk, ROPE_HALF, 1) * sa + pltpu.roll(blk, LANES - ROPE_HALF, 1) * sb
        if j < ATT_W // LANES:
            rot = rot * HEAD_DIM ** -0.5
        ya_ref[:, j * LANES:(j + 1) * LANES] = rot.astype(ya_ref.dtype)
    ya_ref[:, 2 * ATT_W:] = ya[:, 2 * ATT_W:].astype(ya_ref.dtype)


def _inproj(h, g, wr, wa, wg, cos, sa, sb):
    m = h.shape[0]
    tm = 512
    row = lambda w: pl.BlockSpec((tm, w), lambda i: (i, 0))
    full = lambda a: pl.BlockSpec(a.shape, lambda i: (0, 0))
    return pl.pallas_call(
        _inproj_kernel,
        grid=(m // tm,),
        in_specs=[row(D_MODEL), full(g), full(wr), full(wa), full(wg), row(LANES), row(LANES), row(LANES)],
        out_specs=[row(RWKV_COLS), row(ATT_COLS), row(GLA_COLS_PAD)],
        out_shape=[jax.ShapeDtypeStruct((m, RWKV_COLS), F32),
                   jax.ShapeDtypeStruct((m, ATT_COLS), BF16),
                   jax.ShapeDtypeStruct((m, GLA_COLS_PAD), F32)],
        compiler_params=_cparams("parallel"),
        name="inproj",
    )(h, g, wr, wa, wg, cos, sa, sb)


RWKV_GROUP = 4


def _rwkv_kernel(*refs, has_vres, tile):
    nc = tile // CHUNK
    if has_vres:
        (y_ref, vf_ref, mu_ref, w0_ref, w2_ref, a0_ref, a2_ref, g2_ref, kk_ref, ka_ref, rk_ref, lng_ref, lnb_ref,
         psel_ref, tsel_ref, hones_ref, v0_ref, v1_ref, v2_ref, o_ref, *scratch) = refs
    else:
        (y_ref, mu_ref, w0_ref, w2_ref, a0_ref, a2_ref, g2_ref, kk_ref, ka_ref, rk_ref, lng_ref, lnb_ref,
         psel_ref, tsel_ref, hones_ref, o_ref, vfo_ref, *scratch) = refs
    (state_ref, prev_ref, at_ref, rt_ref, bt_ref, kt_ref, bh_ref, kh_ref, vv_ref, gam_ref,
     qt_ref, ol_ref, gm_ref, hm_ref, os_ref) = scratch

    @pl.when(pl.program_id(1) == 0)
    def _():
        state_ref[...] = jnp.zeros_like(state_ref)
        prev_ref[...] = jnp.zeros_like(prev_ref)

    y = y_ref[...]
    row = lax.broadcasted_iota(jnp.int32, (tile, 1), 0)
    ysh = jnp.where(row == 0, prev_ref[...], pltpu.roll(y, 1, 0))
    prev_ref[...] = y[tile - 1:tile, :]
    ym = y + (ysh - y) * mu_ref[...]
    r = ym[:, 0:RWKV_W]
    k = ym[:, RWKV_W:2 * RWKV_W]
    v = ym[:, 2 * RWKV_W:3 * RWKV_W]
    x6 = ym[:, 3 * RWKV_W:]
    wl = w0_ref[...] + _dot3(jnp.tanh(x6), w2_ref[...])
    lw = -jnp.exp(_log_sigmoid(wl) - 0.5)
    a = _sigmoid(a0_ref[...] + _dot3(x6, a2_ref[...]))
    g = _dot3(_sigmoid(x6), g2_ref[...])
    if has_vres:
        v = v + (vf_ref[...] - v) * _sigmoid(v0_ref[...] + _dot3(_dot3(v, v1_ref[...]), v2_ref[...]))
    else:
        vfo_ref[...] = v

    head_ones = hones_ref[...]
    kk = k * kk_ref[...]
    kk = kk * lax.rsqrt(jnp.maximum(_dot_exact_rhs(kk * kk, head_ones), 1e-24))
    k2 = k * (1.0 + (a - 1.0) * ka_ref[...])
    bonus = _dot_exact_rhs(r * k2 * rk_ref[...], head_ones) * v

    c = _chunk_sums(psel_ref[...], lw)
    c_last = _chunk_sums(tsel_ref[...], lw)
    b = kk * a
    e_neg = jnp.exp(-c)
    e_end = jnp.exp(c_last - c)

    def put(ref, val):
        for h in range(RWKV_H):
            ref[h] = val[:, h * HEAD_DIM:(h + 1) * HEAD_DIM].reshape(nc, CHUNK, HEAD_DIM)

    put(rt_ref, r * jnp.exp(c))
    put(at_ref, -kk * jnp.exp(c - lw))
    put(bt_ref, b * e_neg)
    put(kt_ref, k2 * e_neg)
    put(bh_ref, b * e_end)
    put(kh_ref, k2 * e_end)
    put(vv_ref, v)
    put(gam_ref, jnp.exp(c_last))

    n = RWKV_H * RWKV_GROUP
    qi = lax.broadcasted_iota(jnp.int32, (n, CHUNK, CHUNK), 1)
    qj = lax.broadcasted_iota(jnp.int32, (n, CHUNK, CHUNK), 2)
    strict = qj < qi
    incl = qj <= qi
    eye = qj == qi
    blk8 = qi // 8 == qj // 8
    blk16 = qi // 16 == qj // 16
    blk32 = qi // 32 == qj // 32

    def group_body(gi, carry):
        cs = pl.ds(gi * RWKV_GROUP, RWKV_GROUP)
        ld = lambda ref: ref[:, cs].reshape(n, CHUNK, HEAD_DIM)
        at, rt, bt, kt, bh, kh, vv = (ld(x) for x in (at_ref, rt_ref, bt_ref, kt_ref, bh_ref, kh_ref, vv_ref))
        gam = ld(gam_ref)[:, 0:1, :]
        a_ab = jnp.where(strict, _bdot(at, bt, BNT, 1), 0.0)
        a_ak = jnp.where(strict, _bdot(at, kt, BNT, 1), 0.0)
        p_b = jnp.where(incl, _bdot(rt, bt, BNT, 1), 0.0)
        p_k = jnp.where(incl, _bdot(rt, kt, BNT, 1), 0.0)
        d1 = jnp.where(blk8, a_ab, 0.0)
        d2 = _bdot(d1, d1, BNN, 3)
        d4 = _bdot(d2, d2, BNN, 3)
        t = jnp.where(eye, 1.0, d1)
        t = t + _bdot(t, d2, BNN, 3)
        t = t + _bdot(t, d4, BNN, 3)
        t = t + _bdot(_bdot(t, jnp.where(blk16 & ~blk8, a_ab, 0.0), BNN, 1), t, BNN, 1)
        t = t + _bdot(_bdot(t, jnp.where(blk32 & ~blk16, a_ab, 0.0), BNN, 1), t, BNN, 1)
        t = t + _bdot(_bdot(t, jnp.where(blk32, 0.0, a_ab), BNN, 1), t, BNN, 1)
        w = _bdot(t, at, BNN, 1)
        u_loc = _bdot(t, _bdot(a_ak, vv, BNN, 1), BNN, 1)

        def st(ref, val):
            ref[:, cs] = val.reshape(RWKV_H, RWKV_GROUP, CHUNK, HEAD_DIM)

        st(qt_ref, rt + _bdot(p_b, w, BNN, 1))
        st(ol_ref, _bdot(p_b, u_loc, BNN, 1) + _bdot(p_k, vv, BNN, 1))
        st(gm_ref, _bdot(w, bh, BTN, 1) + jnp.where(eye, gam, 0.0))
        st(hm_ref, _bdot(u_loc, bh, BTN, 1) + _bdot(vv, kh, BTN, 1))
        return carry

    lax.fori_loop(0, nc // RWKV_GROUP, group_body, 0)

    def chunk_body(ci, s):
        o = _bdot(qt_ref[:, ci], s, BNT, 3) + ol_ref[:, ci]
        rows = pl.ds(pl.multiple_of(ci * CHUNK, CHUNK), CHUNK)
        for h in range(RWKV_H):
            os_ref[rows, h * HEAD_DIM:(h + 1) * HEAD_DIM] = o[h]
        return _bdot(s, gm_ref[:, ci], BNN, 3) + hm_ref[:, ci]

    state_ref[...] = lax.fori_loop(0, nc, chunk_body, state_ref[...])

    o = os_ref[...]
    mean = _dot_exact_rhs(o, head_ones) * (1.0 / HEAD_DIM)
    oc = o - mean
    var = _dot_exact_rhs(oc * oc, head_ones) * (1.0 / HEAD_DIM)
    on = oc * lax.rsqrt(var + RWKV_GN_EPS) * lng_ref[...] + lnb_ref[...]
    o_ref[...] = (on + bonus) * g


def _rwkv_mixer(y, seq_len, mu, w0, w2, a0, a2, g2, k_k, k_a, r_k, ln_g, ln_b, v_first, vres):
    m = y.shape[0]
    tile = 512
    nt = seq_len // tile
    has_vres = vres is not None
    row = lambda wd: pl.BlockSpec((tile, wd), lambda b, j: (b * nt + j, 0))
    full = lambda arr: pl.BlockSpec(arr.shape, lambda b, j: (0, 0))
    vec = lambda t: t.reshape(1, -1)
    pad_rows = lambda w, start: jnp.zeros((LANES, RWKV_W), F32).at[start:start + w.shape[0]].set(w)
    consts = [vec(mu), vec(w0), pad_rows(w2, 0), vec(a0), pad_rows(a2, RWKV_DECAY_RANK),
              pad_rows(g2, RWKV_DECAY_RANK + RWKV_AAA_RANK), vec(k_k), vec(k_a), vec(r_k), vec(ln_g), vec(ln_b),
              *_chunk_selectors(), _head_ones(RWKV_W)]
    args = [y] + ([v_first] if has_vres else []) + consts
    in_specs = [row(RWKV_COLS)] + ([row(RWKV_W)] if has_vres else []) + [full(c) for c in consts]
    out_shape = [jax.ShapeDtypeStruct((m, RWKV_W), F32)]
    out_specs = [row(RWKV_W)]
    if has_vres:
        v0, v1, v2 = vres
        extra = [vec(v0), jnp.pad(v1, ((0, 0), (0, LANES - v1.shape[1]))),
                 jnp.pad(v2, ((0, LANES - v2.shape[0]), (0, 0)))]
        args += extra
        in_specs += [full(c) for c in extra]
    else:
        out_shape.append(jax.ShapeDtypeStruct((m, RWKV_W), F32))
        out_specs.append(row(RWKV_W))
    unit_buf = pltpu.VMEM((RWKV_H, tile // CHUNK, CHUNK, HEAD_DIM), F32)
    outs = pl.pallas_call(
        functools.partial(_rwkv_kernel, has_vres=has_vres, tile=tile),
        grid=(m // seq_len, nt),
        in_specs=in_specs, out_specs=out_specs, out_shape=out_shape,
        scratch_shapes=[pltpu.VMEM((RWKV_H, HEAD_DIM, HEAD_DIM), F32), pltpu.VMEM((1, RWKV_COLS), F32)]
                       + [unit_buf] * 12 + [pltpu.VMEM((tile, RWKV_W), F32)],
        compiler_params=_cparams("parallel", "arbitrary"),
        name="rwkv7",
    )(*args)
    return (outs[0], v_first) if has_vres else (outs[0], outs[1])


ATT_PAD = ATT_BLOCK * max(DILATIONS)
ATT_UNROLL = 8


def _att_kernel(q_ref, k_ref, v_ref, o_ref, qs_ref, ks_ref, vs_ref, m_ref, n_ref, d_ref, *, seq_len):
    qs_ref[...] = q_ref[...].astype(F32)
    ks_ref[0:ATT_PAD, :] = jnp.zeros((ATT_PAD, LANES), F32)
    vs_ref[0:ATT_PAD, :] = jnp.zeros((ATT_PAD, LANES), F32)
    ks_ref[ATT_PAD:, :] = k_ref[...].astype(F32)
    vs_ref[ATT_PAD:, :] = v_ref[...].astype(F32)

    qi = lax.broadcasted_iota(jnp.int32, (ATT_BLOCK, 2 * ATT_BLOCK), 0)
    kc = lax.broadcasted_iota(jnp.int32, (ATT_BLOCK, 2 * ATT_BLOCK), 1)
    band = (kc >= qi) & (kc <= qi + ATT_BLOCK)
    head0 = lax.broadcasted_iota(jnp.int32, (ATT_BLOCK, LANES), 1) < HEAD_DIM

    for g, dil in enumerate(DILATIONS):
        nb = seq_len // dil // ATT_BLOCK

        def body(it, carry, g=g, dil=dil, nb=nb):
            r = it // nb
            jq = it % nb
            start = r + jq * (ATT_BLOCK * dil)
            rows = pl.ds(start, ATT_BLOCK, stride=dil)
            krows = pl.ds(start + (ATT_PAD - ATT_BLOCK * dil), 2 * ATT_BLOCK, stride=dil)
            q = qs_ref[rows, :].astype(BF16)
            k = ks_ref[krows, :].astype(BF16)
            v = vs_ref[krows, :].astype(BF16)
            valid = band & ((kc >= ATT_BLOCK) | (jq > 0))
            mxs, pvs, dens = [], [], []
            for hh in range(2):
                qh = jnp.where(head0 if hh == 0 else jnp.logical_not(head0), q, jnp.zeros_like(q))
                s = jnp.where(valid, lax.dot_general(qh, k, NT, preferred_element_type=F32), NEG_INF)
                mx = jnp.max(s, axis=-1, keepdims=True)
                p = jnp.exp(s - mx)
                mxs.append(mx)
                dens.append(jnp.sum(p, axis=-1, keepdims=True))
                pvs.append(jnp.dot(p.astype(BF16), v, preferred_element_type=F32))
            mx = jnp.where(head0, mxs[0], mxs[1])
            den = jnp.where(head0, dens[0], dens[1])
            pv = jnp.where(head0, pvs[0], pvs[1])
            if g > 0:
                m_old = m_ref[rows, :]
                m_new = jnp.maximum(m_old, mx)
                a_old = jnp.exp(m_old - m_new)
                a_new = jnp.exp(mx - m_new)
                pv = a_old * n_ref[rows, :] + a_new * pv
                den = a_old * d_ref[rows, :] + a_new * den
                mx = m_new
            if g == len(DILATIONS) - 1:
                o_ref[rows, :] = pv / den
            else:
                m_ref[rows, :] = mx
                n_ref[rows, :] = pv
                d_ref[rows, :] = den
            return carry

        lax.fori_loop(0, dil * nb, body, 0, unroll=ATT_UNROLL)


def _att_mixer(y_att, seq_len):
    m = y_att.shape[0]
    npair = ATT_W // LANES
    col = lambda c: pl.BlockSpec((seq_len, LANES), lambda b, hp: (b, c * npair + hp))
    seq_buf = pltpu.VMEM((seq_len, LANES), F32)
    pad_buf = pltpu.VMEM((ATT_PAD + seq_len, LANES), F32)
    return pl.pallas_call(
        functools.partial(_att_kernel, seq_len=seq_len),
        grid=(m // seq_len, npair),
        in_specs=[col(0), col(1), col(2)],
        out_specs=pl.BlockSpec((seq_len, LANES), lambda b, hp: (b, hp)),
        out_shape=jax.ShapeDtypeStruct((m, ATT_W), F32),
        scratch_shapes=[seq_buf, pad_buf, pad_buf, seq_buf, seq_buf, seq_buf],
        compiler_params=_cparams("parallel", "parallel"),
        name="dilated_att",
    )(y_att, y_att, y_att)


def _gla_kernel(y_ref, gu_ref, gb_ref, ng_ref, psel_ref, tsel_ref, hones_ref, o_ref, state_ref, qi_ref, ki_ref, ke_ref, gam_ref, os_ref, *, tile):
    @pl.when(pl.program_id(1) == 0)
    def _():
        state_ref[...] = jnp.zeros_like(state_ref)

    q = y_ref[:, 0:GLA_W] * HEAD_DIM ** -0.5
    k = y_ref[:, GLA_W:2 * GLA_W]
    og = y_ref[:, 3 * GLA_W:4 * GLA_W]
    lg = _log_sigmoid(_dot3(y_ref[:, 4 * GLA_W:], gu_ref[...]) + gb_ref[...]) * (1.0 / GLA_GATE_TAU)
    b = _chunk_sums(psel_ref[...], lg)
    b_last = _chunk_sums(tsel_ref[...], lg)
    qi_ref[...] = (q * jnp.exp(b)).astype(BF16)
    ki_ref[...] = (k * jnp.exp(-b)).astype(BF16)
    ke_ref[...] = (k * jnp.exp(b_last - b)).astype(BF16)
    gam_ref[...] = jnp.exp(b_last)

    ci = lax.broadcasted_iota(jnp.int32, (CHUNK, CHUNK), 0)
    cj = lax.broadcasted_iota(jnp.int32, (CHUNK, CHUNK), 1)
    causal = cj <= ci

    def chunk_body(c, carry):
        rows = pl.ds(pl.multiple_of(c * CHUNK, CHUNK), CHUNK)
        for h in range(GLA_H):
            sl = slice(h * HEAD_DIM, (h + 1) * HEAD_DIM)
            q_in, k_in, k_end = qi_ref[rows, sl], ki_ref[rows, sl], ke_ref[rows, sl]
            v = y_ref[rows, 2 * GLA_W + h * HEAD_DIM:2 * GLA_W + (h + 1) * HEAD_DIM].astype(BF16)
            gam = gam_ref[rows, sl][0:1, :]
            st = state_ref[h]
            att = jnp.where(causal, lax.dot_general(q_in, k_in, NT, preferred_element_type=F32), 0.0)
            os_ref[rows, sl] = (jnp.dot(att.astype(BF16), v, preferred_element_type=F32)
                                + lax.dot_general(q_in, st.astype(BF16), NT, preferred_element_type=F32))
            state_ref[h] = st * gam + lax.dot_general(v, k_end, TN, preferred_element_type=F32)
        return carry

    lax.fori_loop(0, tile // CHUNK, chunk_body, 0)

    o = os_ref[...]
    ms = _dot_exact_rhs(o * o, hones_ref[...]) * (1.0 / HEAD_DIM)
    o_ref[...] = o * lax.rsqrt(ms + RMS_EPS) * ng_ref[...] * (og * _sigmoid(og))


def _gla_mixer(y, seq_len, gate_up, gate_b, norm_g):
    m = y.shape[0]
    tile = 512
    nt = seq_len // tile
    gu = jnp.pad(gate_up, ((0, LANES - gate_up.shape[0]), (0, 0)))
    gb = gate_b.reshape(1, GLA_W)
    ng = jnp.tile(norm_g, GLA_H).reshape(1, GLA_W)
    consts = [gu, gb, ng, *_chunk_selectors(), _head_ones(GLA_W)]
    full = lambda a: pl.BlockSpec(a.shape, lambda b, j: (0, 0))
    return pl.pallas_call(
        functools.partial(_gla_kernel, tile=tile),
        grid=(m // seq_len, nt),
        in_specs=[pl.BlockSpec((tile, GLA_COLS_PAD), lambda b, j: (b * nt + j, 0))] + [full(c) for c in consts],
        out_specs=pl.BlockSpec((tile, GLA_W), lambda b, j: (b * nt + j, 0)),
        out_shape=jax.ShapeDtypeStruct((m, GLA_W), F32),
        scratch_shapes=[pltpu.VMEM((GLA_H, HEAD_DIM, HEAD_DIM), F32)] + [pltpu.VMEM((tile, GLA_W), BF16)] * 3
                       + [pltpu.VMEM((tile, GLA_W), F32)] * 2,
        compiler_params=_cparams("parallel", "arbitrary"),
        name="gla",
    )(y, *consts)


def _outproj_kernel(or_ref, oa_ref, og_ref, w_ref, h_ref, o_ref):
    acc = jnp.dot(or_ref[...].astype(BF16), w_ref[0:RWKV_W, :], preferred_element_type=F32)
    acc += jnp.dot(oa_ref[...].astype(BF16), w_ref[RWKV_W:RWKV_W + ATT_W, :], preferred_element_type=F32)
    acc += jnp.dot(og_ref[...].astype(BF16), w_ref[RWKV_W + ATT_W:, :], preferred_element_type=F32)
    o_ref[...] = h_ref[...] + acc


def _outproj(o_rwkv, o_att, o_gla, w, h):
    m = h.shape[0]
    tm = 512
    row = lambda wd: pl.BlockSpec((tm, wd), lambda i: (i, 0))
    return pl.pallas_call(
        _outproj_kernel,
        grid=(m // tm,),
        in_specs=[row(RWKV_W), row(ATT_W), row(GLA_W), pl.BlockSpec(w.shape, lambda i: (0, 0)), row(D_MODEL)],
        out_specs=row(D_MODEL),
        out_shape=jax.ShapeDtypeStruct(h.shape, F32),
        compiler_params=_cparams("parallel"),
        name="outproj",
    )(o_rwkv, o_att, o_gla, w, h)


def _ffn_kernel(x_ref, g_ref, wg_ref, wu_ref, wd_ref, o_ref, xn_ref, acc_ref):
    f = pl.program_id(1)

    @pl.when(f == 0)
    def _():
        xn_ref[...] = _rms(x_ref[...], g_ref[...]).astype(xn_ref.dtype)
        acc_ref[...] = jnp.zeros_like(acc_ref)

    xn = xn_ref[...]
    gate = jnp.dot(xn, wg_ref[...], preferred_element_type=F32)
    up = jnp.dot(xn, wu_ref[...], preferred_element_type=F32)
    act = (gate * _sigmoid(gate) * up).astype(BF16)
    acc_ref[...] += jnp.dot(act, wd_ref[...], preferred_element_type=F32)

    @pl.when(f == pl.num_programs(1) - 1)
    def _():
        o_ref[...] = x_ref[...] + acc_ref[...]


def _ffn(h, g, wg, wu, wd):
    m = h.shape[0]
    tm, tf = 512, FFN_DENSE // 2
    row = pl.BlockSpec((tm, D_MODEL), lambda i, f: (i, 0))
    return pl.pallas_call(
        _ffn_kernel,
        grid=(m // tm, FFN_DENSE // tf),
        in_specs=[row, pl.BlockSpec(g.shape, lambda i, f: (0, 0)),
                  pl.BlockSpec((D_MODEL, tf), lambda i, f: (0, f)),
                  pl.BlockSpec((D_MODEL, tf), lambda i, f: (0, f)),
                  pl.BlockSpec((tf, D_MODEL), lambda i, f: (f, 0))],
        out_specs=row,
        out_shape=jax.ShapeDtypeStruct(h.shape, F32),
        scratch_shapes=[pltpu.VMEM((tm, D_MODEL), BF16), pltpu.VMEM((tm, D_MODEL), F32)],
        compiler_params=_cparams("parallel", "arbitrary"),
        name="ffn_dense",
    )(h, g, wg, wu, wd)


MOE_BLOCK = 896
MOE_SUB = 256
MOE_FT = 896


def _router_kernel(x_ref, g_ref, wr_ref, tri_ref, xnt_ref, rank_ref, gate_ref, cnt_ref, *, n_tokens):
    row = lax.broadcasted_iota(jnp.int32, (MOE_BLOCK, 1), 0) + pl.program_id(0) * MOE_BLOCK
    real = row < n_tokens
    xn = jnp.where(real, _rms(x_ref[...], g_ref[...]), 0.0)
    xnt_ref[...] = xn.T.astype(BF16)
    logits = jnp.dot(xn, wr_ref[...], preferred_element_type=F32, precision=lax.Precision.HIGHEST)
    lane = lax.broadcasted_iota(jnp.int32, logits.shape, 1)
    lg = jnp.where(lane < N_EXPERTS, logits, NEG_INF)
    m1 = jnp.max(lg, axis=-1, keepdims=True)
    i1 = jnp.min(jnp.where(lg == m1, lane, LANES), axis=-1, keepdims=True)
    lg2 = jnp.where(lane == i1, NEG_INF, lg)
    m2 = jnp.max(lg2, axis=-1, keepdims=True)
    i2 = jnp.min(jnp.where(lg2 == m2, lane, LANES), axis=-1, keepdims=True)
    e2 = jnp.exp(m2 - m1)
    hot1 = (lane == i1) & real
    hot2 = (lane == i2) & real
    onehot = (hot1 | hot2).astype(F32)
    rank = jnp.dot(tri_ref[...], onehot.astype(BF16), preferred_element_type=F32)
    rank = jnp.where(hot1 | hot2, rank, -1.0)
    gate = jnp.where(hot1, 1.0 / (1.0 + e2), jnp.where(hot2, e2 / (1.0 + e2), 0.0))
    rank_ref[0] = rank.T[:N_EXPERTS]
    gate_ref[0] = gate.T[:N_EXPERTS]
    cnt_ref[0] = jnp.broadcast_to(jnp.sum(onehot, axis=0, keepdims=True), (8, LANES)).astype(jnp.int32)


def _route(h, g, w_router):
    m = h.shape[0]
    sb = MOE_BLOCK
    nblk = pl.cdiv(m, sb)
    t = jnp.arange(sb)
    tri = (t[None, :] < t[:, None]).astype(BF16)
    full = lambda a: pl.BlockSpec(a.shape, lambda i: (0,) * a.ndim)
    per_expert = pl.BlockSpec((1, N_EXPERTS, sb), lambda i: (i, 0, 0))
    return pl.pallas_call(
        functools.partial(_router_kernel, n_tokens=m),
        grid=(nblk,),
        in_specs=[pl.BlockSpec((sb, D_MODEL), lambda i: (i, 0)), full(g), full(w_router), full(tri)],
        out_specs=[pl.BlockSpec((D_MODEL, sb), lambda i: (0, i)), per_expert, per_expert,
                   pl.BlockSpec((1, 8, LANES), lambda i: (i, 0, 0))],
        out_shape=[jax.ShapeDtypeStruct((D_MODEL, nblk * sb), BF16), jax.ShapeDtypeStruct((nblk, N_EXPERTS, sb), F32),
                   jax.ShapeDtypeStruct((nblk, N_EXPERTS, sb), F32),
                   jax.ShapeDtypeStruct((nblk, 8, LANES), jnp.int32)],
        compiler_params=_cparams("parallel"),
        name="moe_router",
    )(h, g, w_router, tri)


def _moe_kernel(cnt_ref, x_ref, xnt_ref, rank_ref, gate_ref, wg_ref, wu_ref, wd_ref, fg_ref, o_ref,
                xs_ref, y_ref, acc_ref, *, final_norm):
    i, e, f = pl.program_id(0), pl.program_id(1), pl.program_id(2)
    last_f = f == pl.num_programs(2) - 1
    n_sub = (cnt_ref[i * N_EXPERTS + e] + (MOE_SUB - 1)) // MOE_SUB

    @pl.when((e == 0) & (f == 0))
    def _():
        acc_ref[...] = jnp.zeros_like(acc_ref)

    def selection(j):
        want = lax.broadcasted_iota(jnp.int32, (MOE_SUB, MOE_BLOCK), 0) + j * MOE_SUB
        return (rank_ref[0, pl.ds(e, 1), :] == want.astype(F32)).astype(BF16)

    def sub_body(j, carry):
        @pl.when(f == 0)
        def _():
            xs_ref[j] = lax.dot_general(xnt_ref[...], selection(j), NT, preferred_element_type=F32).astype(BF16)
            y_ref[j] = jnp.zeros((D_MODEL, MOE_SUB), F32)

        xs = xs_ref[j]
        gate = jnp.dot(wg_ref[0], xs, preferred_element_type=F32)
        up = jnp.dot(wu_ref[0], xs, preferred_element_type=F32)
        act = (gate * _sigmoid(gate) * up).astype(BF16)
        y_ref[j] += jnp.dot(wd_ref[0], act, preferred_element_type=F32)

        @pl.when(last_f)
        def _():
            back = jnp.dot(y_ref[j].astype(BF16), selection(j), preferred_element_type=F32)
            acc_ref[...] += gate_ref[0, pl.ds(e, 1), :] * back

        return carry

    lax.fori_loop(0, n_sub, sub_body, 0)

    @pl.when((e == N_EXPERTS - 1) & last_f)
    def _():
        y = x_ref[...] + acc_ref[...].T
        o_ref[...] = _rms(y, fg_ref[...]) if final_norm else y


def _moe(h, g, w_router, wg_t, wu_t, wd_t, final_g, final_norm):
    m = h.shape[0]
    sb, ft = MOE_BLOCK, MOE_FT
    xnt, rank, gate, cnt = _route(h, g, w_router)
    counts = cnt[:, 0, :N_EXPERTS].reshape(-1)
    row = pl.BlockSpec((sb, D_MODEL), lambda i, e, f, c: (i, 0))
    per_expert = pl.BlockSpec((1, N_EXPERTS, sb), lambda i, e, f, c: (i, 0, 0))
    grid_spec = pltpu.PrefetchScalarGridSpec(
        num_scalar_prefetch=1,
        grid=(pl.cdiv(m, sb), N_EXPERTS, FFN_EXPERT // ft),
        in_specs=[row, pl.BlockSpec((D_MODEL, sb), lambda i, e, f, c: (0, i)), per_expert, per_expert,
                  pl.BlockSpec((1, ft, D_MODEL), lambda i, e, f, c: (e, f, 0)),
                  pl.BlockSpec((1, ft, D_MODEL), lambda i, e, f, c: (e, f, 0)),
                  pl.BlockSpec((1, D_MODEL, ft), lambda i, e, f, c: (e, 0, f)),
                  pl.BlockSpec((1, D_MODEL), lambda i, e, f, c: (0, 0))],
        out_specs=row,
        scratch_shapes=[pltpu.VMEM((pl.cdiv(sb, MOE_SUB), D_MODEL, MOE_SUB), BF16),
                        pltpu.VMEM((pl.cdiv(sb, MOE_SUB), D_MODEL, MOE_SUB), F32),
                        pltpu.VMEM((D_MODEL, sb), F32)],
    )
    return pl.pallas_call(
        functools.partial(_moe_kernel, final_norm=final_norm),
        grid_spec=grid_spec,
        out_shape=jax.ShapeDtypeStruct(h.shape, F32),
        compiler_params=_cparams("parallel", "arbitrary", "arbitrary"),
        name="moe",
    )(counts, h, xnt, rank, gate, wg_t, wu_t, wd_t, final_g)


def kernel(x, positions, mix_norm_g, w_in, rwkv_mu, rwkv_w0, rwkv_w2, rwkv_a0, rwkv_a2, rwkv_g2, rwkv_k_k, rwkv_k_a, rwkv_r_k, rwkv_ln_g, rwkv_ln_b, rwkv_v0, rwkv_v1, rwkv_v2, gla_gate_up, gla_gate_b, gla_norm_g, w_out, ffn_norm_g, ffn_w_gate, ffn_w_up, ffn_w_down, moe_router, moe_w_gate, moe_w_up, moe_w_down, final_norm_g):
    B, S, D = x.shape
    M = B * S
    h = x.reshape(M, D)
    cos, sa, sb = _rope_tables(positions)
    v_first = None
    for layer in range(DEPTH):
        w = w_in[layer]
        wr = w[:, :RWKV_COLS].astype(BF16)
        wa = w[:, RWKV_COLS:RWKV_COLS + ATT_COLS].astype(BF16)
        wgl = w[:, RWKV_COLS + ATT_COLS:]
        wgl = jnp.concatenate([wgl[:, :3 * GLA_W], wgl[:, 3 * GLA_W + GLA_GATE_RANK:],
                               wgl[:, 3 * GLA_W:3 * GLA_W + GLA_GATE_RANK],
                               jnp.zeros((D, LANES - GLA_GATE_RANK), w.dtype)], axis=1).astype(BF16)
        y_rwkv, y_att, y_gla = _inproj(h, mix_norm_g[layer].reshape(1, D), wr, wa, wgl, cos, sa, sb)
        vres = None if layer == 0 else (rwkv_v0[layer - 1], rwkv_v1[layer - 1], rwkv_v2[layer - 1])
        o_rwkv, v_first = _rwkv_mixer(y_rwkv, S, rwkv_mu[layer], rwkv_w0[layer], rwkv_w2[layer], rwkv_a0[layer],
                                      rwkv_a2[layer], rwkv_g2[layer], rwkv_k_k[layer], rwkv_k_a[layer],
                                      rwkv_r_k[layer], rwkv_ln_g[layer], rwkv_ln_b[layer], v_first, vres)
        o_att = _att_mixer(y_att, S)
        o_gla = _gla_mixer(y_gla, S, gla_gate_up[layer], gla_gate_b[layer], gla_norm_g[layer])
        h = _outproj(o_rwkv, o_att, o_gla, w_out[layer].astype(BF16), h)
        i = layer // 2
        g = ffn_norm_g[layer].reshape(1, D)
        if layer % 2 == 0:
            h = _ffn(h, g, ffn_w_gate[i].astype(BF16), ffn_w_up[i].astype(BF16), ffn_w_down[i].astype(BF16))
        else:
            wrt = jnp.pad(moe_router[i], ((0, 0), (0, LANES - N_EXPERTS)))
            to_t = lambda w: w.transpose(0, 2, 1).astype(BF16)
            h = _moe(h, g, wrt, to_t(moe_w_gate[i]), to_t(moe_w_up[i]), to_t(moe_w_down[i]),
                     final_norm_g.reshape(1, D), layer == DEPTH - 1)
    return h.reshape(B, S, D)
```

```python
import functools

import jax
import jax.numpy as jnp
from jax import lax
from jax.experimental import pallas as pl
from jax.experimental.pallas import tpu as pltpu

D_MODEL = 1024
DEPTH = 2
HEAD_DIM = 64
RWKV_W = 256
RWKV_H = 4
ATT_W = 384
GLA_W = 384
GLA_H = 6
RWKV_DECAY_RANK = 32
RWKV_AAA_RANK = 32
RWKV_GATE_RANK = 64
RWKV_GN_EPS = 64e-5
GLA_GATE_RANK = 16
GLA_GATE_TAU = 16.0
CHUNK = 64
DILATIONS = (1, 4, 16)
ATT_BLOCK = 128
ROPE_THETA = 500000.0
ROPE_DIMS = 16
ROPE_HALF = 8
FFN_DENSE = 2816
N_EXPERTS = 8
FFN_EXPERT = 3584
RMS_EPS = 1e-5
NEG_INF = -1e30

RWKV_COLS = 3 * RWKV_W + RWKV_DECAY_RANK + RWKV_AAA_RANK + RWKV_GATE_RANK
ATT_COLS = 3 * ATT_W
LANES = 128
GLA_COLS_PAD = 4 * GLA_W + LANES

VMEM_LIMIT = 56 * 1024 * 1024

F32 = jnp.float32
BF16 = jnp.bfloat16
NN = (((1,), (0,)), ((), ()))
NT = (((1,), (1,)), ((), ()))
TN = (((0,), (0,)), ((), ()))
BNN = (((2,), (1,)), ((0,), (0,)))
BNT = (((2,), (2,)), ((0,), (0,)))
BTN = (((1,), (1,)), ((0,), (0,)))


def _cparams(*sem):
    return pltpu.CompilerParams(dimension_semantics=sem, vmem_limit_bytes=VMEM_LIMIT)


def _rms(x, g):
    return x * lax.rsqrt(jnp.mean(x * x, axis=-1, keepdims=True) + RMS_EPS) * g


def _sigmoid(x):
    return 1.0 / (1.0 + jnp.exp(-x))


def _log_sigmoid(x):
    return jnp.minimum(x, 0.0) - jnp.log(1.0 + jnp.exp(-jnp.abs(x)))


def _split_hi_lo(a):
    hi = a.astype(BF16)
    return hi, (a - hi.astype(F32)).astype(BF16)


def _dot3(a, b, dims=NN):
    a_hi, a_lo = _split_hi_lo(a)
    b_hi, b_lo = _split_hi_lo(b)
    d = lambda x, y: lax.dot_general(x, y, dims, preferred_element_type=F32)
    return d(a_hi, b_hi) + d(a_lo, b_hi) + d(a_hi, b_lo)


def _dot_exact_rhs(x, r_bf16):
    x_hi, x_lo = _split_hi_lo(x)
    return jnp.dot(x_hi, r_bf16, preferred_element_type=F32) + jnp.dot(x_lo, r_bf16, preferred_element_type=F32)


def _bdot(a, b, dims, passes):
    d = lambda x, y: lax.dot_general(x, y, dims, preferred_element_type=F32)
    if passes == 1:
        return d(a.astype(BF16), b.astype(BF16))
    a_hi, a_lo = _split_hi_lo(a)
    b_hi, b_lo = _split_hi_lo(b)
    return d(a_hi, b_hi) + d(a_lo, b_hi) + d(a_hi, b_lo)


def _head_ones(width):
    idx = jnp.arange(width) // HEAD_DIM
    return (idx[:, None] == idx[None, :]).astype(BF16)


def _chunk_selectors():
    t = jnp.arange(CHUNK)
    return (t[None, :] <= t[:, None]).astype(BF16), jnp.ones((CHUNK, CHUNK), BF16)


def _chunk_sums(sel, x):
    nc = x.shape[0] // CHUNK
    hi, lo = _split_hi_lo(x.reshape(nc, CHUNK, x.shape[1]))
    s3 = jnp.broadcast_to(sel[None], (nc, CHUNK, CHUNK))
    d = lambda a, b: lax.dot_general(a, b, BNN, preferred_element_type=F32)
    return (d(s3, hi) + d(s3, lo)).reshape(x.shape)


def _rope_table_kernel(pos_ref, invf_ref, cos_ref, sa_ref, sb_ref):
    ang = pos_ref[...].astype(F32) * invf_ref[...]
    lane = lax.broadcasted_iota(jnp.int32, ang.shape, 1) % HEAD_DIM
    c = jnp.cos(ang)
    s = jnp.sin(ang)
    cos_ref[...] = jnp.where(lane < ROPE_DIMS, c, 1.0)
    sa_ref[...] = jnp.where((lane >= ROPE_HALF) & (lane < ROPE_DIMS), s, 0.0)
    sb_ref[...] = jnp.where(lane < ROPE_HALF, -s, 0.0)


def _rope_tables(positions):
    m = positions.size
    tm = 2048
    lane = jnp.arange(LANES) % ROPE_HALF
    invf = (ROPE_THETA ** (-lane.astype(F32) / ROPE_HALF)).reshape(1, LANES)
    out = jax.ShapeDtypeStruct((m, LANES), F32)
    row = pl.BlockSpec((tm, LANES), lambda i: (i, 0))
    return pl.pallas_call(
        _rope_table_kernel,
        grid=(m // tm,),
        in_specs=[pl.BlockSpec((tm, 1), lambda i: (i, 0)), pl.BlockSpec((1, LANES), lambda i: (0, 0))],
        out_specs=[row, row, row],
        out_shape=[out, out, out],
        compiler_params=_cparams("parallel"),
        name="rope_tables",
    )(positions.reshape(m, 1), invf)


def _inproj_kernel(x_ref, g_ref, wr_ref, wa_ref, wg_ref, cos_ref, sa_ref, sb_ref, yr_ref, ya_ref, yg_ref):
    xb = _rms(x_ref[...], g_ref[...]).astype(BF16)
    yr_ref[...] = jnp.dot(xb, wr_ref[...], preferred_element_type=F32)
    yg_ref[...] = jnp.dot(xb, wg_ref[...], preferred_element_type=F32)
    ya = jnp.dot(xb, wa_ref[...], preferred_element_type=F32)
    cos, sa, sb = cos_ref[...], sa_ref[...], sb_ref[...]
    for j in range(2 * ATT_W // LANES):
        blk = ya[:, j * LANES:(j + 1) * LANES]
        rot = blk * cos + pltpu.roll(blk, ROPE_HALF, 1) * sa + pltpu.roll(blk, LANES - ROPE_HALF, 1) * sb
        if j < ATT_W // LANES:
            rot = rot * HEAD_DIM ** -0.5
        ya_ref[:, j * LANES:(j + 1) * LANES] = rot.astype(ya_ref.dtype)
    ya_ref[:, 2 * ATT_W:] = ya[:, 2 * ATT_W:].astype(ya_ref.dtype)


def _inproj(h, g, wr, wa, wg, cos, sa, sb):
    m = h.shape[0]
    tm = 512
    row = lambda w: pl.BlockSpec((tm, w), lambda i: (i, 0))
    full = lambda a: pl.BlockSpec(a.shape, lambda i: (0, 0))
    return pl.pallas_call(
        _inproj_kernel,
        grid=(m // tm,),
        in_specs=[row(D_MODEL), full(g), full(wr), full(wa), full(wg), row(LANES), row(LANES), row(LANES)],
        out_specs=[row(RWKV_COLS), row(ATT_COLS), row(GLA_COLS_PAD)],
        out_shape=[jax.ShapeDtypeStruct((m, RWKV_COLS), F32),
                   jax.ShapeDtypeStruct((m, ATT_COLS), BF16),
                   jax.ShapeDtypeStruct((m, GLA_COLS_PAD), F32)],
        compiler_params=_cparams("parallel"),
        name="inproj",
    )(h, g, wr, wa, wg, cos, sa, sb)


RWKV_GROUP = 4


def _rwkv_kernel(*refs, has_vres, tile):
    nc = tile // CHUNK
    if has_vres:
        (y_ref, vf_ref, mu_ref, w0_ref, w2_ref, a0_ref, a2_ref, g2_ref, kk_ref, ka_ref, rk_ref, lng_ref, lnb_ref,
         psel_ref, tsel_ref, hones_ref, v0_ref, v1_ref, v2_ref, o_ref, *scratch) = refs
    else:
        (y_ref, mu_ref, w0_ref, w2_ref, a0_ref, a2_ref, g2_ref, kk_ref, ka_ref, rk_ref, lng_ref, lnb_ref,
         psel_ref, tsel_ref, hones_ref, o_ref, vfo_ref, *scratch) = refs
    (state_ref, prev_ref, at_ref, rt_ref, bt_ref, kt_ref, bh_ref, kh_ref, vv_ref, gam_ref,
     qt_ref, ol_ref, gm_ref, hm_ref, sb_ref, os_ref) = scratch

    @pl.when(pl.program_id(1) == 0)
    def _():
        state_ref[...] = jnp.zeros_like(state_ref)
        prev_ref[...] = jnp.zeros_like(prev_ref)

    y = y_ref[...]
    row = lax.broadcasted_iota(jnp.int32, (tile, 1), 0)
    ysh = jnp.where(row == 0, prev_ref[...], pltpu.roll(y, 1, 0))
    prev_ref[...] = y[tile - 1:tile, :]
    ym = y + (ysh - y) * mu_ref[...]
    r = ym[:, 0:RWKV_W]
    k = ym[:, RWKV_W:2 * RWKV_W]
    v = ym[:, 2 * RWKV_W:3 * RWKV_W]
    x6 = ym[:, 3 * RWKV_W:]
    wl = w0_ref[...] + _dot3(jnp.tanh(x6), w2_ref[...])
    lw = -jnp.exp(_log_sigmoid(wl) - 0.5)
    a = _sigmoid(a0_ref[...] + _dot3(x6, a2_ref[...]))
    g = _dot3(_sigmoid(x6), g2_ref[...])
    if has_vres:
        v = v + (vf_ref[...] - v) * _sigmoid(v0_ref[...] + _dot3(_dot3(v, v1_ref[...]), v2_ref[...]))
    else:
        vfo_ref[...] = v

    head_ones = hones_ref[...]
    kk = k * kk_ref[...]
    kk = kk * lax.rsqrt(jnp.maximum(_dot_exact_rhs(kk * kk, head_ones), 1e-24))
    k2 = k * (1.0 + (a - 1.0) * ka_ref[...])
    bonus = _dot_exact_rhs(r * k2 * rk_ref[...], head_ones) * v

    c = _chunk_sums(psel_ref[...], lw)
    c_last = _chunk_sums(tsel_ref[...], lw)
    b = kk * a
    e_neg = jnp.exp(-c)
    e_end = jnp.exp(c_last - c)

    def put(ref, val):
        for h in range(RWKV_H):
            ref[h] = val[:, h * HEAD_DIM:(h + 1) * HEAD_DIM].reshape(nc, CHUNK, HEAD_DIM)

    put(rt_ref, r * jnp.exp(c))
    put(at_ref, -kk * jnp.exp(c - lw))
    put(bt_ref, b * e_neg)
    put(kt_ref, k2 * e_neg)
    put(bh_ref, b * e_end)
    put(kh_ref, k2 * e_end)
    put(vv_ref, v)
    put(gam_ref, jnp.exp(c_last))

    n = RWKV_H * RWKV_GROUP
    qi = lax.broadcasted_iota(jnp.int32, (n, CHUNK, CHUNK), 1)
    qj = lax.broadcasted_iota(jnp.int32, (n, CHUNK, CHUNK), 2)
    strict = qj < qi
    incl = qj <= qi
    eye = qj == qi
    blk8 = qi // 8 == qj // 8
    blk16 = qi // 16 == qj // 16
    blk32 = qi // 32 == qj // 32

    def group_body(gi, carry):
        cs = pl.ds(gi * RWKV_GROUP, RWKV_GROUP)
        ld = lambda ref: ref[:, cs].reshape(n, CHUNK, HEAD_DIM)
        at, rt, bt, kt, bh, kh, vv = (ld(x) for x in (at_ref, rt_ref, bt_ref, kt_ref, bh_ref, kh_ref, vv_ref))
        gam = ld(gam_ref)[:, 0:1, :]
        a_ab = jnp.where(strict, _bdot(at, bt, BNT, 1), 0.0)
        a_ak = jnp.where(strict, _bdot(at, kt, BNT, 1), 0.0)
        p_b = jnp.where(incl, _bdot(rt, bt, BNT, 1), 0.0)
        p_k = jnp.where(incl, _bdot(rt, kt, BNT, 1), 0.0)
        d1 = jnp.where(blk8, a_ab, 0.0)
        d2 = _bdot(d1, d1, BNN, 3)
        d4 = _bdot(d2, d2, BNN, 3)
        t = jnp.where(eye, 1.0, d1)
        t = t + _bdot(t, d2, BNN, 3)
        t = t + _bdot(t, d4, BNN, 3)
        t = t + _bdot(_bdot(t, jnp.where(blk16 & ~blk8, a_ab, 0.0), BNN, 1), t, BNN, 1)
        t = t + _bdot(_bdot(t, jnp.where(blk32 & ~blk16, a_ab, 0.0), BNN, 1), t, BNN, 1)
        t = t + _bdot(_bdot(t, jnp.where(blk32, 0.0, a_ab), BNN, 1), t, BNN, 1)
        w = _bdot(t, at, BNN, 1)
        u_loc = _bdot(t, _bdot(a_ak, vv, BNN, 1), BNN, 1)

        def st(ref, val):
            ref[:, cs] = val.reshape(RWKV_H, RWKV_GROUP, CHUNK, HEAD_DIM)

        st(qt_ref, rt + _bdot(p_b, w, BNN, 1))
        st(ol_ref, _bdot(p_b, u_loc, BNN, 1) + _bdot(p_k, vv, BNN, 1))
        st(gm_ref, _bdot(w, bh, BTN, 1) + jnp.where(eye, gam, 0.0))
        st(hm_ref, _bdot(u_loc, bh, BTN, 1) + _bdot(vv, kh, BTN, 1))
        return carry

    lax.fori_loop(0, nc // RWKV_GROUP, group_body, 0)

    def chunk_body(ci, s):
        sb_ref[:, ci] = s
        return _bdot(s, gm_ref[:, ci], BNN, 3) + hm_ref[:, ci]

    state_ref[...] = lax.fori_loop(0, nc, chunk_body, state_ref[...], unroll=True)

    to_units = lambda ref: ref[...].reshape(RWKV_H * nc, CHUNK, HEAD_DIM)
    o = _bdot(to_units(qt_ref), to_units(sb_ref), BNT, 3) + to_units(ol_ref)
    for h in range(RWKV_H):
        os_ref[:, h * HEAD_DIM:(h + 1) * HEAD_DIM] = o[h * nc:(h + 1) * nc].reshape(tile, HEAD_DIM)

    o = os_ref[...]
    mean = _dot_exact_rhs(o, head_ones) * (1.0 / HEAD_DIM)
    oc = o - mean
    var = _dot_exact_rhs(oc * oc, head_ones) * (1.0 / HEAD_DIM)
    on = oc * lax.rsqrt(var + RWKV_GN_EPS) * lng_ref[...] + lnb_ref[...]
    o_ref[...] = (on + bonus) * g


def _rwkv_mixer(y, seq_len, mu, w0, w2, a0, a2, g2, k_k, k_a, r_k, ln_g, ln_b, v_first, vres):
    m = y.shape[0]
    tile = 512
    nt = seq_len // tile
    has_vres = vres is not None
    row = lambda wd: pl.BlockSpec((tile, wd), lambda b, j: (b * nt + j, 0))
    full = lambda arr: pl.BlockSpec(arr.shape, lambda b, j: (0, 0))
    vec = lambda t: t.reshape(1, -1)
    pad_rows = lambda w, start: jnp.zeros((LANES, RWKV_W), F32).at[start:start + w.shape[0]].set(w)
    consts = [vec(mu), vec(w0), pad_rows(w2, 0), vec(a0), pad_rows(a2, RWKV_DECAY_RANK),
              pad_rows(g2, RWKV_DECAY_RANK + RWKV_AAA_RANK), vec(k_k), vec(k_a), vec(r_k), vec(ln_g), vec(ln_b),
              *_chunk_selectors(), _head_ones(RWKV_W)]
    args = [y] + ([v_first] if has_vres else []) + consts
    in_specs = [row(RWKV_COLS)] + ([row(RWKV_W)] if has_vres else []) + [full(c) for c in consts]
    out_shape = [jax.ShapeDtypeStruct((m, RWKV_W), F32)]
    out_specs = [row(RWKV_W)]
    if has_vres:
        v0, v1, v2 = vres
        extra = [vec(v0), jnp.pad(v1, ((0, 0), (0, LANES - v1.shape[1]))),
                 jnp.pad(v2, ((0, LANES - v2.shape[0]), (0, 0)))]
        args += extra
        in_specs += [full(c) for c in extra]
    else:
        out_shape.append(jax.ShapeDtypeStruct((m, RWKV_W), F32))
        out_specs.append(row(RWKV_W))
    unit_buf = pltpu.VMEM((RWKV_H, tile // CHUNK, CHUNK, HEAD_DIM), F32)
    outs = pl.pallas_call(
        functools.partial(_rwkv_kernel, has_vres=has_vres, tile=tile),
        grid=(m // seq_len, nt),
        in_specs=in_specs, out_specs=out_specs, out_shape=out_shape,
        scratch_shapes=[pltpu.VMEM((RWKV_H, HEAD_DIM, HEAD_DIM), F32), pltpu.VMEM((1, RWKV_COLS), F32)]
                       + [unit_buf] * 13 + [pltpu.VMEM((tile, RWKV_W), F32)],
        compiler_params=_cparams("parallel", "arbitrary"),
        name="rwkv7",
    )(*args)
    return (outs[0], v_first) if has_vres else (outs[0], outs[1])


ATT_PAD = ATT_BLOCK * max(DILATIONS)
ATT_UNROLL = 8


def _att_kernel(q_ref, k_ref, v_ref, o_ref, qs_ref, ks_ref, vs_ref, m_ref, n_ref, d_ref, *, seq_len):
    qs_ref[...] = q_ref[...].astype(F32)
    ks_ref[0:ATT_PAD, :] = jnp.zeros((ATT_PAD, LANES), F32)
    vs_ref[0:ATT_PAD, :] = jnp.zeros((ATT_PAD, LANES), F32)
    ks_ref[ATT_PAD:, :] = k_ref[...].astype(F32)
    vs_ref[ATT_PAD:, :] = v_ref[...].astype(F32)

    qi = lax.broadcasted_iota(jnp.int32, (ATT_BLOCK, 2 * ATT_BLOCK), 0)
    kc = lax.broadcasted_iota(jnp.int32, (ATT_BLOCK, 2 * ATT_BLOCK), 1)
    band = (kc >= qi) & (kc <= qi + ATT_BLOCK)
    head0 = lax.broadcasted_iota(jnp.int32, (ATT_BLOCK, LANES), 1) < HEAD_DIM

    for g, dil in enumerate(DILATIONS):
        nb = seq_len // dil // ATT_BLOCK

        def body(it, carry, g=g, dil=dil, nb=nb):
            r = it // nb
            jq = it % nb
            start = r + jq * (ATT_BLOCK * dil)
            rows = pl.ds(start, ATT_BLOCK, stride=dil)
            krows = pl.ds(start + (ATT_PAD - ATT_BLOCK * dil), 2 * ATT_BLOCK, stride=dil)
            q = qs_ref[rows, :].astype(BF16)
            k = ks_ref[krows, :].astype(BF16)
            v = vs_ref[krows, :].astype(BF16)
            valid = band & ((kc >= ATT_BLOCK) | (jq > 0))
            mxs, pvs, dens = [], [], []
            for hh in range(2):
                qh = jnp.where(head0 if hh == 0 else jnp.logical_not(head0), q, jnp.zeros_like(q))
                s = jnp.where(valid, lax.dot_general(qh, k, NT, preferred_element_type=F32), NEG_INF)
                mx = jnp.max(s, axis=-1, keepdims=True)
                p = jnp.exp(s - mx)
                mxs.append(mx)
                dens.append(jnp.sum(p, axis=-1, keepdims=True))
                pvs.append(jnp.dot(p.astype(BF16), v, preferred_element_type=F32))
            mx = jnp.where(head0, mxs[0], mxs[1])
            den = jnp.where(head0, dens[0], dens[1])
            pv = jnp.where(head0, pvs[0], pvs[1])
            if g > 0:
                m_old = m_ref[rows, :]
                m_new = jnp.maximum(m_old, mx)
                a_old = jnp.exp(m_old - m_new)
                a_new = jnp.exp(mx - m_new)
                pv = a_old * n_ref[rows, :] + a_new * pv
                den = a_old * d_ref[rows, :] + a_new * den
                mx = m_new
            if g == len(DILATIONS) - 1:
                o_ref[rows, :] = pv / den
            else:
                m_ref[rows, :] = mx
                n_ref[rows, :] = pv
                d_ref[rows, :] = den
            return carry

        lax.fori_loop(0, dil * nb, body, 0, unroll=ATT_UNROLL)


def _att_mixer(y_att, seq_len):
    m = y_att.shape[0]
    npair = ATT_W // LANES
    col = lambda c: pl.BlockSpec((seq_len, LANES), lambda b, hp: (b, c * npair + hp))
    seq_buf = pltpu.VMEM((seq_len, LANES), F32)
    pad_buf = pltpu.VMEM((ATT_PAD + seq_len, LANES), F32)
    return pl.pallas_call(
        functools.partial(_att_kernel, seq_len=seq_len),
        grid=(m // seq_len, npair),
        in_specs=[col(0), col(1), col(2)],
        out_specs=pl.BlockSpec((seq_len, LANES), lambda b, hp: (b, hp)),
        out_shape=jax.ShapeDtypeStruct((m, ATT_W), F32),
        scratch_shapes=[seq_buf, pad_buf, pad_buf, seq_buf, seq_buf, seq_buf],
        compiler_params=_cparams("parallel", "parallel"),
        name="dilated_att",
    )(y_att, y_att, y_att)


def _gla_kernel(y_ref, gu_ref, gb_ref, ng_ref, psel_ref, tsel_ref, hones_ref, o_ref, state_ref, os_ref, *, tile):
    nc = tile // CHUNK
    n = GLA_H * nc

    @pl.when(pl.program_id(1) == 0)
    def _():
        state_ref[...] = jnp.zeros_like(state_ref)

    def units(val):
        return jnp.concatenate([val[:, h * HEAD_DIM:(h + 1) * HEAD_DIM].reshape(nc, CHUNK, HEAD_DIM)
                                for h in range(GLA_H)], axis=0)

    q = y_ref[:, 0:GLA_W] * HEAD_DIM ** -0.5
    k = y_ref[:, GLA_W:2 * GLA_W]
    og = y_ref[:, 3 * GLA_W:4 * GLA_W]
    lg = _log_sigmoid(_dot3(y_ref[:, 4 * GLA_W:], gu_ref[...]) + gb_ref[...]) * (1.0 / GLA_GATE_TAU)
    b = _chunk_sums(psel_ref[...], lg)
    b_last = _chunk_sums(tsel_ref[...], lg)
    q_in = units((q * jnp.exp(b)).astype(BF16))
    k_in = units((k * jnp.exp(-b)).astype(BF16))
    k_end = units((k * jnp.exp(b_last - b)).astype(BF16))
    v = units(y_ref[:, 2 * GLA_W:3 * GLA_W].astype(BF16))
    gam = units(jnp.exp(b_last))[:, 0:1, :].reshape(GLA_H, nc, 1, HEAD_DIM)

    ci = lax.broadcasted_iota(jnp.int32, (n, CHUNK, CHUNK), 1)
    cj = lax.broadcasted_iota(jnp.int32, (n, CHUNK, CHUNK), 2)
    d = lambda a, bb, dims: lax.dot_general(a, bb, dims, preferred_element_type=F32)
    att = jnp.where(cj <= ci, d(q_in, k_in, BNT), 0.0)
    o = d(att.astype(BF16), v, BNN)
    kv = d(v, k_end, BTN).reshape(GLA_H, nc, HEAD_DIM, HEAD_DIM)

    s = state_ref[...]
    before = []
    for c in range(nc):
        before.append(s)
        s = s * gam[:, c] + kv[:, c]
    state_ref[...] = s
    s_before = jnp.stack(before, axis=1).reshape(n, HEAD_DIM, HEAD_DIM)
    o = o + d(q_in, s_before.astype(BF16), BNT)
    for h in range(GLA_H):
        os_ref[:, h * HEAD_DIM:(h + 1) * HEAD_DIM] = o[h * nc:(h + 1) * nc].reshape(tile, HEAD_DIM)

    o = os_ref[...]
    ms = _dot_exact_rhs(o * o, hones_ref[...]) * (1.0 / HEAD_DIM)
    o_ref[...] = o * lax.rsqrt(ms + RMS_EPS) * ng_ref[...] * (og * _sigmoid(og))


def _gla_mixer(y, seq_len, gate_up, gate_b, norm_g):
    m = y.shape[0]
    tile = 512
    nt = seq_len // tile
    gu = jnp.pad(gate_up, ((0, LANES - gate_up.shape[0]), (0, 0)))
    gb = gate_b.reshape(1, GLA_W)
    ng = jnp.tile(norm_g, GLA_H).reshape(1, GLA_W)
    consts = [gu, gb, ng, *_chunk_selectors(), _head_ones(GLA_W)]
    full = lambda a: pl.BlockSpec(a.shape, lambda b, j: (0, 0))
    return pl.pallas_call(
        functools.partial(_gla_kernel, tile=tile),
        grid=(m // seq_len, nt),
        in_specs=[pl.BlockSpec((tile, GLA_COLS_PAD), lambda b, j: (b * nt + j, 0))] + [full(c) for c in consts],
        out_specs=pl.BlockSpec((tile, GLA_W), lambda b, j: (b * nt + j, 0)),
        out_shape=jax.ShapeDtypeStruct((m, GLA_W), F32),
        scratch_shapes=[pltpu.VMEM((GLA_H, HEAD_DIM, HEAD_DIM), F32), pltpu.VMEM((tile, GLA_W), F32)],
        compiler_params=_cparams("parallel", "arbitrary"),
        name="gla",
    )(y, *consts)


def _outproj_kernel(or_ref, oa_ref, og_ref, w_ref, h_ref, o_ref):
    acc = jnp.dot(or_ref[...].astype(BF16), w_ref[0:RWKV_W, :], preferred_element_type=F32)
    acc += jnp.dot(oa_ref[...].astype(BF16), w_ref[RWKV_W:RWKV_W + ATT_W, :], preferred_element_type=F32)
    acc += jnp.dot(og_ref[...].astype(BF16), w_ref[RWKV_W + ATT_W:, :], preferred_element_type=F32)
    o_ref[...] = h_ref[...] + acc


def _outproj(o_rwkv, o_att, o_gla, w, h):
    m = h.shape[0]
    tm = 1024
    row = lambda wd: pl.BlockSpec((tm, wd), lambda i: (i, 0))
    return pl.pallas_call(
        _outproj_kernel,
        grid=(m // tm,),
        in_specs=[row(RWKV_W), row(ATT_W), row(GLA_W), pl.BlockSpec(w.shape, lambda i: (0, 0)), row(D_MODEL)],
        out_specs=row(D_MODEL),
        out_shape=jax.ShapeDtypeStruct(h.shape, F32),
        compiler_params=_cparams("parallel"),
        name="outproj",
    )(o_rwkv, o_att, o_gla, w, h)


def _ffn_kernel(x_ref, g_ref, wg_ref, wu_ref, wd_ref, o_ref, xn_ref, acc_ref):
    f = pl.program_id(1)

    @pl.when(f == 0)
    def _():
        xn_ref[...] = _rms(x_ref[...], g_ref[...]).astype(xn_ref.dtype)
        acc_ref[...] = jnp.zeros_like(acc_ref)

    xn = xn_ref[...]
    gate = jnp.dot(xn, wg_ref[...], preferred_element_type=F32)
    up = jnp.dot(xn, wu_ref[...], preferred_element_type=F32)
    act = (gate * _sigmoid(gate) * up).astype(BF16)
    acc_ref[...] += jnp.dot(act, wd_ref[...], preferred_element_type=F32)

    @pl.when(f == pl.num_programs(1) - 1)
    def _():
        o_ref[...] = x_ref[...] + acc_ref[...]


def _ffn(h, g, wg, wu, wd):
    m = h.shape[0]
    tm, tf = 512, FFN_DENSE // 2
    row = pl.BlockSpec((tm, D_MODEL), lambda i, f: (i, 0))
    return pl.pallas_call(
        _ffn_kernel,
        grid=(m // tm, FFN_DENSE // tf),
        in_specs=[row, pl.BlockSpec(g.shape, lambda i, f: (0, 0)),
                  pl.BlockSpec((D_MODEL, tf), lambda i, f: (0, f)),
                  pl.BlockSpec((D_MODEL, tf), lambda i, f: (0, f)),
                  pl.BlockSpec((tf, D_MODEL), lambda i, f: (f, 0))],
        out_specs=row,
        out_shape=jax.ShapeDtypeStruct(h.shape, F32),
        scratch_shapes=[pltpu.VMEM((tm, D_MODEL), BF16), pltpu.VMEM((tm, D_MODEL), F32)],
        compiler_params=_cparams("parallel", "arbitrary"),
        name="ffn_dense",
    )(h, g, wg, wu, wd)


MOE_BLOCK = 896
MOE_SUB = 256
MOE_FT = 896


def _router_kernel(x_ref, g_ref, wr_ref, tri_ref, xnt_ref, rank_ref, gate_ref, cnt_ref, *, n_tokens):
    row = lax.broadcasted_iota(jnp.int32, (MOE_BLOCK, 1), 0) + pl.program_id(0) * MOE_BLOCK
    real = row < n_tokens
    xn = jnp.where(real, _rms(x_ref[...], g_ref[...]), 0.0)
    xnt_ref[...] = xn.T.astype(BF16)
    logits = jnp.dot(xn, wr_ref[...], preferred_element_type=F32, precision=lax.Precision.HIGHEST)
    lane = lax.broadcasted_iota(jnp.int32, logits.shape, 1)
    lg = jnp.where(lane < N_EXPERTS, logits, NEG_INF)
    m1 = jnp.max(lg, axis=-1, keepdims=True)
    i1 = jnp.min(jnp.where(lg == m1, lane, LANES), axis=-1, keepdims=True)
    lg2 = jnp.where(lane == i1, NEG_INF, lg)
    m2 = jnp.max(lg2, axis=-1, keepdims=True)
    i2 = jnp.min(jnp.where(lg2 == m2, lane, LANES), axis=-1, keepdims=True)
    e2 = jnp.exp(m2 - m1)
    hot1 = (lane == i1) & real
    hot2 = (lane == i2) & real
    onehot = (hot1 | hot2).astype(F32)
    rank = jnp.dot(tri_ref[...], onehot.astype(BF16), preferred_element_type=F32)
    rank = jnp.where(hot1 | hot2, rank, -1.0)
    gate = jnp.where(hot1, 1.0 / (1.0 + e2), jnp.where(hot2, e2 / (1.0 + e2), 0.0))
    rank_ref[0] = rank.T[:N_EXPERTS]
    gate_ref[0] = gate.T[:N_EXPERTS]
    cnt_ref[0] = jnp.broadcast_to(jnp.sum(onehot, axis=0, keepdims=True), (8, LANES)).astype(jnp.int32)


def _route(h, g, w_router):
    m = h.shape[0]
    sb = MOE_BLOCK
    nblk = pl.cdiv(m, sb)
    t = jnp.arange(sb)
    tri = (t[None, :] < t[:, None]).astype(BF16)
    full = lambda a: pl.BlockSpec(a.shape, lambda i: (0,) * a.ndim)
    per_expert = pl.BlockSpec((1, N_EXPERTS, sb), lambda i: (i, 0, 0))
    return pl.pallas_call(
        functools.partial(_router_kernel, n_tokens=m),
        grid=(nblk,),
        in_specs=[pl.BlockSpec((sb, D_MODEL), lambda i: (i, 0)), full(g), full(w_router), full(tri)],
        out_specs=[pl.BlockSpec((D_MODEL, sb), lambda i: (0, i)), per_expert, per_expert,
                   pl.BlockSpec((1, 8, LANES), lambda i: (i, 0, 0))],
        out_shape=[jax.ShapeDtypeStruct((D_MODEL, nblk * sb), BF16), jax.ShapeDtypeStruct((nblk, N_EXPERTS, sb), F32),
                   jax.ShapeDtypeStruct((nblk, N_EXPERTS, sb), F32),
                   jax.ShapeDtypeStruct((nblk, 8, LANES), jnp.int32)],
        compiler_params=_cparams("parallel"),
        name="moe_router",
    )(h, g, w_router, tri)


def _moe_kernel(cnt_ref, x_ref, xnt_ref, rank_ref, gate_ref, wg_ref, wu_ref, wd_ref, fg_ref, o_ref,
                xs_ref, y_ref, acc_ref, *, final_norm):
    i, e, f = pl.program_id(0), pl.program_id(1), pl.program_id(2)
    last_f = f == pl.num_programs(2) - 1
    n_sub = (cnt_ref[i * N_EXPERTS + e] + (MOE_SUB - 1)) // MOE_SUB

    @pl.when((e == 0) & (f == 0))
    def _():
        acc_ref[...] = jnp.zeros_like(acc_ref)

    def selection(j):
        want = lax.broadcasted_iota(jnp.int32, (MOE_SUB, MOE_BLOCK), 0) + j * MOE_SUB
        return (rank_ref[0, pl.ds(e, 1), :] == want.astype(F32)).astype(BF16)

    def sub_body(j, carry):
        @pl.when(f == 0)
        def _():
            xs_ref[j] = lax.dot_general(xnt_ref[...], selection(j), NT, preferred_element_type=F32).astype(BF16)
            y_ref[j] = jnp.zeros((D_MODEL, MOE_SUB), F32)

        xs = xs_ref[j]
        gate = lax.dot_general(wg_ref[0], xs, TN, preferred_element_type=F32)
        up = lax.dot_general(wu_ref[0], xs, TN, preferred_element_type=F32)
        act = (gate * _sigmoid(gate) * up).astype(BF16)
        y_ref[j] += lax.dot_general(wd_ref[0], act, TN, preferred_element_type=F32)

        @pl.when(last_f)
        def _():
            back = jnp.dot(y_ref[j].astype(BF16), selection(j), preferred_element_type=F32)
            acc_ref[...] += gate_ref[0, pl.ds(e, 1), :] * back

        return carry

    lax.fori_loop(0, n_sub, sub_body, 0)

    @pl.when((e == N_EXPERTS - 1) & last_f)
    def _():
        y = x_ref[...] + acc_ref[...].T
        o_ref[...] = _rms(y, fg_ref[...]) if final_norm else y


def _moe(h, g, w_router, wg, wu, wd, final_g, final_norm):
    m = h.shape[0]
    sb, ft = MOE_BLOCK, MOE_FT
    xnt, rank, gate, cnt = _route(h, g, w_router)
    counts = cnt[:, 0, :N_EXPERTS].reshape(-1)
    row = pl.BlockSpec((sb, D_MODEL), lambda i, e, f, c: (i, 0))
    per_expert = pl.BlockSpec((1, N_EXPERTS, sb), lambda i, e, f, c: (i, 0, 0))
    grid_spec = pltpu.PrefetchScalarGridSpec(
        num_scalar_prefetch=1,
        grid=(pl.cdiv(m, sb), N_EXPERTS, FFN_EXPERT // ft),
        in_specs=[row, pl.BlockSpec((D_MODEL, sb), lambda i, e, f, c: (0, i)), per_expert, per_expert,
                  pl.BlockSpec((1, D_MODEL, ft), lambda i, e, f, c: (e, 0, f)),
                  pl.BlockSpec((1, D_MODEL, ft), lambda i, e, f, c: (e, 0, f)),
                  pl.BlockSpec((1, ft, D_MODEL), lambda i, e, f, c: (e, f, 0)),
                  pl.BlockSpec((1, D_MODEL), lambda i, e, f, c: (0, 0))],
        out_specs=row,
        scratch_shapes=[pltpu.VMEM((pl.cdiv(sb, MOE_SUB), D_MODEL, MOE_SUB), BF16),
                        pltpu.VMEM((pl.cdiv(sb, MOE_SUB), D_MODEL, MOE_SUB), F32),
                        pltpu.VMEM((D_MODEL, sb), F32)],
    )
    return pl.pallas_call(
        functools.partial(_moe_kernel, final_norm=final_norm),
        grid_spec=grid_spec,
        out_shape=jax.ShapeDtypeStruct(h.shape, F32),
        compiler_params=_cparams("parallel", "arbitrary", "arbitrary"),
        name="moe",
    )(counts, h, xnt, rank, gate, wg, wu, wd, final_g)


def kernel(x, positions, mix_norm_g, w_in, rwkv_mu, rwkv_w0, rwkv_w2, rwkv_a0, rwkv_a2, rwkv_g2, rwkv_k_k, rwkv_k_a, rwkv_r_k, rwkv_ln_g, rwkv_ln_b, rwkv_v0, rwkv_v1, rwkv_v2, gla_gate_up, gla_gate_b, gla_norm_g, w_out, ffn_norm_g, ffn_w_gate, ffn_w_up, ffn_w_down, moe_router, moe_w_gate, moe_w_up, moe_w_down, final_norm_g):
    B, S, D = x.shape
    M = B * S
    h = x.reshape(M, D)
    cos, sa, sb = _rope_tables(positions)
    v_first = None
    for layer in range(DEPTH):
        w = w_in[layer]
        wr = w[:, :RWKV_COLS].astype(BF16)
        wa = w[:, RWKV_COLS:RWKV_COLS + ATT_COLS].astype(BF16)
        wgl = w[:, RWKV_COLS + ATT_COLS:]
        wgl = jnp.concatenate([wgl[:, :3 * GLA_W], wgl[:, 3 * GLA_W + GLA_GATE_RANK:],
                               wgl[:, 3 * GLA_W:3 * GLA_W + GLA_GATE_RANK],
                               jnp.zeros((D, LANES - GLA_GATE_RANK), w.dtype)], axis=1).astype(BF16)
        y_rwkv, y_att, y_gla = _inproj(h, mix_norm_g[layer].reshape(1, D), wr, wa, wgl, cos, sa, sb)
        vres = None if layer == 0 else (rwkv_v0[layer - 1], rwkv_v1[layer - 1], rwkv_v2[layer - 1])
        o_rwkv, v_first = _rwkv_mixer(y_rwkv, S, rwkv_mu[layer], rwkv_w0[layer], rwkv_w2[layer], rwkv_a0[layer],
                                      rwkv_a2[layer], rwkv_g2[layer], rwkv_k_k[layer], rwkv_k_a[layer],
                                      rwkv_r_k[layer], rwkv_ln_g[layer], rwkv_ln_b[layer], v_first, vres)
        o_att = _att_mixer(y_att, S)
        o_gla = _gla_mixer(y_gla, S, gla_gate_up[layer], gla_gate_b[layer], gla_norm_g[layer])
        h = _outproj(o_rwkv, o_att, o_gla, w_out[layer].astype(BF16), h)
        i = layer // 2
        g = ffn_norm_g[layer].reshape(1, D)
        if layer % 2 == 0:
            h = _ffn(h, g, ffn_w_gate[i].astype(BF16), ffn_w_up[i].astype(BF16), ffn_w_down[i].astype(BF16))
        else:
            wrt = jnp.pad(moe_router[i], ((0, 0), (0, LANES - N_EXPERTS)))
            h = _moe(h, g, wrt, moe_w_gate[i].astype(BF16), moe_w_up[i].astype(BF16), moe_w_down[i].astype(BF16),
                     final_norm_g.reshape(1, D), layer == DEPTH - 1)
    return h.reshape(B, S, D)
```

```python
import functools

import jax
import jax.numpy as jnp
from jax import lax
from jax.experimental import pallas as pl
from jax.experimental.pallas import tpu as pltpu

D_MODEL = 1024
DEPTH = 2
HEAD_DIM = 64
RWKV_W = 256
RWKV_H = 4
ATT_W = 384
GLA_W = 384
GLA_H = 6
RWKV_DECAY_RANK = 32
RWKV_AAA_RANK = 32
RWKV_GATE_RANK = 64
RWKV_GN_EPS = 64e-5
GLA_GATE_RANK = 16
GLA_GATE_TAU = 16.0
CHUNK = 64
DILATIONS = (1, 4, 16)
ATT_BLOCK = 128
ROPE_THETA = 500000.0
ROPE_DIMS = 16
ROPE_HALF = 8
FFN_DENSE = 2816
N_EXPERTS = 8
FFN_EXPERT = 3584
RMS_EPS = 1e-5
NEG_INF = -1e30

RWKV_COLS = 3 * RWKV_W + RWKV_DECAY_RANK + RWKV_AAA_RANK + RWKV_GATE_RANK
ATT_COLS = 3 * ATT_W
LANES = 128
GLA_COLS_PAD = 4 * GLA_W + LANES

VMEM_LIMIT = 56 * 1024 * 1024

F32 = jnp.float32
BF16 = jnp.bfloat16
NN = (((1,), (0,)), ((), ()))
NT = (((1,), (1,)), ((), ()))
TN = (((0,), (0,)), ((), ()))
BNN = (((2,), (1,)), ((0,), (0,)))
BNT = (((2,), (2,)), ((0,), (0,)))
BTN = (((1,), (1,)), ((0,), (0,)))


def _cparams(*sem):
    return pltpu.CompilerParams(dimension_semantics=sem, vmem_limit_bytes=VMEM_LIMIT)


def _rms(x, g):
    return x * lax.rsqrt(jnp.mean(x * x, axis=-1, keepdims=True) + RMS_EPS) * g


def _sigmoid(x):
    return 1.0 / (1.0 + jnp.exp(-x))


def _log_sigmoid(x):
    return jnp.minimum(x, 0.0) - jnp.log(1.0 + jnp.exp(-jnp.abs(x)))


def _split_hi_lo(a):
    hi = a.astype(BF16)
    return hi, (a - hi.astype(F32)).astype(BF16)


def _dot3(a, b, dims=NN):
    a_hi, a_lo = _split_hi_lo(a)
    b_hi, b_lo = _split_hi_lo(b)
    d = lambda x, y: lax.dot_general(x, y, dims, preferred_element_type=F32)
    return d(a_hi, b_hi) + d(a_lo, b_hi) + d(a_hi, b_lo)


def _dot_exact_rhs(x, r_bf16):
    x_hi, x_lo = _split_hi_lo(x)
    return jnp.dot(x_hi, r_bf16, preferred_element_type=F32) + jnp.dot(x_lo, r_bf16, preferred_element_type=F32)


def _bdot(a, b, dims, passes):
    d = lambda x, y: lax.dot_general(x, y, dims, preferred_element_type=F32)
    if passes == 1:
        return d(a.astype(BF16), b.astype(BF16))
    a_hi, a_lo = _split_hi_lo(a)
    b_hi, b_lo = _split_hi_lo(b)
    return d(a_hi, b_hi) + d(a_lo, b_hi) + d(a_hi, b_lo)


def _head_ones(width):
    idx = jnp.arange(width) // HEAD_DIM
    return (idx[:, None] == idx[None, :]).astype(BF16)


def _chunk_selectors():
    t = jnp.arange(CHUNK)
    return (t[None, :] <= t[:, None]).astype(BF16), jnp.ones((CHUNK, CHUNK), BF16)


def _chunk_sums(sel, x):
    nc = x.shape[0] // CHUNK
    hi, lo = _split_hi_lo(x.reshape(nc, CHUNK, x.shape[1]))
    s3 = jnp.broadcast_to(sel[None], (nc, CHUNK, CHUNK))
    d = lambda a, b: lax.dot_general(a, b, BNN, preferred_element_type=F32)
    return (d(s3, hi) + d(s3, lo)).reshape(x.shape)


def _rope_table_kernel(pos_ref, invf_ref, cos_ref, sa_ref, sb_ref):
    ang = pos_ref[...].astype(F32) * invf_ref[...]
    lane = lax.broadcasted_iota(jnp.int32, ang.shape, 1) % HEAD_DIM
    c = jnp.cos(ang)
    s = jnp.sin(ang)
    cos_ref[...] = jnp.where(lane < ROPE_DIMS, c, 1.0)
    sa_ref[...] = jnp.where((lane >= ROPE_HALF) & (lane < ROPE_DIMS), s, 0.0)
    sb_ref[...] = jnp.where(lane < ROPE_HALF, -s, 0.0)


def _rope_tables(positions):
    m = positions.size
    tm = 2048
    lane = jnp.arange(LANES) % ROPE_HALF
    invf = (ROPE_THETA ** (-lane.astype(F32) / ROPE_HALF)).reshape(1, LANES)
    out = jax.ShapeDtypeStruct((m, LANES), F32)
    row = pl.BlockSpec((tm, LANES), lambda i: (i, 0))
    return pl.pallas_call(
        _rope_table_kernel,
        grid=(m // tm,),
        in_specs=[pl.BlockSpec((tm, 1), lambda i: (i, 0)), pl.BlockSpec((1, LANES), lambda i: (0, 0))],
        out_specs=[row, row, row],
        out_shape=[out, out, out],
        compiler_params=_cparams("parallel"),
        name="rope_tables",
    )(positions.reshape(m, 1), invf)


def _inproj_kernel(x_ref, g_ref, wr_ref, wa_ref, wg_ref, cos_ref, sa_ref, sb_ref, yr_ref, ya_ref, yg_ref):
    xb = _rms(x_ref[...], g_ref[...]).astype(BF16)
    yr_ref[...] = jnp.dot(xb, wr_ref[...], preferred_element_type=F32)
    yg_ref[...] = jnp.dot(xb, wg_ref[...], preferred_element_type=F32)
    ya = jnp.dot(xb, wa_ref[...], preferred_element_type=F32)
    cos, sa, sb = cos_ref[...], sa_ref[...], sb_ref[...]
    for j in range(2 * ATT_W // LANES):
        blk = ya[:, j * LANES:(j + 1) * LANES]
        rot = blk * cos + pltpu.roll(blk, ROPE_HALF, 1) * sa + pltpu.roll(blk, LANES - ROPE_HALF, 1) * sb
        if j < ATT_W // LANES:
            rot = rot * HEAD_DIM ** -0.5
        ya_ref[:, j * LANES:(j + 1) * LANES] = rot.astype(ya_ref.dtype)
    ya_ref[:, 2 * ATT_W:] = ya[:, 2 * ATT_W:].astype(ya_ref.dtype)


def _inproj(h, g, wr, wa, wg, cos, sa, sb):
    m = h.shape[0]
    tm = 512
    row = lambda w: pl.BlockSpec((tm, w), lambda i: (i, 0))
    full = lambda a: pl.BlockSpec(a.shape, lambda i: (0, 0))
    return pl.pallas_call(
        _inproj_kernel,
        grid=(m // tm,),
        in_specs=[row(D_MODEL), full(g), full(wr), full(wa), full(wg), row(LANES), row(LANES), row(LANES)],
        out_specs=[row(RWKV_COLS), row(ATT_COLS), row(GLA_COLS_PAD)],
        out_shape=[jax.ShapeDtypeStruct((m, RWKV_COLS), F32),
                   jax.ShapeDtypeStruct((m, ATT_COLS), BF16),
                   jax.ShapeDtypeStruct((m, GLA_COLS_PAD), F32)],
        compiler_params=_cparams("parallel"),
        name="inproj",
    )(h, g, wr, wa, wg, cos, sa, sb)


RWKV_GROUP = 4


def _rwkv_kernel(*refs, has_vres, tile):
    nc = tile // CHUNK
    if has_vres:
        (y_ref, vf_ref, mu_ref, w0_ref, w2_ref, a0_ref, a2_ref, g2_ref, kk_ref, ka_ref, rk_ref, lng_ref, lnb_ref,
         psel_ref, tsel_ref, hones_ref, v0_ref, v1_ref, v2_ref, o_ref, *scratch) = refs
    else:
        (y_ref, mu_ref, w0_ref, w2_ref, a0_ref, a2_ref, g2_ref, kk_ref, ka_ref, rk_ref, lng_ref, lnb_ref,
         psel_ref, tsel_ref, hones_ref, o_ref, vfo_ref, *scratch) = refs
    (state_ref, prev_ref, at_ref, rt_ref, bt_ref, kt_ref, bh_ref, kh_ref, vv_ref, gam_ref,
     qt_ref, ol_ref, gm_ref, hm_ref, sb_ref, os_ref) = scratch

    @pl.when(pl.program_id(1) == 0)
    def _():
        state_ref[...] = jnp.zeros_like(state_ref)
        prev_ref[...] = jnp.zeros_like(prev_ref)

    y = y_ref[...]
    row = lax.broadcasted_iota(jnp.int32, (tile, 1), 0)
    ysh = jnp.where(row == 0, prev_ref[...], pltpu.roll(y, 1, 0))
    prev_ref[...] = y[tile - 1:tile, :]
    ym = y + (ysh - y) * mu_ref[...]
    r = ym[:, 0:RWKV_W]
    k = ym[:, RWKV_W:2 * RWKV_W]
    v = ym[:, 2 * RWKV_W:3 * RWKV_W]
    x6 = ym[:, 3 * RWKV_W:]
    wl = w0_ref[...] + _dot3(jnp.tanh(x6), w2_ref[...])
    lw = -jnp.exp(_log_sigmoid(wl) - 0.5)
    a = _sigmoid(a0_ref[...] + _dot3(x6, a2_ref[...]))
    g = _dot3(_sigmoid(x6), g2_ref[...])
    if has_vres:
        v = v + (vf_ref[...] - v) * _sigmoid(v0_ref[...] + _dot3(_dot3(v, v1_ref[...]), v2_ref[...]))
    else:
        vfo_ref[...] = v

    head_ones = hones_ref[...]
    kk = k * kk_ref[...]
    kk = kk * lax.rsqrt(jnp.maximum(_dot_exact_rhs(kk * kk, head_ones), 1e-24))
    k2 = k * (1.0 + (a - 1.0) * ka_ref[...])
    bonus = _dot_exact_rhs(r * k2 * rk_ref[...], head_ones) * v

    c = _chunk_sums(psel_ref[...], lw)
    c_last = _chunk_sums(tsel_ref[...], lw)
    b = kk * a
    e_neg = jnp.exp(-c)
    e_end = jnp.exp(c_last - c)

    def put(ref, val):
        for h in range(RWKV_H):
            ref[h] = val[:, h * HEAD_DIM:(h + 1) * HEAD_DIM].reshape(nc, CHUNK, HEAD_DIM)

    put(rt_ref, r * jnp.exp(c))
    put(at_ref, -kk * jnp.exp(c - lw))
    put(bt_ref, b * e_neg)
    put(kt_ref, k2 * e_neg)
    put(bh_ref, b * e_end)
    put(kh_ref, k2 * e_end)
    put(vv_ref, v)
    put(gam_ref, jnp.exp(c_last))

    n = RWKV_H * RWKV_GROUP
    qi = lax.broadcasted_iota(jnp.int32, (n, CHUNK, CHUNK), 1)
    qj = lax.broadcasted_iota(jnp.int32, (n, CHUNK, CHUNK), 2)
    strict = qj < qi
    incl = qj <= qi
    eye = qj == qi
    same_block = [qi // size == qj // size for size in (2, 4, 8, 16, 32, CHUNK)]

    def group_body(gi, carry):
        cs = pl.ds(gi * RWKV_GROUP, RWKV_GROUP)
        ld = lambda ref: ref[:, cs].reshape(n, CHUNK, HEAD_DIM)
        at, rt, bt, kt, bh, kh, vv = (ld(x) for x in (at_ref, rt_ref, bt_ref, kt_ref, bh_ref, kh_ref, vv_ref))
        gam = ld(gam_ref)[:, 0:1, :]
        a_ab = jnp.where(strict, _bdot(at, bt, BNT, 1), 0.0)
        a_ak = jnp.where(strict, _bdot(at, kt, BNT, 1), 0.0)
        p_b = jnp.where(incl, _bdot(rt, bt, BNT, 1), 0.0)
        p_k = jnp.where(incl, _bdot(rt, kt, BNT, 1), 0.0)
        t = jnp.where(eye, 1.0, jnp.where(same_block[0], a_ab, 0.0))
        for small, big in zip(same_block[:-1], same_block[1:]):
            t = t + _bdot(_bdot(t, jnp.where(big & ~small, a_ab, 0.0), BNN, 1), t, BNN, 1)
        w = _bdot(t, at, BNN, 1)
        u_loc = _bdot(t, _bdot(a_ak, vv, BNN, 1), BNN, 1)

        def st(ref, val):
            ref[:, cs] = val.reshape(RWKV_H, RWKV_GROUP, CHUNK, HEAD_DIM)

        st(qt_ref, rt + _bdot(p_b, w, BNN, 1))
        st(ol_ref, _bdot(p_b, u_loc, BNN, 1) + _bdot(p_k, vv, BNN, 1))
        st(gm_ref, _bdot(w, bh, BTN, 1) + jnp.where(eye, gam, 0.0))
        st(hm_ref, _bdot(u_loc, bh, BTN, 1) + _bdot(vv, kh, BTN, 1))
        return carry

    lax.fori_loop(0, nc // RWKV_GROUP, group_body, 0)

    def chunk_body(ci, s):
        sb_ref[:, ci] = s
        return _bdot(s, gm_ref[:, ci], BNN, 3) + hm_ref[:, ci]

    state_ref[...] = lax.fori_loop(0, nc, chunk_body, state_ref[...], unroll=True)

    to_units = lambda ref: ref[...].reshape(RWKV_H * nc, CHUNK, HEAD_DIM)
    o = _bdot(to_units(qt_ref), to_units(sb_ref), BNT, 3) + to_units(ol_ref)
    for h in range(RWKV_H):
        os_ref[:, h * HEAD_DIM:(h + 1) * HEAD_DIM] = o[h * nc:(h + 1) * nc].reshape(tile, HEAD_DIM)

    o = os_ref[...]
    mean = _dot_exact_rhs(o, head_ones) * (1.0 / HEAD_DIM)
    oc = o - mean
    var = _dot_exact_rhs(oc * oc, head_ones) * (1.0 / HEAD_DIM)
    on = oc * lax.rsqrt(var + RWKV_GN_EPS) * lng_ref[...] + lnb_ref[...]
    o_ref[...] = (on + bonus) * g


def _rwkv_mixer(y, seq_len, mu, w0, w2, a0, a2, g2, k_k, k_a, r_k, ln_g, ln_b, v_first, vres):
    m = y.shape[0]
    tile = 512
    nt = seq_len // tile
    has_vres = vres is not None
    row = lambda wd: pl.BlockSpec((tile, wd), lambda b, j: (b * nt + j, 0))
    full = lambda arr: pl.BlockSpec(arr.shape, lambda b, j: (0, 0))
    vec = lambda t: t.reshape(1, -1)
    pad_rows = lambda w, start: jnp.zeros((LANES, RWKV_W), F32).at[start:start + w.shape[0]].set(w)
    consts = [vec(mu), vec(w0), pad_rows(w2, 0), vec(a0), pad_rows(a2, RWKV_DECAY_RANK),
              pad_rows(g2, RWKV_DECAY_RANK + RWKV_AAA_RANK), vec(k_k), vec(k_a), vec(r_k), vec(ln_g), vec(ln_b),
              *_chunk_selectors(), _head_ones(RWKV_W)]
    args = [y] + ([v_first] if has_vres else []) + consts
    in_specs = [row(RWKV_COLS)] + ([row(RWKV_W)] if has_vres else []) + [full(c) for c in consts]
    out_shape = [jax.ShapeDtypeStruct((m, RWKV_W), F32)]
    out_specs = [row(RWKV_W)]
    if has_vres:
        v0, v1, v2 = vres
        extra = [vec(v0), jnp.pad(v1, ((0, 0), (0, LANES - v1.shape[1]))),
                 jnp.pad(v2, ((0, LANES - v2.shape[0]), (0, 0)))]
        args += extra
        in_specs += [full(c) for c in extra]
    else:
        out_shape.append(jax.ShapeDtypeStruct((m, RWKV_W), F32))
        out_specs.append(row(RWKV_W))
    unit_buf = pltpu.VMEM((RWKV_H, tile // CHUNK, CHUNK, HEAD_DIM), F32)
    outs = pl.pallas_call(
        functools.partial(_rwkv_kernel, has_vres=has_vres, tile=tile),
        grid=(m // seq_len, nt),
        in_specs=in_specs, out_specs=out_specs, out_shape=out_shape,
        scratch_shapes=[pltpu.VMEM((RWKV_H, HEAD_DIM, HEAD_DIM), F32), pltpu.VMEM((1, RWKV_COLS), F32)]
                       + [unit_buf] * 13 + [pltpu.VMEM((tile, RWKV_W), F32)],
        compiler_params=_cparams("parallel", "arbitrary"),
        name="rwkv7",
    )(*args)
    return (outs[0], v_first) if has_vres else (outs[0], outs[1])


ATT_PAD = ATT_BLOCK * max(DILATIONS)
ATT_UNROLL = 4


def _att_kernel(q_ref, k_ref, v_ref, o_ref, qs_ref, ks_ref, vs_ref, m_ref, n_ref, d_ref, *, seq_len):
    qs_ref[...] = q_ref[...].astype(F32)
    ks_ref[0:ATT_PAD, :] = jnp.zeros((ATT_PAD, LANES), F32)
    vs_ref[0:ATT_PAD, :] = jnp.zeros((ATT_PAD, LANES), F32)
    ks_ref[ATT_PAD:, :] = k_ref[...].astype(F32)
    vs_ref[ATT_PAD:, :] = v_ref[...].astype(F32)

    qi = lax.broadcasted_iota(jnp.int32, (ATT_BLOCK, 2 * ATT_BLOCK), 0)
    kc = lax.broadcasted_iota(jnp.int32, (ATT_BLOCK, 2 * ATT_BLOCK), 1)
    band = (kc >= qi) & (kc <= qi + ATT_BLOCK)
    head0 = lax.broadcasted_iota(jnp.int32, (ATT_BLOCK, LANES), 1) < HEAD_DIM

    def block(g, dil, first, q, k, v, rows):
        valid = band & ((kc >= ATT_BLOCK) | jnp.logical_not(first))
        v1 = jnp.concatenate([v, jnp.ones_like(v)], axis=1)
        mxs, pvs = [], []
        for hh in range(2):
            qh = jnp.where(head0 if hh == 0 else jnp.logical_not(head0), q, jnp.zeros_like(q))
            s = jnp.where(valid, lax.dot_general(qh, k, NT, preferred_element_type=F32), NEG_INF)
            mx = jnp.max(s, axis=-1, keepdims=True)
            mxs.append(mx)
            pvs.append(jnp.dot(jnp.exp(s - mx).astype(BF16), v1, preferred_element_type=F32))
        mx = jnp.where(head0, mxs[0], mxs[1])
        pv = jnp.where(head0, pvs[0][:, :LANES], pvs[1][:, :LANES])
        den = jnp.where(head0, pvs[0][:, LANES:], pvs[1][:, LANES:])
        if g > 0:
            m_old = m_ref[rows, :]
            m_new = jnp.maximum(m_old, mx)
            a_old = jnp.exp(m_old - m_new)
            a_new = jnp.exp(mx - m_new)
            pv = a_old * n_ref[rows, :] + a_new * pv
            den = a_old * d_ref[rows, :] + a_new * den
            mx = m_new
        if g == len(DILATIONS) - 1:
            o_ref[rows, :] = pv / den
        else:
            m_ref[rows, :] = mx
            n_ref[rows, :] = pv
            d_ref[rows, :] = den

    for g, dil in enumerate(DILATIONS):
        npairs = seq_len // dil // ATT_BLOCK // 2
        step = ATT_BLOCK * dil

        def body(it, carry, g=g, dil=dil, npairs=npairs, step=step):
            r = it // npairs
            jp = it % npairs
            start = r + jp * (2 * step)
            q = qs_ref[pl.ds(start, 2 * ATT_BLOCK, stride=dil), :].astype(BF16)
            k = ks_ref[pl.ds(start + (ATT_PAD - step), 3 * ATT_BLOCK, stride=dil), :].astype(BF16)
            v = vs_ref[pl.ds(start + (ATT_PAD - step), 3 * ATT_BLOCK, stride=dil), :].astype(BF16)
            for u in range(2):
                block(g, dil, (jp == 0) if u == 0 else False, q[u * ATT_BLOCK:(u + 1) * ATT_BLOCK],
                      k[u * ATT_BLOCK:(u + 2) * ATT_BLOCK], v[u * ATT_BLOCK:(u + 2) * ATT_BLOCK],
                      pl.ds(start + u * step, ATT_BLOCK, stride=dil))
            return carry

        lax.fori_loop(0, dil * npairs, body, 0, unroll=ATT_UNROLL)


def _att_mixer(y_att, seq_len):
    m = y_att.shape[0]
    npair = ATT_W // LANES
    col = lambda c: pl.BlockSpec((seq_len, LANES), lambda b, hp: (b, c * npair + hp))
    seq_buf = pltpu.VMEM((seq_len, LANES), F32)
    pad_buf = pltpu.VMEM((ATT_PAD + seq_len, LANES), F32)
    return pl.pallas_call(
        functools.partial(_att_kernel, seq_len=seq_len),
        grid=(m // seq_len, npair),
        in_specs=[col(0), col(1), col(2)],
        out_specs=pl.BlockSpec((seq_len, LANES), lambda b, hp: (b, hp)),
        out_shape=jax.ShapeDtypeStruct((m, ATT_W), F32),
        scratch_shapes=[seq_buf, pad_buf, pad_buf, seq_buf, seq_buf, seq_buf],
        compiler_params=_cparams("parallel", "parallel"),
        name="dilated_att",
    )(y_att, y_att, y_att)


def _gla_kernel(y_ref, gu_ref, gb_ref, ng_ref, psel_ref, tsel_ref, hones_ref, o_ref, state_ref, os_ref, *, tile):
    nc = tile // CHUNK
    n = GLA_H * nc

    @pl.when(pl.program_id(1) == 0)
    def _():
        state_ref[...] = jnp.zeros_like(state_ref)

    def units(val):
        return jnp.concatenate([val[:, h * HEAD_DIM:(h + 1) * HEAD_DIM].reshape(nc, CHUNK, HEAD_DIM)
                                for h in range(GLA_H)], axis=0)

    q = y_ref[:, 0:GLA_W] * HEAD_DIM ** -0.5
    k = y_ref[:, GLA_W:2 * GLA_W]
    og = y_ref[:, 3 * GLA_W:4 * GLA_W]
    lg = _log_sigmoid(_dot3(y_ref[:, 4 * GLA_W:], gu_ref[...]) + gb_ref[...]) * (1.0 / GLA_GATE_TAU)
    b = _chunk_sums(psel_ref[...], lg)
    b_last = _chunk_sums(tsel_ref[...], lg)
    q_in = units((q * jnp.exp(b)).astype(BF16))
    k_in = units((k * jnp.exp(-b)).astype(BF16))
    k_end = units((k * jnp.exp(b_last - b)).astype(BF16))
    v = units(y_ref[:, 2 * GLA_W:3 * GLA_W].astype(BF16))
    gam = units(jnp.exp(b_last))[:, 0:1, :].reshape(GLA_H, nc, 1, HEAD_DIM)

    ci = lax.broadcasted_iota(jnp.int32, (n, CHUNK, CHUNK), 1)
    cj = lax.broadcasted_iota(jnp.int32, (n, CHUNK, CHUNK), 2)
    d = lambda a, bb, dims: lax.dot_general(a, bb, dims, preferred_element_type=F32)
    att = jnp.where(cj <= ci, d(q_in, k_in, BNT), 0.0)
    o = d(att.astype(BF16), v, BNN)
    kv = d(v, k_end, BTN).reshape(GLA_H, nc, HEAD_DIM, HEAD_DIM)

    s = state_ref[...]
    before = []
    for c in range(nc):
        before.append(s)
        s = s * gam[:, c] + kv[:, c]
    state_ref[...] = s
    s_before = jnp.stack(before, axis=1).reshape(n, HEAD_DIM, HEAD_DIM)
    o = o + d(q_in, s_before.astype(BF16), BNT)
    for h in range(GLA_H):
        os_ref[:, h * HEAD_DIM:(h + 1) * HEAD_DIM] = o[h * nc:(h + 1) * nc].reshape(tile, HEAD_DIM)

    o = os_ref[...]
    ms = _dot_exact_rhs(o * o, hones_ref[...]) * (1.0 / HEAD_DIM)
    o_ref[...] = o * lax.rsqrt(ms + RMS_EPS) * ng_ref[...] * (og * _sigmoid(og))


def _gla_mixer(y, seq_len, gate_up, gate_b, norm_g):
    m = y.shape[0]
    tile = 512
    nt = seq_len // tile
    gu = jnp.pad(gate_up, ((0, LANES - gate_up.shape[0]), (0, 0)))
    gb = gate_b.reshape(1, GLA_W)
    ng = jnp.tile(norm_g, GLA_H).reshape(1, GLA_W)
    consts = [gu, gb, ng, *_chunk_selectors(), _head_ones(GLA_W)]
    full = lambda a: pl.BlockSpec(a.shape, lambda b, j: (0, 0))
    return pl.pallas_call(
        functools.partial(_gla_kernel, tile=tile),
        grid=(m // seq_len, nt),
        in_specs=[pl.BlockSpec((tile, GLA_COLS_PAD), lambda b, j: (b * nt + j, 0))] + [full(c) for c in consts],
        out_specs=pl.BlockSpec((tile, GLA_W), lambda b, j: (b * nt + j, 0)),
        out_shape=jax.ShapeDtypeStruct((m, GLA_W), F32),
        scratch_shapes=[pltpu.VMEM((GLA_H, HEAD_DIM, HEAD_DIM), F32), pltpu.VMEM((tile, GLA_W), F32)],
        compiler_params=_cparams("parallel", "arbitrary"),
        name="gla",
    )(y, *consts)


def _outproj_kernel(or_ref, oa_ref, og_ref, w_ref, h_ref, o_ref):
    acc = jnp.dot(or_ref[...].astype(BF16), w_ref[0:RWKV_W, :], preferred_element_type=F32)
    acc += jnp.dot(oa_ref[...].astype(BF16), w_ref[RWKV_W:RWKV_W + ATT_W, :], preferred_element_type=F32)
    acc += jnp.dot(og_ref[...].astype(BF16), w_ref[RWKV_W + ATT_W:, :], preferred_element_type=F32)
    o_ref[...] = h_ref[...] + acc


def _outproj(o_rwkv, o_att, o_gla, w, h):
    m = h.shape[0]
    tm = 1024
    row = lambda wd: pl.BlockSpec((tm, wd), lambda i: (i, 0))
    return pl.pallas_call(
        _outproj_kernel,
        grid=(m // tm,),
        in_specs=[row(RWKV_W), row(ATT_W), row(GLA_W), pl.BlockSpec(w.shape, lambda i: (0, 0)), row(D_MODEL)],
        out_specs=row(D_MODEL),
        out_shape=jax.ShapeDtypeStruct(h.shape, F32),
        compiler_params=_cparams("parallel"),
        name="outproj",
    )(o_rwkv, o_att, o_gla, w, h)


def _ffn_kernel(x_ref, g_ref, wg_ref, wu_ref, wd_ref, o_ref, xn_ref, acc_ref):
    f = pl.program_id(1)

    @pl.when(f == 0)
    def _():
        xn_ref[...] = _rms(x_ref[...], g_ref[...]).astype(xn_ref.dtype)
        acc_ref[...] = jnp.zeros_like(acc_ref)

    xn = xn_ref[...]
    gate = jnp.dot(xn, wg_ref[...], preferred_element_type=F32)
    up = jnp.dot(xn, wu_ref[...], preferred_element_type=F32)
    act = (gate * _sigmoid(gate) * up).astype(BF16)
    acc_ref[...] += jnp.dot(act, wd_ref[...], preferred_element_type=F32)

    @pl.when(f == pl.num_programs(1) - 1)
    def _():
        o_ref[...] = x_ref[...] + acc_ref[...]


def _ffn(h, g, wg, wu, wd):
    m = h.shape[0]
    tm, tf = 512, FFN_DENSE // 2
    row = pl.BlockSpec((tm, D_MODEL), lambda i, f: (i, 0))
    return pl.pallas_call(
        _ffn_kernel,
        grid=(m // tm, FFN_DENSE // tf),
        in_specs=[row, pl.BlockSpec(g.shape, lambda i, f: (0, 0)),
                  pl.BlockSpec((D_MODEL, tf), lambda i, f: (0, f)),
                  pl.BlockSpec((D_MODEL, tf), lambda i, f: (0, f)),
                  pl.BlockSpec((tf, D_MODEL), lambda i, f: (f, 0))],
        out_specs=row,
        out_shape=jax.ShapeDtypeStruct(h.shape, F32),
        scratch_shapes=[pltpu.VMEM((tm, D_MODEL), BF16), pltpu.VMEM((tm, D_MODEL), F32)],
        compiler_params=_cparams("parallel", "arbitrary"),
        name="ffn_dense",
    )(h, g, wg, wu, wd)


MOE_BLOCK = 896
MOE_SUB = 256
MOE_FT = 896


def _router_kernel(x_ref, g_ref, wr_ref, tri_ref, xnt_ref, rank_ref, gate_ref, cnt_ref, *, n_tokens):
    row = lax.broadcasted_iota(jnp.int32, (MOE_BLOCK, 1), 0) + pl.program_id(0) * MOE_BLOCK
    real = row < n_tokens
    xn = jnp.where(real, _rms(x_ref[...], g_ref[...]), 0.0)
    xnt_ref[...] = xn.T.astype(BF16)
    logits = jnp.dot(xn, wr_ref[...], preferred_element_type=F32, precision=lax.Precision.HIGHEST)
    lane = lax.broadcasted_iota(jnp.int32, logits.shape, 1)
    lg = jnp.where(lane < N_EXPERTS, logits, NEG_INF)
    m1 = jnp.max(lg, axis=-1, keepdims=True)
    i1 = jnp.min(jnp.where(lg == m1, lane, LANES), axis=-1, keepdims=True)
    lg2 = jnp.where(lane == i1, NEG_INF, lg)
    m2 = jnp.max(lg2, axis=-1, keepdims=True)
    i2 = jnp.min(jnp.where(lg2 == m2, lane, LANES), axis=-1, keepdims=True)
    e2 = jnp.exp(m2 - m1)
    hot1 = (lane == i1) & real
    hot2 = (lane == i2) & real
    onehot = (hot1 | hot2).astype(F32)
    rank = jnp.dot(tri_ref[...], onehot.astype(BF16), preferred_element_type=F32)
    rank = jnp.where(hot1 | hot2, rank, -1.0)
    gate = jnp.where(hot1, 1.0 / (1.0 + e2), jnp.where(hot2, e2 / (1.0 + e2), 0.0))
    rank_ref[0] = rank.T[:N_EXPERTS]
    gate_ref[0] = gate.T[:N_EXPERTS]
    cnt_ref[0] = jnp.broadcast_to(jnp.sum(onehot, axis=0, keepdims=True), (8, LANES)).astype(jnp.int32)


def _route(h, g, w_router):
    m = h.shape[0]
    sb = MOE_BLOCK
    nblk = pl.cdiv(m, sb)
    t = jnp.arange(sb)
    tri = (t[None, :] < t[:, None]).astype(BF16)
    full = lambda a: pl.BlockSpec(a.shape, lambda i: (0,) * a.ndim)
    per_expert = pl.BlockSpec((1, N_EXPERTS, sb), lambda i: (i, 0, 0))
    return pl.pallas_call(
        functools.partial(_router_kernel, n_tokens=m),
        grid=(nblk,),
        in_specs=[pl.BlockSpec((sb, D_MODEL), lambda i: (i, 0)), full(g), full(w_router), full(tri)],
        out_specs=[pl.BlockSpec((D_MODEL, sb), lambda i: (0, i)), per_expert, per_expert,
                   pl.BlockSpec((1, 8, LANES), lambda i: (i, 0, 0))],
        out_shape=[jax.ShapeDtypeStruct((D_MODEL, nblk * sb), BF16), jax.ShapeDtypeStruct((nblk, N_EXPERTS, sb), F32),
                   jax.ShapeDtypeStruct((nblk, N_EXPERTS, sb), F32),
                   jax.ShapeDtypeStruct((nblk, 8, LANES), jnp.int32)],
        compiler_params=_cparams("parallel"),
        name="moe_router",
    )(h, g, w_router, tri)


def _moe_kernel(cnt_ref, x_ref, xnt_ref, rank_ref, gate_ref, wg_ref, wu_ref, wd_ref, fg_ref, o_ref,
                xs_ref, y_ref, acc_ref, *, final_norm):
    i, e, f = pl.program_id(0), pl.program_id(1), pl.program_id(2)
    last_f = f == pl.num_programs(2) - 1
    n_sub = (cnt_ref[i * N_EXPERTS + e] + (MOE_SUB - 1)) // MOE_SUB

    @pl.when((e == 0) & (f == 0))
    def _():
        acc_ref[...] = jnp.zeros_like(acc_ref)

    def selection(j):
        want = lax.broadcasted_iota(jnp.int32, (MOE_SUB, MOE_BLOCK), 0) + j * MOE_SUB
        return (rank_ref[0, pl.ds(e, 1), :] == want.astype(F32)).astype(BF16)

    def sub_body(j, carry):
        @pl.when(f == 0)
        def _():
            xs_ref[j] = lax.dot_general(xnt_ref[...], selection(j), NT, preferred_element_type=F32).astype(BF16)
            y_ref[j] = jnp.zeros((D_MODEL, MOE_SUB), F32)

        xs = xs_ref[j]
        gate = lax.dot_general(wg_ref[0], xs, TN, preferred_element_type=F32)
        up = lax.dot_general(wu_ref[0], xs, TN, preferred_element_type=F32)
        act = (gate * _sigmoid(gate) * up).astype(BF16)
        y_ref[j] += lax.dot_general(wd_ref[0], act, TN, preferred_element_type=F32)

        @pl.when(last_f)
        def _():
            back = jnp.dot(y_ref[j].astype(BF16), selection(j), preferred_element_type=F32)
            acc_ref[...] += gate_ref[0, pl.ds(e, 1), :] * back

        return carry

    lax.fori_loop(0, n_sub, sub_body, 0)

    @pl.when((e == N_EXPERTS - 1) & last_f)
    def _():
        y = x_ref[...] + acc_ref[...].T
        o_ref[...] = _rms(y, fg_ref[...]) if final_norm else y


def _moe(h, g, w_router, wg, wu, wd, final_g, final_norm):
    m = h.shape[0]
    sb, ft = MOE_BLOCK, MOE_FT
    xnt, rank, gate, cnt = _route(h, g, w_router)
    counts = cnt[:, 0, :N_EXPERTS].reshape(-1)
    row = pl.BlockSpec((sb, D_MODEL), lambda i, e, f, c: (i, 0))
    per_expert = pl.BlockSpec((1, N_EXPERTS, sb), lambda i, e, f, c: (i, 0, 0))
    grid_spec = pltpu.PrefetchScalarGridSpec(
        num_scalar_prefetch=1,
        grid=(pl.cdiv(m, sb), N_EXPERTS, FFN_EXPERT // ft),
        in_specs=[row, pl.BlockSpec((D_MODEL, sb), lambda i, e, f, c: (0, i)), per_expert, per_expert,
                  pl.BlockSpec((1, D_MODEL, ft), lambda i, e, f, c: (e, 0, f)),
                  pl.BlockSpec((1, D_MODEL, ft), lambda i, e, f, c: (e, 0, f)),
                  pl.BlockSpec((1, ft, D_MODEL), lambda i, e, f, c: (e, f, 0)),
                  pl.BlockSpec((1, D_MODEL), lambda i, e, f, c: (0, 0))],
        out_specs=row,
        scratch_shapes=[pltpu.VMEM((pl.cdiv(sb, MOE_SUB), D_MODEL, MOE_SUB), BF16),
                        pltpu.VMEM((pl.cdiv(sb, MOE_SUB), D_MODEL, MOE_SUB), F32),
                        pltpu.VMEM((D_MODEL, sb), F32)],
    )
    return pl.pallas_call(
        functools.partial(_moe_kernel, final_norm=final_norm),
        grid_spec=grid_spec,
        out_shape=jax.ShapeDtypeStruct(h.shape, F32),
        compiler_params=_cparams("parallel", "arbitrary", "arbitrary"),
        name="moe",
    )(counts, h, xnt, rank, gate, wg, wu, wd, final_g)


def kernel(x, positions, mix_norm_g, w_in, rwkv_mu, rwkv_w0, rwkv_w2, rwkv_a0, rwkv_a2, rwkv_g2, rwkv_k_k, rwkv_k_a, rwkv_r_k, rwkv_ln_g, rwkv_ln_b, rwkv_v0, rwkv_v1, rwkv_v2, gla_gate_up, gla_gate_b, gla_norm_g, w_out, ffn_norm_g, ffn_w_gate, ffn_w_up, ffn_w_down, moe_router, moe_w_gate, moe_w_up, moe_w_down, final_norm_g):
    B, S, D = x.shape
    M = B * S
    h = x.reshape(M, D)
    cos, sa, sb = _rope_tables(positions)
    v_first = None
    for layer in range(DEPTH):
        w = w_in[layer]
        wr = w[:, :RWKV_COLS].astype(BF16)
        wa = w[:, RWKV_COLS:RWKV_COLS + ATT_COLS].astype(BF16)
        wgl = w[:, RWKV_COLS + ATT_COLS:]
        wgl = jnp.concatenate([wgl[:, :3 * GLA_W], wgl[:, 3 * GLA_W + GLA_GATE_RANK:],
                               wgl[:, 3 * GLA_W:3 * GLA_W + GLA_GATE_RANK],
                               jnp.zeros((D, LANES - GLA_GATE_RANK), w.dtype)], axis=1).astype(BF16)
        y_rwkv, y_att, y_gla = _inproj(h, mix_norm_g[layer].reshape(1, D), wr, wa, wgl, cos, sa, sb)
        vres = None if layer == 0 else (rwkv_v0[layer - 1], rwkv_v1[layer - 1], rwkv_v2[layer - 1])
        o_rwkv, v_first = _rwkv_mixer(y_rwkv, S, rwkv_mu[layer], rwkv_w0[layer], rwkv_w2[layer], rwkv_a0[layer],
                                      rwkv_a2[layer], rwkv_g2[layer], rwkv_k_k[layer], rwkv_k_a[layer],
                                      rwkv_r_k[layer], rwkv_ln_g[layer], rwkv_ln_b[layer], v_first, vres)
        o_att = _att_mixer(y_att, S)
        o_gla = _gla_mixer(y_gla, S, gla_gate_up[layer], gla_gate_b[layer], gla_norm_g[layer])
        h = _outproj(o_rwkv, o_att, o_gla, w_out[layer].astype(BF16), h)
        i = layer // 2
        g = ffn_norm_g[layer].reshape(1, D)
        if layer % 2 == 0:
            h = _ffn(h, g, ffn_w_gate[i].astype(BF16), ffn_w_up[i].astype(BF16), ffn_w_down[i].astype(BF16))
        else:
            wrt = jnp.pad(moe_router[i], ((0, 0), (0, LANES - N_EXPERTS)))
            h = _moe(h, g, wrt, moe_w_gate[i].astype(BF16), moe_w_up[i].astype(BF16), moe_w_down[i].astype(BF16),
                     final_norm_g.reshape(1, D), layer == DEPTH - 1)
    return h.reshape(B, S, D)
```

```python
import functools

import jax
import jax.numpy as jnp
from jax import lax
from jax.experimental import pallas as pl
from jax.experimental.pallas import tpu as pltpu

D_MODEL = 1024
DEPTH = 2
HEAD_DIM = 64
RWKV_W = 256
RWKV_H = 4
ATT_W = 384
GLA_W = 384
GLA_H = 6
RWKV_DECAY_RANK = 32
RWKV_AAA_RANK = 32
RWKV_GATE_RANK = 64
RWKV_GN_EPS = 64e-5
GLA_GATE_RANK = 16
GLA_GATE_TAU = 16.0
CHUNK = 64
DILATIONS = (1, 4, 16)
ATT_BLOCK = 128
ROPE_THETA = 500000.0
ROPE_DIMS = 16
ROPE_HALF = 8
FFN_DENSE = 2816
N_EXPERTS = 8
FFN_EXPERT = 3584
RMS_EPS = 1e-5
NEG_INF = -1e30

RWKV_COLS = 3 * RWKV_W + RWKV_DECAY_RANK + RWKV_AAA_RANK + RWKV_GATE_RANK
ATT_COLS = 3 * ATT_W
LANES = 128
GLA_COLS_PAD = 4 * GLA_W + LANES

VMEM_LIMIT = 56 * 1024 * 1024

ROPE_ROWS = 2048
INPROJ_ROWS = 512
RWKV_ROWS = 512
GLA_ROWS = 1024
OUTPROJ_ROWS = 1024
FFN_ROWS = 512
FFN_COLS = FFN_DENSE // 2

F32 = jnp.float32
BF16 = jnp.bfloat16
NN = (((1,), (0,)), ((), ()))
NT = (((1,), (1,)), ((), ()))
TN = (((0,), (0,)), ((), ()))
BNN = (((2,), (1,)), ((0,), (0,)))
BNT = (((2,), (2,)), ((0,), (0,)))
BTN = (((1,), (1,)), ((0,), (0,)))


def _cparams(*sem):
    return pltpu.CompilerParams(dimension_semantics=sem, vmem_limit_bytes=VMEM_LIMIT)


def _rms(x, g):
    return x * lax.rsqrt(jnp.mean(x * x, axis=-1, keepdims=True) + RMS_EPS) * g


def _sigmoid(x):
    return 1.0 / (1.0 + jnp.exp(-x))


def _log_sigmoid(x):
    return jnp.minimum(x, 0.0) - jnp.log(1.0 + jnp.exp(-jnp.abs(x)))


def _split_hi_lo(a):
    hi = a.astype(BF16)
    return hi, (a - hi.astype(F32)).astype(BF16)


def _dot3(a, b, dims=NN):
    a_hi, a_lo = _split_hi_lo(a)
    b_hi, b_lo = _split_hi_lo(b)
    d = lambda x, y: lax.dot_general(x, y, dims, preferred_element_type=F32)
    return d(a_hi, b_hi) + d(a_lo, b_hi) + d(a_hi, b_lo)


def _dot_exact_rhs(x, r_bf16):
    x_hi, x_lo = _split_hi_lo(x)
    return jnp.dot(x_hi, r_bf16, preferred_element_type=F32) + jnp.dot(x_lo, r_bf16, preferred_element_type=F32)


def _bdot(a, b, dims, passes):
    d = lambda x, y: lax.dot_general(x, y, dims, preferred_element_type=F32)
    if passes == 1:
        return d(a.astype(BF16), b.astype(BF16))
    a_hi, a_lo = _split_hi_lo(a)
    b_hi, b_lo = _split_hi_lo(b)
    return d(a_hi, b_hi) + d(a_lo, b_hi) + d(a_hi, b_lo)


def _head_ones(width):
    idx = jnp.arange(width) // HEAD_DIM
    return (idx[:, None] == idx[None, :]).astype(BF16)


def _chunk_selectors():
    t = jnp.arange(CHUNK)
    return (t[None, :] <= t[:, None]).astype(BF16), jnp.ones((CHUNK, CHUNK), BF16)


def _chunk_sums(sel, x):
    nc = x.shape[0] // CHUNK
    hi, lo = _split_hi_lo(x.reshape(nc, CHUNK, x.shape[1]))
    s3 = jnp.broadcast_to(sel[None], (nc, CHUNK, CHUNK))
    d = lambda a, b: lax.dot_general(a, b, BNN, preferred_element_type=F32)
    return (d(s3, hi) + d(s3, lo)).reshape(x.shape)


def _rope_table_kernel(pos_ref, invf_ref, cos_ref, sa_ref, sb_ref):
    ang = pos_ref[...].astype(F32) * invf_ref[...]
    lane = lax.broadcasted_iota(jnp.int32, ang.shape, 1) % HEAD_DIM
    c = jnp.cos(ang)
    s = jnp.sin(ang)
    cos_ref[...] = jnp.where(lane < ROPE_DIMS, c, 1.0)
    sa_ref[...] = jnp.where((lane >= ROPE_HALF) & (lane < ROPE_DIMS), s, 0.0)
    sb_ref[...] = jnp.where(lane < ROPE_HALF, -s, 0.0)


def _rope_tables(positions):
    m = positions.size
    tm = ROPE_ROWS
    lane = jnp.arange(LANES) % ROPE_HALF
    invf = (ROPE_THETA ** (-lane.astype(F32) / ROPE_HALF)).reshape(1, LANES)
    out = jax.ShapeDtypeStruct((m, LANES), F32)
    row = pl.BlockSpec((tm, LANES), lambda i: (i, 0))
    return pl.pallas_call(
        _rope_table_kernel,
        grid=(m // tm,),
        in_specs=[pl.BlockSpec((tm, 1), lambda i: (i, 0)), pl.BlockSpec((1, LANES), lambda i: (0, 0))],
        out_specs=[row, row, row],
        out_shape=[out, out, out],
        compiler_params=_cparams("parallel"),
        name="rope_tables",
    )(positions.reshape(m, 1), invf)


def _inproj_kernel(x_ref, g_ref, wr_ref, wa_ref, wg_ref, cos_ref, sa_ref, sb_ref, yr_ref, ya_ref, yg_ref):
    xb = _rms(x_ref[...], g_ref[...]).astype(BF16)
    yr_ref[...] = jnp.dot(xb, wr_ref[...], preferred_element_type=F32)
    yg_ref[...] = jnp.dot(xb, wg_ref[...], preferred_element_type=F32)
    ya = jnp.dot(xb, wa_ref[...], preferred_element_type=F32)
    cos, sa, sb = cos_ref[...], sa_ref[...], sb_ref[...]
    for j in range(2 * ATT_W // LANES):
        blk = ya[:, j * LANES:(j + 1) * LANES]
        rot = blk * cos + pltpu.roll(blk, ROPE_HALF, 1) * sa + pltpu.roll(blk, LANES - ROPE_HALF, 1) * sb
        if j < ATT_W // LANES:
            rot = rot * HEAD_DIM ** -0.5
        ya_ref[:, j * LANES:(j + 1) * LANES] = rot.astype(ya_ref.dtype)
    ya_ref[:, 2 * ATT_W:] = ya[:, 2 * ATT_W:].astype(ya_ref.dtype)


def _inproj(h, g, wr, wa, wg, cos, sa, sb):
    m = h.shape[0]
    tm = INPROJ_ROWS
    row = lambda w: pl.BlockSpec((tm, w), lambda i: (i, 0))
    full = lambda a: pl.BlockSpec(a.shape, lambda i: (0, 0))
    return pl.pallas_call(
        _inproj_kernel,
        grid=(m // tm,),
        in_specs=[row(D_MODEL), full(g), full(wr), full(wa), full(wg), row(LANES), row(LANES), row(LANES)],
        out_specs=[row(RWKV_COLS), row(ATT_COLS), row(GLA_COLS_PAD)],
        out_shape=[jax.ShapeDtypeStruct((m, RWKV_COLS), F32),
                   jax.ShapeDtypeStruct((m, ATT_COLS), BF16),
                   jax.ShapeDtypeStruct((m, GLA_COLS_PAD), F32)],
        compiler_params=_cparams("parallel"),
        name="inproj",
    )(h, g, wr, wa, wg, cos, sa, sb)


def _rwkv_kernel(*refs, has_vres, tile):
    nc = tile // CHUNK
    if has_vres:
        (y_ref, vf_ref, mu_ref, w0_ref, w2_ref, a0_ref, a2_ref, g2_ref, kk_ref, ka_ref, rk_ref, lng_ref, lnb_ref,
         psel_ref, tsel_ref, hones_ref, v0_ref, v1_ref, v2_ref, o_ref, *scratch) = refs
    else:
        (y_ref, mu_ref, w0_ref, w2_ref, a0_ref, a2_ref, g2_ref, kk_ref, ka_ref, rk_ref, lng_ref, lnb_ref,
         psel_ref, tsel_ref, hones_ref, o_ref, vfo_ref, *scratch) = refs
    state_ref, prev_ref, os_ref = scratch

    @pl.when(pl.program_id(1) == 0)
    def _():
        state_ref[...] = jnp.zeros_like(state_ref)
        prev_ref[...] = jnp.zeros_like(prev_ref)

    y = y_ref[...]
    row = lax.broadcasted_iota(jnp.int32, (tile, 1), 0)
    ysh = jnp.where(row == 0, prev_ref[...], pltpu.roll(y, 1, 0))
    prev_ref[...] = y[tile - 1:tile, :]
    ym = y + (ysh - y) * mu_ref[...]
    r = ym[:, 0:RWKV_W]
    k = ym[:, RWKV_W:2 * RWKV_W]
    v = ym[:, 2 * RWKV_W:3 * RWKV_W]
    x6 = ym[:, 3 * RWKV_W:]
    wl = w0_ref[...] + _dot3(jnp.tanh(x6), w2_ref[...])
    lw = -jnp.exp(_log_sigmoid(wl) - 0.5)
    a = _sigmoid(a0_ref[...] + _dot3(x6, a2_ref[...]))
    g = _dot3(_sigmoid(x6), g2_ref[...])
    if has_vres:
        v = v + (vf_ref[...] - v) * _sigmoid(v0_ref[...] + _dot3(_dot3(v, v1_ref[...]), v2_ref[...]))
    else:
        vfo_ref[...] = v

    head_ones = hones_ref[...]
    kk = k * kk_ref[...]
    kk = kk * lax.rsqrt(jnp.maximum(_dot_exact_rhs(kk * kk, head_ones), 1e-24))
    k2 = k * (1.0 + (a - 1.0) * ka_ref[...])
    bonus = _dot_exact_rhs(r * k2 * rk_ref[...], head_ones) * v

    c = _chunk_sums(psel_ref[...], lw)
    c_last = _chunk_sums(tsel_ref[...], lw)
    b = kk * a
    e_neg = jnp.exp(-c)
    e_end = jnp.exp(c_last - c)

    def units(val):
        return jnp.concatenate([val[:, h * HEAD_DIM:(h + 1) * HEAD_DIM].reshape(nc, CHUNK, HEAD_DIM)
                                for h in range(RWKV_H)], axis=0)

    rt = units(r * jnp.exp(c))
    at = units(-kk * jnp.exp(c - lw))
    bt = units(b * e_neg)
    kt = units(k2 * e_neg)
    bh = units(b * e_end)
    kh = units(k2 * e_end)
    vv = units(v)
    gam = units(jnp.exp(c_last))[:, 0:1, :]

    n = RWKV_H * nc
    qi = lax.broadcasted_iota(jnp.int32, (n, CHUNK, CHUNK), 1)
    qj = lax.broadcasted_iota(jnp.int32, (n, CHUNK, CHUNK), 2)
    strict = qj < qi
    incl = qj <= qi
    eye = qj == qi
    same_block = [qi // size == qj // size for size in (2, 4, 8, 16, 32, CHUNK)]
    a_ab = jnp.where(strict, _bdot(at, bt, BNT, 1), 0.0)
    a_ak = jnp.where(strict, _bdot(at, kt, BNT, 1), 0.0)
    p_b = jnp.where(incl, _bdot(rt, bt, BNT, 1), 0.0)
    p_k = jnp.where(incl, _bdot(rt, kt, BNT, 1), 0.0)
    t = jnp.where(eye, 1.0, jnp.where(same_block[0], a_ab, 0.0))
    for small, big in zip(same_block[:-1], same_block[1:]):
        t = t + _bdot(_bdot(t, jnp.where(big & ~small, a_ab, 0.0), BNN, 1), t, BNN, 1)
    w = _bdot(t, at, BNN, 1)
    u_loc = _bdot(t, _bdot(a_ak, vv, BNN, 1), BNN, 1)
    q_t = rt + _bdot(p_b, w, BNN, 1)
    o_loc = _bdot(p_b, u_loc, BNN, 1) + _bdot(p_k, vv, BNN, 1)
    per_chunk = lambda x: x.reshape(RWKV_H, nc, CHUNK, HEAD_DIM)
    g_mat = per_chunk(_bdot(w, bh, BTN, 1) + jnp.where(eye, gam, 0.0))
    h_mat = per_chunk(_bdot(u_loc, bh, BTN, 1) + _bdot(vv, kh, BTN, 1))

    s = state_ref[...]
    before = []
    for ci in range(nc):
        before.append(s)
        s = _bdot(s, g_mat[:, ci], BNN, 3) + h_mat[:, ci]
    state_ref[...] = s

    o = _bdot(q_t, jnp.stack(before, axis=1).reshape(n, HEAD_DIM, HEAD_DIM), BNT, 3) + o_loc
    for h in range(RWKV_H):
        os_ref[:, h * HEAD_DIM:(h + 1) * HEAD_DIM] = o[h * nc:(h + 1) * nc].reshape(tile, HEAD_DIM)

    o = os_ref[...]
    mean = _dot_exact_rhs(o, head_ones) * (1.0 / HEAD_DIM)
    oc = o - mean
    var = _dot_exact_rhs(oc * oc, head_ones) * (1.0 / HEAD_DIM)
    on = oc * lax.rsqrt(var + RWKV_GN_EPS) * lng_ref[...] + lnb_ref[...]
    o_ref[...] = (on + bonus) * g


def _rwkv_mixer(y, seq_len, mu, w0, w2, a0, a2, g2, k_k, k_a, r_k, ln_g, ln_b, v_first, vres):
    m = y.shape[0]
    tile = RWKV_ROWS
    nt = seq_len // tile
    has_vres = vres is not None
    row = lambda wd: pl.BlockSpec((tile, wd), lambda b, j: (b * nt + j, 0))
    full = lambda arr: pl.BlockSpec(arr.shape, lambda b, j: (0, 0))
    vec = lambda t: t.reshape(1, -1)
    pad_rows = lambda w, start: jnp.zeros((LANES, RWKV_W), F32).at[start:start + w.shape[0]].set(w)
    consts = [vec(mu), vec(w0), pad_rows(w2, 0), vec(a0), pad_rows(a2, RWKV_DECAY_RANK),
              pad_rows(g2, RWKV_DECAY_RANK + RWKV_AAA_RANK), vec(k_k), vec(k_a), vec(r_k), vec(ln_g), vec(ln_b),
              *_chunk_selectors(), _head_ones(RWKV_W)]
    args = [y] + ([v_first] if has_vres else []) + consts
    in_specs = [row(RWKV_COLS)] + ([row(RWKV_W)] if has_vres else []) + [full(c) for c in consts]
    out_shape = [jax.ShapeDtypeStruct((m, RWKV_W), F32)]
    out_specs = [row(RWKV_W)]
    if has_vres:
        v0, v1, v2 = vres
        extra = [vec(v0), jnp.pad(v1, ((0, 0), (0, LANES - v1.shape[1]))),
                 jnp.pad(v2, ((0, LANES - v2.shape[0]), (0, 0)))]
        args += extra
        in_specs += [full(c) for c in extra]
    else:
        out_shape.append(jax.ShapeDtypeStruct((m, RWKV_W), F32))
        out_specs.append(row(RWKV_W))
    outs = pl.pallas_call(
        functools.partial(_rwkv_kernel, has_vres=has_vres, tile=tile),
        grid=(m // seq_len, nt),
        in_specs=in_specs, out_specs=out_specs, out_shape=out_shape,
        scratch_shapes=[pltpu.VMEM((RWKV_H, HEAD_DIM, HEAD_DIM), F32), pltpu.VMEM((1, RWKV_COLS), F32),
                        pltpu.VMEM((tile, RWKV_W), F32)],
        compiler_params=_cparams("parallel", "arbitrary"),
        name="rwkv7",
    )(*args)
    return (outs[0], v_first) if has_vres else (outs[0], outs[1])


ATT_PAD = ATT_BLOCK * max(DILATIONS)
ATT_UNROLL = 4


def _att_kernel(q_ref, k_ref, v_ref, o_ref, qs_ref, ks_ref, vs_ref, m_ref, n_ref, d_ref, *, seq_len):
    qs_ref[...] = q_ref[...].astype(F32)
    ks_ref[0:ATT_PAD, :] = jnp.zeros((ATT_PAD, LANES), F32)
    vs_ref[0:ATT_PAD, :] = jnp.zeros((ATT_PAD, LANES), F32)
    ks_ref[ATT_PAD:, :] = k_ref[...].astype(F32)
    vs_ref[ATT_PAD:, :] = v_ref[...].astype(F32)

    qi = lax.broadcasted_iota(jnp.int32, (ATT_BLOCK, 2 * ATT_BLOCK), 0)
    kc = lax.broadcasted_iota(jnp.int32, (ATT_BLOCK, 2 * ATT_BLOCK), 1)
    band = (kc >= qi) & (kc <= qi + ATT_BLOCK)
    head0 = lax.broadcasted_iota(jnp.int32, (ATT_BLOCK, LANES), 1) < HEAD_DIM

    def block(g, dil, first, q, k, v, rows):
        valid = band & ((kc >= ATT_BLOCK) | jnp.logical_not(first))
        v1 = jnp.concatenate([v, jnp.ones_like(v)], axis=1)
        mxs, pvs = [], []
        for hh in range(2):
            qh = jnp.where(head0 if hh == 0 else jnp.logical_not(head0), q, jnp.zeros_like(q))
            s = jnp.where(valid, lax.dot_general(qh, k, NT, preferred_element_type=F32), NEG_INF)
            mx = jnp.max(s, axis=-1, keepdims=True)
            mxs.append(mx)
            pvs.append(jnp.dot(jnp.exp(s - mx).astype(BF16), v1, preferred_element_type=F32))
        mx = jnp.where(head0, mxs[0], mxs[1])
        pv = jnp.where(head0, pvs[0][:, :LANES], pvs[1][:, :LANES])
        den = jnp.where(head0, pvs[0][:, LANES:], pvs[1][:, LANES:])
        if g > 0:
            m_old = m_ref[rows, :]
            m_new = jnp.maximum(m_old, mx)
            a_old = jnp.exp(m_old - m_new)
            a_new = jnp.exp(mx - m_new)
            pv = a_old * n_ref[rows, :] + a_new * pv
            den = a_old * d_ref[rows, :] + a_new * den
            mx = m_new
        if g == len(DILATIONS) - 1:
            o_ref[rows, :] = pv / den
        else:
            m_ref[rows, :] = mx
            n_ref[rows, :] = pv
            d_ref[rows, :] = den

    for g, dil in enumerate(DILATIONS):
        npairs = seq_len // dil // ATT_BLOCK // 2
        step = ATT_BLOCK * dil

        def body(it, carry, g=g, dil=dil, npairs=npairs, step=step):
            r = it // npairs
            jp = it % npairs
            start = r + jp * (2 * step)
            q = qs_ref[pl.ds(start, 2 * ATT_BLOCK, stride=dil), :].astype(BF16)
            k = ks_ref[pl.ds(start + (ATT_PAD - step), 3 * ATT_BLOCK, stride=dil), :].astype(BF16)
            v = vs_ref[pl.ds(start + (ATT_PAD - step), 3 * ATT_BLOCK, stride=dil), :].astype(BF16)
            for u in range(2):
                block(g, dil, (jp == 0) if u == 0 else False, q[u * ATT_BLOCK:(u + 1) * ATT_BLOCK],
                      k[u * ATT_BLOCK:(u + 2) * ATT_BLOCK], v[u * ATT_BLOCK:(u + 2) * ATT_BLOCK],
                      pl.ds(start + u * step, ATT_BLOCK, stride=dil))
            return carry

        lax.fori_loop(0, dil * npairs, body, 0, unroll=ATT_UNROLL)


def _att_mixer(y_att, seq_len):
    m = y_att.shape[0]
    npair = ATT_W // LANES
    col = lambda c: pl.BlockSpec((seq_len, LANES), lambda b, hp: (b, c * npair + hp))
    seq_buf = pltpu.VMEM((seq_len, LANES), F32)
    pad_buf = pltpu.VMEM((ATT_PAD + seq_len, LANES), F32)
    return pl.pallas_call(
        functools.partial(_att_kernel, seq_len=seq_len),
        grid=(m // seq_len, npair),
        in_specs=[col(0), col(1), col(2)],
        out_specs=pl.BlockSpec((seq_len, LANES), lambda b, hp: (b, hp)),
        out_shape=jax.ShapeDtypeStruct((m, ATT_W), F32),
        scratch_shapes=[seq_buf, pad_buf, pad_buf, seq_buf, seq_buf, seq_buf],
        compiler_params=_cparams("parallel", "parallel"),
        name="dilated_att",
    )(y_att, y_att, y_att)


def _gla_kernel(y_ref, gu_ref, gb_ref, ng_ref, psel_ref, tsel_ref, hones_ref, o_ref, state_ref, os_ref, *, tile):
    nc = tile // CHUNK
    n = GLA_H * nc

    @pl.when(pl.program_id(1) == 0)
    def _():
        state_ref[...] = jnp.zeros_like(state_ref)

    def units(val):
        return jnp.concatenate([val[:, h * HEAD_DIM:(h + 1) * HEAD_DIM].reshape(nc, CHUNK, HEAD_DIM)
                                for h in range(GLA_H)], axis=0)

    q = y_ref[:, 0:GLA_W] * HEAD_DIM ** -0.5
    k = y_ref[:, GLA_W:2 * GLA_W]
    og = y_ref[:, 3 * GLA_W:4 * GLA_W]
    lg = _log_sigmoid(_dot3(y_ref[:, 4 * GLA_W:], gu_ref[...]) + gb_ref[...]) * (1.0 / GLA_GATE_TAU)
    b = _chunk_sums(psel_ref[...], lg)
    b_last = _chunk_sums(tsel_ref[...], lg)
    q_in = units((q * jnp.exp(b)).astype(BF16))
    k_in = units((k * jnp.exp(-b)).astype(BF16))
    k_end = units((k * jnp.exp(b_last - b)).astype(BF16))
    v = units(y_ref[:, 2 * GLA_W:3 * GLA_W].astype(BF16))
    gam = units(jnp.exp(b_last))[:, 0:1, :].reshape(GLA_H, nc, 1, HEAD_DIM)

    ci = lax.broadcasted_iota(jnp.int32, (n, CHUNK, CHUNK), 1)
    cj = lax.broadcasted_iota(jnp.int32, (n, CHUNK, CHUNK), 2)
    d = lambda a, bb, dims: lax.dot_general(a, bb, dims, preferred_element_type=F32)
    att = jnp.where(cj <= ci, d(q_in, k_in, BNT), 0.0)
    o = d(att.astype(BF16), v, BNN)
    kv = d(v, k_end, BTN).reshape(GLA_H, nc, HEAD_DIM, HEAD_DIM)

    s = state_ref[...]
    before = []
    for c in range(nc):
        before.append(s)
        s = s * gam[:, c] + kv[:, c]
    state_ref[...] = s
    s_before = jnp.stack(before, axis=1).reshape(n, HEAD_DIM, HEAD_DIM)
    o = o + d(q_in, s_before.astype(BF16), BNT)
    for h in range(GLA_H):
        os_ref[:, h * HEAD_DIM:(h + 1) * HEAD_DIM] = o[h * nc:(h + 1) * nc].reshape(tile, HEAD_DIM)

    o = os_ref[...]
    ms = _dot_exact_rhs(o * o, hones_ref[...]) * (1.0 / HEAD_DIM)
    o_ref[...] = o * lax.rsqrt(ms + RMS_EPS) * ng_ref[...] * (og * _sigmoid(og))


def _gla_mixer(y, seq_len, gate_up, gate_b, norm_g):
    m = y.shape[0]
    tile = GLA_ROWS
    nt = seq_len // tile
    gu = jnp.pad(gate_up, ((0, LANES - gate_up.shape[0]), (0, 0)))
    gb = gate_b.reshape(1, GLA_W)
    ng = jnp.tile(norm_g, GLA_H).reshape(1, GLA_W)
    consts = [gu, gb, ng, *_chunk_selectors(), _head_ones(GLA_W)]
    full = lambda a: pl.BlockSpec(a.shape, lambda b, j: (0, 0))
    return pl.pallas_call(
        functools.partial(_gla_kernel, tile=tile),
        grid=(m // seq_len, nt),
        in_specs=[pl.BlockSpec((tile, GLA_COLS_PAD), lambda b, j: (b * nt + j, 0))] + [full(c) for c in consts],
        out_specs=pl.BlockSpec((tile, GLA_W), lambda b, j: (b * nt + j, 0)),
        out_shape=jax.ShapeDtypeStruct((m, GLA_W), F32),
        scratch_shapes=[pltpu.VMEM((GLA_H, HEAD_DIM, HEAD_DIM), F32), pltpu.VMEM((tile, GLA_W), F32)],
        compiler_params=_cparams("parallel", "arbitrary"),
        name="gla",
    )(y, *consts)


def _outproj_kernel(or_ref, oa_ref, og_ref, w_ref, h_ref, o_ref):
    acc = jnp.dot(or_ref[...].astype(BF16), w_ref[0:RWKV_W, :], preferred_element_type=F32)
    acc += jnp.dot(oa_ref[...].astype(BF16), w_ref[RWKV_W:RWKV_W + ATT_W, :], preferred_element_type=F32)
    acc += jnp.dot(og_ref[...].astype(BF16), w_ref[RWKV_W + ATT_W:, :], preferred_element_type=F32)
    o_ref[...] = h_ref[...] + acc


def _outproj(o_rwkv, o_att, o_gla, w, h):
    m = h.shape[0]
    tm = OUTPROJ_ROWS
    row = lambda wd: pl.BlockSpec((tm, wd), lambda i: (i, 0))
    return pl.pallas_call(
        _outproj_kernel,
        grid=(m // tm,),
        in_specs=[row(RWKV_W), row(ATT_W), row(GLA_W), pl.BlockSpec(w.shape, lambda i: (0, 0)), row(D_MODEL)],
        out_specs=row(D_MODEL),
        out_shape=jax.ShapeDtypeStruct(h.shape, F32),
        compiler_params=_cparams("parallel"),
        name="outproj",
    )(o_rwkv, o_att, o_gla, w, h)


def _ffn_kernel(x_ref, g_ref, wg_ref, wu_ref, wd_ref, o_ref, xn_ref, acc_ref):
    f = pl.program_id(1)

    @pl.when(f == 0)
    def _():
        xn_ref[...] = _rms(x_ref[...], g_ref[...]).astype(xn_ref.dtype)
        acc_ref[...] = jnp.zeros_like(acc_ref)

    xn = xn_ref[...]
    gate = jnp.dot(xn, wg_ref[...], preferred_element_type=F32)
    up = jnp.dot(xn, wu_ref[...], preferred_element_type=F32)
    act = (gate * _sigmoid(gate) * up).astype(BF16)
    acc_ref[...] += jnp.dot(act, wd_ref[...], preferred_element_type=F32)

    @pl.when(f == pl.num_programs(1) - 1)
    def _():
        o_ref[...] = x_ref[...] + acc_ref[...]


def _ffn(h, g, wg, wu, wd):
    m = h.shape[0]
    tm, tf = FFN_ROWS, FFN_COLS
    row = pl.BlockSpec((tm, D_MODEL), lambda i, f: (i, 0))
    return pl.pallas_call(
        _ffn_kernel,
        grid=(m // tm, FFN_DENSE // tf),
        in_specs=[row, pl.BlockSpec(g.shape, lambda i, f: (0, 0)),
                  pl.BlockSpec((D_MODEL, tf), lambda i, f: (0, f)),
                  pl.BlockSpec((D_MODEL, tf), lambda i, f: (0, f)),
                  pl.BlockSpec((tf, D_MODEL), lambda i, f: (f, 0))],
        out_specs=row,
        out_shape=jax.ShapeDtypeStruct(h.shape, F32),
        scratch_shapes=[pltpu.VMEM((tm, D_MODEL), BF16), pltpu.VMEM((tm, D_MODEL), F32)],
        compiler_params=_cparams("parallel", "arbitrary"),
        name="ffn_dense",
    )(h, g, wg, wu, wd)


MOE_BLOCK = 896
MOE_SUB = 256
MOE_FT = 896


def _router_kernel(x_ref, g_ref, wr_ref, tri_ref, xnt_ref, rank_ref, gate_ref, cnt_ref, *, n_tokens):
    row = lax.broadcasted_iota(jnp.int32, (MOE_BLOCK, 1), 0) + pl.program_id(0) * MOE_BLOCK
    real = row < n_tokens
    xn = jnp.where(real, _rms(x_ref[...], g_ref[...]), 0.0)
    xnt_ref[...] = xn.T.astype(BF16)
    logits = jnp.dot(xn, wr_ref[...], preferred_element_type=F32, precision=lax.Precision.HIGHEST)
    lane = lax.broadcasted_iota(jnp.int32, logits.shape, 1)
    lg = jnp.where(lane < N_EXPERTS, logits, NEG_INF)
    m1 = jnp.max(lg, axis=-1, keepdims=True)
    i1 = jnp.min(jnp.where(lg == m1, lane, LANES), axis=-1, keepdims=True)
    lg2 = jnp.where(lane == i1, NEG_INF, lg)
    m2 = jnp.max(lg2, axis=-1, keepdims=True)
    i2 = jnp.min(jnp.where(lg2 == m2, lane, LANES), axis=-1, keepdims=True)
    e2 = jnp.exp(m2 - m1)
    hot1 = (lane == i1) & real
    hot2 = (lane == i2) & real
    onehot = (hot1 | hot2).astype(F32)
    rank = jnp.dot(tri_ref[...], onehot.astype(BF16), preferred_element_type=F32)
    rank = jnp.where(hot1 | hot2, rank, -1.0)
    gate = jnp.where(hot1, 1.0 / (1.0 + e2), jnp.where(hot2, e2 / (1.0 + e2), 0.0))
    rank_ref[0] = rank.T[:N_EXPERTS]
    gate_ref[0] = gate.T[:N_EXPERTS]
    cnt_ref[0] = jnp.broadcast_to(jnp.sum(onehot, axis=0, keepdims=True), (8, LANES)).astype(jnp.int32)


def _route(h, g, w_router):
    m = h.shape[0]
    sb = MOE_BLOCK
    nblk = pl.cdiv(m, sb)
    t = jnp.arange(sb)
    tri = (t[None, :] < t[:, None]).astype(BF16)
    full = lambda a: pl.BlockSpec(a.shape, lambda i: (0,) * a.ndim)
    per_expert = pl.BlockSpec((1, N_EXPERTS, sb), lambda i: (i, 0, 0))
    return pl.pallas_call(
        functools.partial(_router_kernel, n_tokens=m),
        grid=(nblk,),
        in_specs=[pl.BlockSpec((sb, D_MODEL), lambda i: (i, 0)), full(g), full(w_router), full(tri)],
        out_specs=[pl.BlockSpec((D_MODEL, sb), lambda i: (0, i)), per_expert, per_expert,
                   pl.BlockSpec((1, 8, LANES), lambda i: (i, 0, 0))],
        out_shape=[jax.ShapeDtypeStruct((D_MODEL, nblk * sb), BF16), jax.ShapeDtypeStruct((nblk, N_EXPERTS, sb), F32),
                   jax.ShapeDtypeStruct((nblk, N_EXPERTS, sb), F32),
                   jax.ShapeDtypeStruct((nblk, 8, LANES), jnp.int32)],
        compiler_params=_cparams("parallel"),
        name="moe_router",
    )(h, g, w_router, tri)


def _moe_kernel(cnt_ref, x_ref, xnt_ref, rank_ref, gate_ref, wg_ref, wu_ref, wd_ref, fg_ref, o_ref,
                xs_ref, y_ref, acc_ref, *, final_norm):
    i, e, f = pl.program_id(0), pl.program_id(1), pl.program_id(2)
    last_f = f == pl.num_programs(2) - 1
    n_sub = (cnt_ref[i * N_EXPERTS + e] + (MOE_SUB - 1)) // MOE_SUB

    @pl.when((e == 0) & (f == 0))
    def _():
        acc_ref[...] = jnp.zeros_like(acc_ref)

    def selection(j):
        want = lax.broadcasted_iota(jnp.int32, (MOE_SUB, MOE_BLOCK), 0) + j * MOE_SUB
        return (rank_ref[0, pl.ds(e, 1), :] == want.astype(F32)).astype(BF16)

    def sub_body(j, carry):
        @pl.when(f == 0)
        def _():
            xs_ref[j] = lax.dot_general(xnt_ref[...], selection(j), NT, preferred_element_type=F32).astype(BF16)
            y_ref[j] = jnp.zeros((D_MODEL, MOE_SUB), F32)

        xs = xs_ref[j]
        gate = lax.dot_general(wg_ref[0], xs, TN, preferred_element_type=F32)
        up = lax.dot_general(wu_ref[0], xs, TN, preferred_element_type=F32)
        act = (gate * _sigmoid(gate) * up).astype(BF16)
        y_ref[j] += lax.dot_general(wd_ref[0], act, TN, preferred_element_type=F32)

        @pl.when(last_f)
        def _():
            back = jnp.dot(y_ref[j].astype(BF16), selection(j), preferred_element_type=F32)
            acc_ref[...] += gate_ref[0, pl.ds(e, 1), :] * back

        return carry

    lax.fori_loop(0, n_sub, sub_body, 0)

    @pl.when((e == N_EXPERTS - 1) & last_f)
    def _():
        y = x_ref[...] + acc_ref[...].T
        o_ref[...] = _rms(y, fg_ref[...]) if final_norm else y


def _moe(h, g, w_router, wg, wu, wd, final_g, final_norm):
    m = h.shape[0]
    sb, ft = MOE_BLOCK, MOE_FT
    xnt, rank, gate, cnt = _route(h, g, w_router)
    counts = cnt[:, 0, :N_EXPERTS].reshape(-1)
    row = pl.BlockSpec((sb, D_MODEL), lambda i, e, f, c: (i, 0))
    per_expert = pl.BlockSpec((1, N_EXPERTS, sb), lambda i, e, f, c: (i, 0, 0))
    grid_spec = pltpu.PrefetchScalarGridSpec(
        num_scalar_prefetch=1,
        grid=(pl.cdiv(m, sb), N_EXPERTS, FFN_EXPERT // ft),
        in_specs=[row, pl.BlockSpec((D_MODEL, sb), lambda i, e, f, c: (0, i)), per_expert, per_expert,
                  pl.BlockSpec((1, D_MODEL, ft), lambda i, e, f, c: (e, 0, f)),
                  pl.BlockSpec((1, D_MODEL, ft), lambda i, e, f, c: (e, 0, f)),
                  pl.BlockSpec((1, ft, D_MODEL), lambda i, e, f, c: (e, f, 0)),
                  pl.BlockSpec((1, D_MODEL), lambda i, e, f, c: (0, 0))],
        out_specs=row,
        scratch_shapes=[pltpu.VMEM((pl.cdiv(sb, MOE_SUB), D_MODEL, MOE_SUB), BF16),
                        pltpu.VMEM((pl.cdiv(sb, MOE_SUB), D_MODEL, MOE_SUB), F32),
                        pltpu.VMEM((D_MODEL, sb), F32)],
    )
    return pl.pallas_call(
        functools.partial(_moe_kernel, final_norm=final_norm),
        grid_spec=grid_spec,
        out_shape=jax.ShapeDtypeStruct(h.shape, F32),
        compiler_params=_cparams("parallel", "arbitrary", "arbitrary"),
        name="moe",
    )(counts, h, xnt, rank, gate, wg, wu, wd, final_g)


def kernel(x, positions, mix_norm_g, w_in, rwkv_mu, rwkv_w0, rwkv_w2, rwkv_a0, rwkv_a2, rwkv_g2, rwkv_k_k, rwkv_k_a, rwkv_r_k, rwkv_ln_g, rwkv_ln_b, rwkv_v0, rwkv_v1, rwkv_v2, gla_gate_up, gla_gate_b, gla_norm_g, w_out, ffn_norm_g, ffn_w_gate, ffn_w_up, ffn_w_down, moe_router, moe_w_gate, moe_w_up, moe_w_down, final_norm_g):
    B, S, D = x.shape
    M = B * S
    h = x.reshape(M, D)
    cos, sa, sb = _rope_tables(positions)
    v_first = None
    for layer in range(DEPTH):
        w = w_in[layer]
        wr = w[:, :RWKV_COLS].astype(BF16)
        wa = w[:, RWKV_COLS:RWKV_COLS + ATT_COLS].astype(BF16)
        wgl = w[:, RWKV_COLS + ATT_COLS:]
        wgl = jnp.concatenate([wgl[:, :3 * GLA_W], wgl[:, 3 * GLA_W + GLA_GATE_RANK:],
                               wgl[:, 3 * GLA_W:3 * GLA_W + GLA_GATE_RANK],
                               jnp.zeros((D, LANES - GLA_GATE_RANK), w.dtype)], axis=1).astype(BF16)
        y_rwkv, y_att, y_gla = _inproj(h, mix_norm_g[layer].reshape(1, D), wr, wa, wgl, cos, sa, sb)
        vres = None if layer == 0 else (rwkv_v0[layer - 1], rwkv_v1[layer - 1], rwkv_v2[layer - 1])
        o_rwkv, v_first = _rwkv_mixer(y_rwkv, S, rwkv_mu[layer], rwkv_w0[layer], rwkv_w2[layer], rwkv_a0[layer],
                                      rwkv_a2[layer], rwkv_g2[layer], rwkv_k_k[layer], rwkv_k_a[layer],
                                      rwkv_r_k[layer], rwkv_ln_g[layer], rwkv_ln_b[layer], v_first, vres)
        o_att = _att_mixer(y_att, S)
        o_gla = _gla_mixer(y_gla, S, gla_gate_up[layer], gla_gate_b[layer], gla_norm_g[layer])
        h = _outproj(o_rwkv, o_att, o_gla, w_out[layer].astype(BF16), h)
        i = layer // 2
        g = ffn_norm_g[layer].reshape(1, D)
        if layer % 2 == 0:
            h = _ffn(h, g, ffn_w_gate[i].astype(BF16), ffn_w_up[i].astype(BF16), ffn_w_down[i].astype(BF16))
        else:
            wrt = jnp.pad(moe_router[i], ((0, 0), (0, LANES - N_EXPERTS)))
            h = _moe(h, g, wrt, moe_w_gate[i].astype(BF16), moe_w_up[i].astype(BF16), moe_w_down[i].astype(BF16),
                     final_norm_g.reshape(1, D), layer == DEPTH - 1)
    return h.reshape(B, S, D)
```

```python
import functools

import jax
import jax.numpy as jnp
from jax import lax
from jax.experimental import pallas as pl
from jax.experimental.pallas import tpu as pltpu

D_MODEL = 1024
DEPTH = 2
HEAD_DIM = 64
RWKV_W = 256
RWKV_H = 4
ATT_W = 384
GLA_W = 384
GLA_H = 6
RWKV_DECAY_RANK = 32
RWKV_AAA_RANK = 32
RWKV_GATE_RANK = 64
RWKV_GN_EPS = 64e-5
GLA_GATE_RANK = 16
GLA_GATE_TAU = 16.0
CHUNK = 64
DILATIONS = (1, 4, 16)
ATT_BLOCK = 128
ROPE_THETA = 500000.0
ROPE_DIMS = 16
ROPE_HALF = 8
FFN_DENSE = 2816
N_EXPERTS = 8
FFN_EXPERT = 3584
RMS_EPS = 1e-5
NEG_INF = -1e30

RWKV_COLS = 3 * RWKV_W + RWKV_DECAY_RANK + RWKV_AAA_RANK + RWKV_GATE_RANK
ATT_COLS = 3 * ATT_W
LANES = 128
GLA_COLS_PAD = 4 * GLA_W + LANES

VMEM_LIMIT = 56 * 1024 * 1024

ROPE_ROWS = 2048
INPROJ_ROWS = 1024
RWKV_ROWS = 512
GLA_ROWS = 1024
OUTPROJ_ROWS = 1024
FFN_ROWS = 1024
FFN_COLS = FFN_DENSE // 2

F32 = jnp.float32
BF16 = jnp.bfloat16
NN = (((1,), (0,)), ((), ()))
NT = (((1,), (1,)), ((), ()))
TN = (((0,), (0,)), ((), ()))
BNN = (((2,), (1,)), ((0,), (0,)))
BNT = (((2,), (2,)), ((0,), (0,)))
BTN = (((1,), (1,)), ((0,), (0,)))


def _cparams(*sem):
    return pltpu.CompilerParams(dimension_semantics=sem, vmem_limit_bytes=VMEM_LIMIT)


def _rms(x, g):
    return x * lax.rsqrt(jnp.mean(x * x, axis=-1, keepdims=True) + RMS_EPS) * g


def _sigmoid(x):
    return 1.0 / (1.0 + jnp.exp(-x))


def _log_sigmoid(x):
    return jnp.minimum(x, 0.0) - jnp.log(1.0 + jnp.exp(-jnp.abs(x)))


def _split_hi_lo(a):
    hi = a.astype(BF16)
    return hi, (a - hi.astype(F32)).astype(BF16)


def _dot3(a, b, dims=NN):
    a_hi, a_lo = _split_hi_lo(a)
    b_hi, b_lo = _split_hi_lo(b)
    d = lambda x, y: lax.dot_general(x, y, dims, preferred_element_type=F32)
    return d(a_hi, b_hi) + d(a_lo, b_hi) + d(a_hi, b_lo)


def _dot_exact_rhs(x, r_bf16):
    x_hi, x_lo = _split_hi_lo(x)
    return jnp.dot(x_hi, r_bf16, preferred_element_type=F32) + jnp.dot(x_lo, r_bf16, preferred_element_type=F32)


def _bdot(a, b, dims, passes):
    d = lambda x, y: lax.dot_general(x, y, dims, preferred_element_type=F32)
    if passes == 1:
        return d(a.astype(BF16), b.astype(BF16))
    a_hi, a_lo = _split_hi_lo(a)
    b_hi, b_lo = _split_hi_lo(b)
    return d(a_hi, b_hi) + d(a_lo, b_hi) + d(a_hi, b_lo)


def _head_ones(width):
    idx = jnp.arange(width) // HEAD_DIM
    return (idx[:, None] == idx[None, :]).astype(BF16)


def _chunk_selectors():
    t = jnp.arange(CHUNK)
    return (t[None, :] <= t[:, None]).astype(BF16), jnp.ones((CHUNK, CHUNK), BF16)


def _chunk_sums(sel, x):
    nc = x.shape[0] // CHUNK
    hi, lo = _split_hi_lo(x.reshape(nc, CHUNK, x.shape[1]))
    s3 = jnp.broadcast_to(sel[None], (nc, CHUNK, CHUNK))
    d = lambda a, b: lax.dot_general(a, b, BNN, preferred_element_type=F32)
    return (d(s3, hi) + d(s3, lo)).reshape(x.shape)


def _rope_table_kernel(pos_ref, invf_ref, cos_ref, sa_ref, sb_ref):
    ang = pos_ref[...].astype(F32) * invf_ref[...]
    lane = lax.broadcasted_iota(jnp.int32, ang.shape, 1) % HEAD_DIM
    c = jnp.cos(ang)
    s = jnp.sin(ang)
    cos_ref[...] = jnp.where(lane < ROPE_DIMS, c, 1.0)
    sa_ref[...] = jnp.where((lane >= ROPE_HALF) & (lane < ROPE_DIMS), s, 0.0)
    sb_ref[...] = jnp.where(lane < ROPE_HALF, -s, 0.0)


def _rope_tables(positions):
    m = positions.size
    tm = ROPE_ROWS
    lane = jnp.arange(LANES) % ROPE_HALF
    invf = (ROPE_THETA ** (-lane.astype(F32) / ROPE_HALF)).reshape(1, LANES)
    out = jax.ShapeDtypeStruct((m, LANES), F32)
    row = pl.BlockSpec((tm, LANES), lambda i: (i, 0))
    return pl.pallas_call(
        _rope_table_kernel,
        grid=(m // tm,),
        in_specs=[pl.BlockSpec((tm, 1), lambda i: (i, 0)), pl.BlockSpec((1, LANES), lambda i: (0, 0))],
        out_specs=[row, row, row],
        out_shape=[out, out, out],
        compiler_params=_cparams("parallel"),
        name="rope_tables",
    )(positions.reshape(m, 1), invf)


def _inproj_kernel(x_ref, g_ref, wr_ref, wa_ref, wg_ref, cos_ref, sa_ref, sb_ref, yr_ref, ya_ref, yg_ref):
    xb = _rms(x_ref[...], g_ref[...]).astype(BF16)
    yr_ref[...] = jnp.dot(xb, wr_ref[...], preferred_element_type=F32)
    yg_ref[...] = jnp.dot(xb, wg_ref[...], preferred_element_type=F32)
    ya = jnp.dot(xb, wa_ref[...], preferred_element_type=F32)
    cos, sa, sb = cos_ref[...], sa_ref[...], sb_ref[...]
    for j in range(2 * ATT_W // LANES):
        blk = ya[:, j * LANES:(j + 1) * LANES]
        rot = blk * cos + pltpu.roll(blk, ROPE_HALF, 1) * sa + pltpu.roll(blk, LANES - ROPE_HALF, 1) * sb
        if j < ATT_W // LANES:
            rot = rot * HEAD_DIM ** -0.5
        ya_ref[:, j * LANES:(j + 1) * LANES] = rot.astype(ya_ref.dtype)
    ya_ref[:, 2 * ATT_W:] = ya[:, 2 * ATT_W:].astype(ya_ref.dtype)


def _inproj(h, g, wr, wa, wg, cos, sa, sb):
    m = h.shape[0]
    tm = INPROJ_ROWS
    row = lambda w: pl.BlockSpec((tm, w), lambda i: (i, 0))
    full = lambda a: pl.BlockSpec(a.shape, lambda i: (0, 0))
    return pl.pallas_call(
        _inproj_kernel,
        grid=(m // tm,),
        in_specs=[row(D_MODEL), full(g), full(wr), full(wa), full(wg), row(LANES), row(LANES), row(LANES)],
        out_specs=[row(RWKV_COLS), row(ATT_COLS), row(GLA_COLS_PAD)],
        out_shape=[jax.ShapeDtypeStruct((m, RWKV_COLS), F32),
                   jax.ShapeDtypeStruct((m, ATT_COLS), BF16),
                   jax.ShapeDtypeStruct((m, GLA_COLS_PAD), F32)],
        compiler_params=_cparams("parallel"),
        name="inproj",
    )(h, g, wr, wa, wg, cos, sa, sb)


def _rwkv_kernel(*refs, has_vres, tile):
    nc = tile // CHUNK
    if has_vres:
        (y_ref, vf_ref, mu_ref, w0_ref, w2_ref, a0_ref, a2_ref, g2_ref, kk_ref, ka_ref, rk_ref, lng_ref, lnb_ref,
         psel_ref, tsel_ref, hones_ref, v0_ref, v1_ref, v2_ref, o_ref, *scratch) = refs
    else:
        (y_ref, mu_ref, w0_ref, w2_ref, a0_ref, a2_ref, g2_ref, kk_ref, ka_ref, rk_ref, lng_ref, lnb_ref,
         psel_ref, tsel_ref, hones_ref, o_ref, vfo_ref, *scratch) = refs
    state_ref, prev_ref, os_ref = scratch

    @pl.when(pl.program_id(1) == 0)
    def _():
        state_ref[...] = jnp.zeros_like(state_ref)
        prev_ref[...] = jnp.zeros_like(prev_ref)

    y = y_ref[...]
    row = lax.broadcasted_iota(jnp.int32, (tile, 1), 0)
    ysh = jnp.where(row == 0, prev_ref[...], pltpu.roll(y, 1, 0))
    prev_ref[...] = y[tile - 1:tile, :]
    ym = y + (ysh - y) * mu_ref[...]
    r = ym[:, 0:RWKV_W]
    k = ym[:, RWKV_W:2 * RWKV_W]
    v = ym[:, 2 * RWKV_W:3 * RWKV_W]
    x6 = ym[:, 3 * RWKV_W:]
    wl = w0_ref[...] + _dot3(jnp.tanh(x6), w2_ref[...])
    lw = -jnp.exp(_log_sigmoid(wl) - 0.5)
    a = _sigmoid(a0_ref[...] + _dot3(x6, a2_ref[...]))
    g = _dot3(_sigmoid(x6), g2_ref[...])
    if has_vres:
        v = v + (vf_ref[...] - v) * _sigmoid(v0_ref[...] + _dot3(_dot3(v, v1_ref[...]), v2_ref[...]))
    else:
        vfo_ref[...] = v

    head_ones = hones_ref[...]
    kk = k * kk_ref[...]
    kk = kk * lax.rsqrt(jnp.maximum(_dot_exact_rhs(kk * kk, head_ones), 1e-24))
    k2 = k * (1.0 + (a - 1.0) * ka_ref[...])
    bonus = _dot_exact_rhs(r * k2 * rk_ref[...], head_ones) * v

    c = _chunk_sums(psel_ref[...], lw)
    c_last = _chunk_sums(tsel_ref[...], lw)
    b = kk * a
    e_neg = jnp.exp(-c)
    e_end = jnp.exp(c_last - c)

    def units(val):
        return jnp.concatenate([val[:, h * HEAD_DIM:(h + 1) * HEAD_DIM].reshape(nc, CHUNK, HEAD_DIM)
                                for h in range(RWKV_H)], axis=0)

    rt = units(r * jnp.exp(c))
    at = units(-kk * jnp.exp(c - lw))
    bt = units(b * e_neg)
    kt = units(k2 * e_neg)
    bh = units(b * e_end)
    kh = units(k2 * e_end)
    vv = units(v)
    gam = units(jnp.exp(c_last))[:, 0:1, :]

    n = RWKV_H * nc
    qi = lax.broadcasted_iota(jnp.int32, (n, CHUNK, CHUNK), 1)
    qj = lax.broadcasted_iota(jnp.int32, (n, CHUNK, CHUNK), 2)
    strict = qj < qi
    incl = qj <= qi
    eye = qj == qi
    same_block = [qi // size == qj // size for size in (2, 4, 8, 16, 32, CHUNK)]
    a_ab = jnp.where(strict, _bdot(at, bt, BNT, 1), 0.0)
    a_ak = jnp.where(strict, _bdot(at, kt, BNT, 1), 0.0)
    p_b = jnp.where(incl, _bdot(rt, bt, BNT, 1), 0.0)
    p_k = jnp.where(incl, _bdot(rt, kt, BNT, 1), 0.0)
    t = jnp.where(eye, 1.0, jnp.where(same_block[0], a_ab, 0.0))
    for small, big in zip(same_block[:-1], same_block[1:]):
        t = t + _bdot(_bdot(t, jnp.where(big & ~small, a_ab, 0.0), BNN, 1), t, BNN, 1)
    w = _bdot(t, at, BNN, 1)
    u_loc = _bdot(t, _bdot(a_ak, vv, BNN, 1), BNN, 1)
    q_t = rt + _bdot(p_b, w, BNN, 1)
    o_loc = _bdot(p_b, u_loc, BNN, 1) + _bdot(p_k, vv, BNN, 1)
    per_chunk = lambda x: x.reshape(RWKV_H, nc, CHUNK, HEAD_DIM)
    g_mat = per_chunk(_bdot(w, bh, BTN, 1) + jnp.where(eye, gam, 0.0))
    h_mat = per_chunk(_bdot(u_loc, bh, BTN, 1) + _bdot(vv, kh, BTN, 1))

    s = state_ref[...]
    before = []
    for ci in range(nc):
        before.append(s)
        s = _bdot(s, g_mat[:, ci], BNN, 3) + h_mat[:, ci]
    state_ref[...] = s

    o = _bdot(q_t, jnp.stack(before, axis=1).reshape(n, HEAD_DIM, HEAD_DIM), BNT, 3) + o_loc
    for h in range(RWKV_H):
        os_ref[:, h * HEAD_DIM:(h + 1) * HEAD_DIM] = o[h * nc:(h + 1) * nc].reshape(tile, HEAD_DIM)

    o = os_ref[...]
    mean = _dot_exact_rhs(o, head_ones) * (1.0 / HEAD_DIM)
    oc = o - mean
    var = _dot_exact_rhs(oc * oc, head_ones) * (1.0 / HEAD_DIM)
    on = oc * lax.rsqrt(var + RWKV_GN_EPS) * lng_ref[...] + lnb_ref[...]
    o_ref[...] = (on + bonus) * g


def _rwkv_mixer(y, seq_len, mu, w0, w2, a0, a2, g2, k_k, k_a, r_k, ln_g, ln_b, v_first, vres):
    m = y.shape[0]
    tile = RWKV_ROWS
    nt = seq_len // tile
    has_vres = vres is not None
    row = lambda wd: pl.BlockSpec((tile, wd), lambda b, j: (b * nt + j, 0))
    full = lambda arr: pl.BlockSpec(arr.shape, lambda b, j: (0, 0))
    vec = lambda t: t.reshape(1, -1)
    pad_rows = lambda w, start: jnp.zeros((LANES, RWKV_W), F32).at[start:start + w.shape[0]].set(w)
    consts = [vec(mu), vec(w0), pad_rows(w2, 0), vec(a0), pad_rows(a2, RWKV_DECAY_RANK),
              pad_rows(g2, RWKV_DECAY_RANK + RWKV_AAA_RANK), vec(k_k), vec(k_a), vec(r_k), vec(ln_g), vec(ln_b),
              *_chunk_selectors(), _head_ones(RWKV_W)]
    args = [y] + ([v_first] if has_vres else []) + consts
    in_specs = [row(RWKV_COLS)] + ([row(RWKV_W)] if has_vres else []) + [full(c) for c in consts]
    out_shape = [jax.ShapeDtypeStruct((m, RWKV_W), F32)]
    out_specs = [row(RWKV_W)]
    if has_vres:
        v0, v1, v2 = vres
        extra = [vec(v0), jnp.pad(v1, ((0, 0), (0, LANES - v1.shape[1]))),
                 jnp.pad(v2, ((0, LANES - v2.shape[0]), (0, 0)))]
        args += extra
        in_specs += [full(c) for c in extra]
    else:
        out_shape.append(jax.ShapeDtypeStruct((m, RWKV_W), F32))
        out_specs.append(row(RWKV_W))
    outs = pl.pallas_call(
        functools.partial(_rwkv_kernel, has_vres=has_vres, tile=tile),
        grid=(m // seq_len, nt),
        in_specs=in_specs, out_specs=out_specs, out_shape=out_shape,
        scratch_shapes=[pltpu.VMEM((RWKV_H, HEAD_DIM, HEAD_DIM), F32), pltpu.VMEM((1, RWKV_COLS), F32),
                        pltpu.VMEM((tile, RWKV_W), F32)],
        compiler_params=_cparams("parallel", "arbitrary"),
        name="rwkv7",
    )(*args)
    return (outs[0], v_first) if has_vres else (outs[0], outs[1])


ATT_PAD = ATT_BLOCK * max(DILATIONS)
ATT_UNROLL = 8


def _att_kernel(q_ref, k_ref, v_ref, o_ref, qs_ref, ks_ref, vs_ref, m_ref, n_ref, d_ref, *, seq_len):
    qs_ref[...] = q_ref[...].astype(F32)
    ks_ref[0:ATT_PAD, :] = jnp.zeros((ATT_PAD, LANES), F32)
    vs_ref[0:ATT_PAD, :] = jnp.zeros((ATT_PAD, LANES), F32)
    ks_ref[ATT_PAD:, :] = k_ref[...].astype(F32)
    vs_ref[ATT_PAD:, :] = v_ref[...].astype(F32)

    qi = lax.broadcasted_iota(jnp.int32, (ATT_BLOCK, 2 * ATT_BLOCK), 0)
    kc = lax.broadcasted_iota(jnp.int32, (ATT_BLOCK, 2 * ATT_BLOCK), 1)
    band = (kc >= qi) & (kc <= qi + ATT_BLOCK)
    head0 = lax.broadcasted_iota(jnp.int32, (ATT_BLOCK, LANES), 1) < HEAD_DIM

    def block(g, dil, first, q, k, v, rows):
        valid = band & ((kc >= ATT_BLOCK) | jnp.logical_not(first))
        v1 = jnp.concatenate([v, jnp.ones_like(v)], axis=1)
        mxs, pvs = [], []
        for hh in range(2):
            qh = jnp.where(head0 if hh == 0 else jnp.logical_not(head0), q, jnp.zeros_like(q))
            s = jnp.where(valid, lax.dot_general(qh, k, NT, preferred_element_type=F32), NEG_INF)
            mx = jnp.max(s, axis=-1, keepdims=True)
            mxs.append(mx)
            pvs.append(jnp.dot(jnp.exp(s - mx).astype(BF16), v1, preferred_element_type=F32))
        mx = jnp.where(head0, mxs[0], mxs[1])
        pv = jnp.where(head0, pvs[0][:, :LANES], pvs[1][:, :LANES])
        den = jnp.where(head0, pvs[0][:, LANES:], pvs[1][:, LANES:])
        if g > 0:
            m_old = m_ref[rows, :]
            m_new = jnp.maximum(m_old, mx)
            a_old = jnp.exp(m_old - m_new)
            a_new = jnp.exp(mx - m_new)
            pv = a_old * n_ref[rows, :] + a_new * pv
            den = a_old * d_ref[rows, :] + a_new * den
            mx = m_new
        if g == len(DILATIONS) - 1:
            o_ref[rows, :] = pv / den
        else:
            m_ref[rows, :] = mx
            n_ref[rows, :] = pv
            d_ref[rows, :] = den

    for g, dil in enumerate(DILATIONS):
        npairs = seq_len // dil // ATT_BLOCK // 2
        step = ATT_BLOCK * dil

        def body(it, carry, g=g, dil=dil, npairs=npairs, step=step):
            r = it // npairs
            jp = it % npairs
            start = r + jp * (2 * step)
            q = qs_ref[pl.ds(start, 2 * ATT_BLOCK, stride=dil), :].astype(BF16)
            k = ks_ref[pl.ds(start + (ATT_PAD - step), 3 * ATT_BLOCK, stride=dil), :].astype(BF16)
            v = vs_ref[pl.ds(start + (ATT_PAD - step), 3 * ATT_BLOCK, stride=dil), :].astype(BF16)
            for u in range(2):
                block(g, dil, (jp == 0) if u == 0 else False, q[u * ATT_BLOCK:(u + 1) * ATT_BLOCK],
                      k[u * ATT_BLOCK:(u + 2) * ATT_BLOCK], v[u * ATT_BLOCK:(u + 2) * ATT_BLOCK],
                      pl.ds(start + u * step, ATT_BLOCK, stride=dil))
            return carry

        lax.fori_loop(0, dil * npairs, body, 0, unroll=ATT_UNROLL)


def _att_mixer(y_att, seq_len):
    m = y_att.shape[0]
    npair = ATT_W // LANES
    col = lambda c: pl.BlockSpec((seq_len, LANES), lambda b, hp: (b, c * npair + hp))
    seq_buf = pltpu.VMEM((seq_len, LANES), F32)
    pad_buf = pltpu.VMEM((ATT_PAD + seq_len, LANES), F32)
    return pl.pallas_call(
        functools.partial(_att_kernel, seq_len=seq_len),
        grid=(m // seq_len, npair),
        in_specs=[col(0), col(1), col(2)],
        out_specs=pl.BlockSpec((seq_len, LANES), lambda b, hp: (b, hp)),
        out_shape=jax.ShapeDtypeStruct((m, ATT_W), F32),
        scratch_shapes=[seq_buf, pad_buf, pad_buf, seq_buf, seq_buf, seq_buf],
        compiler_params=_cparams("parallel", "parallel"),
        name="dilated_att",
    )(y_att, y_att, y_att)


def _gla_kernel(y_ref, gu_ref, gb_ref, ng_ref, psel_ref, tsel_ref, hones_ref, o_ref, state_ref, os_ref, *, tile):
    nc = tile // CHUNK
    n = GLA_H * nc

    @pl.when(pl.program_id(1) == 0)
    def _():
        state_ref[...] = jnp.zeros_like(state_ref)

    def units(val):
        return jnp.concatenate([val[:, h * HEAD_DIM:(h + 1) * HEAD_DIM].reshape(nc, CHUNK, HEAD_DIM)
                                for h in range(GLA_H)], axis=0)

    q = y_ref[:, 0:GLA_W] * HEAD_DIM ** -0.5
    k = y_ref[:, GLA_W:2 * GLA_W]
    og = y_ref[:, 3 * GLA_W:4 * GLA_W]
    lg = _log_sigmoid(_dot3(y_ref[:, 4 * GLA_W:], gu_ref[...]) + gb_ref[...]) * (1.0 / GLA_GATE_TAU)
    b = _chunk_sums(psel_ref[...], lg)
    b_last = _chunk_sums(tsel_ref[...], lg)
    q_in = units((q * jnp.exp(b)).astype(BF16))
    k_in = units((k * jnp.exp(-b)).astype(BF16))
    k_end = units((k * jnp.exp(b_last - b)).astype(BF16))
    v = units(y_ref[:, 2 * GLA_W:3 * GLA_W].astype(BF16))
    gam = units(jnp.exp(b_last))[:, 0:1, :].reshape(GLA_H, nc, 1, HEAD_DIM)

    ci = lax.broadcasted_iota(jnp.int32, (n, CHUNK, CHUNK), 1)
    cj = lax.broadcasted_iota(jnp.int32, (n, CHUNK, CHUNK), 2)
    d = lambda a, bb, dims: lax.dot_general(a, bb, dims, preferred_element_type=F32)
    att = jnp.where(cj <= ci, d(q_in, k_in, BNT), 0.0)
    o = d(att.astype(BF16), v, BNN)
    kv = d(v, k_end, BTN).reshape(GLA_H, nc, HEAD_DIM, HEAD_DIM)

    s = state_ref[...]
    before = []
    for c in range(nc):
        before.append(s)
        s = s * gam[:, c] + kv[:, c]
    state_ref[...] = s
    s_before = jnp.stack(before, axis=1).reshape(n, HEAD_DIM, HEAD_DIM)
    o = o + d(q_in, s_before.astype(BF16), BNT)
    for h in range(GLA_H):
        os_ref[:, h * HEAD_DIM:(h + 1) * HEAD_DIM] = o[h * nc:(h + 1) * nc].reshape(tile, HEAD_DIM)

    o = os_ref[...]
    ms = _dot_exact_rhs(o * o, hones_ref[...]) * (1.0 / HEAD_DIM)
    o_ref[...] = o * lax.rsqrt(ms + RMS_EPS) * ng_ref[...] * (og * _sigmoid(og))


def _gla_mixer(y, seq_len, gate_up, gate_b, norm_g):
    m = y.shape[0]
    tile = GLA_ROWS
    nt = seq_len // tile
    gu = jnp.pad(gate_up, ((0, LANES - gate_up.shape[0]), (0, 0)))
    gb = gate_b.reshape(1, GLA_W)
    ng = jnp.tile(norm_g, GLA_H).reshape(1, GLA_W)
    consts = [gu, gb, ng, *_chunk_selectors(), _head_ones(GLA_W)]
    full = lambda a: pl.BlockSpec(a.shape, lambda b, j: (0, 0))
    return pl.pallas_call(
        functools.partial(_gla_kernel, tile=tile),
        grid=(m // seq_len, nt),
        in_specs=[pl.BlockSpec((tile, GLA_COLS_PAD), lambda b, j: (b * nt + j, 0))] + [full(c) for c in consts],
        out_specs=pl.BlockSpec((tile, GLA_W), lambda b, j: (b * nt + j, 0)),
        out_shape=jax.ShapeDtypeStruct((m, GLA_W), F32),
        scratch_shapes=[pltpu.VMEM((GLA_H, HEAD_DIM, HEAD_DIM), F32), pltpu.VMEM((tile, GLA_W), F32)],
        compiler_params=_cparams("parallel", "arbitrary"),
        name="gla",
    )(y, *consts)


def _outproj_kernel(or_ref, oa_ref, og_ref, w_ref, h_ref, o_ref):
    acc = jnp.dot(or_ref[...].astype(BF16), w_ref[0:RWKV_W, :], preferred_element_type=F32)
    acc += jnp.dot(oa_ref[...].astype(BF16), w_ref[RWKV_W:RWKV_W + ATT_W, :], preferred_element_type=F32)
    acc += jnp.dot(og_ref[...].astype(BF16), w_ref[RWKV_W + ATT_W:, :], preferred_element_type=F32)
    o_ref[...] = h_ref[...] + acc


def _outproj(o_rwkv, o_att, o_gla, w, h):
    m = h.shape[0]
    tm = OUTPROJ_ROWS
    row = lambda wd: pl.BlockSpec((tm, wd), lambda i: (i, 0))
    return pl.pallas_call(
        _outproj_kernel,
        grid=(m // tm,),
        in_specs=[row(RWKV_W), row(ATT_W), row(GLA_W), pl.BlockSpec(w.shape, lambda i: (0, 0)), row(D_MODEL)],
        out_specs=row(D_MODEL),
        out_shape=jax.ShapeDtypeStruct(h.shape, F32),
        compiler_params=_cparams("parallel"),
        name="outproj",
    )(o_rwkv, o_att, o_gla, w, h)


def _ffn_kernel(x_ref, g_ref, wg_ref, wu_ref, wd_ref, o_ref, xn_ref):
    @pl.when(pl.program_id(1) == 0)
    def _():
        xn_ref[...] = _rms(x_ref[...], g_ref[...]).astype(xn_ref.dtype)
        o_ref[...] = x_ref[...]

    xn = xn_ref[...]
    gate = jnp.dot(xn, wg_ref[...], preferred_element_type=F32)
    up = jnp.dot(xn, wu_ref[...], preferred_element_type=F32)
    act = (gate * _sigmoid(gate) * up).astype(BF16)
    o_ref[...] += jnp.dot(act, wd_ref[...], preferred_element_type=F32)


def _ffn(h, g, wg, wu, wd):
    m = h.shape[0]
    tm, tf = FFN_ROWS, FFN_COLS
    row = pl.BlockSpec((tm, D_MODEL), lambda i, f: (i, 0))
    return pl.pallas_call(
        _ffn_kernel,
        grid=(m // tm, FFN_DENSE // tf),
        in_specs=[row, pl.BlockSpec(g.shape, lambda i, f: (0, 0)),
                  pl.BlockSpec((D_MODEL, tf), lambda i, f: (0, f)),
                  pl.BlockSpec((D_MODEL, tf), lambda i, f: (0, f)),
                  pl.BlockSpec((tf, D_MODEL), lambda i, f: (f, 0))],
        out_specs=row,
        out_shape=jax.ShapeDtypeStruct(h.shape, F32),
        scratch_shapes=[pltpu.VMEM((tm, D_MODEL), BF16)],
        compiler_params=_cparams("parallel", "arbitrary"),
        name="ffn_dense",
    )(h, g, wg, wu, wd)


MOE_BLOCK = 896
MOE_SUB = 256
MOE_FT = 896


def _router_kernel(x_ref, g_ref, wr_ref, tri_ref, xnt_ref, rank_ref, gate_ref, cnt_ref, *, n_tokens):
    row = lax.broadcasted_iota(jnp.int32, (MOE_BLOCK, 1), 0) + pl.program_id(0) * MOE_BLOCK
    real = row < n_tokens
    xn = jnp.where(real, _rms(x_ref[...], g_ref[...]), 0.0)
    xnt_ref[...] = xn.T.astype(BF16)
    logits = _dot3(xn, wr_ref[...])
    lane = lax.broadcasted_iota(jnp.int32, logits.shape, 1)
    lg = jnp.where(lane < N_EXPERTS, logits, NEG_INF)
    m1 = jnp.max(lg, axis=-1, keepdims=True)
    i1 = jnp.min(jnp.where(lg == m1, lane, LANES), axis=-1, keepdims=True)
    lg2 = jnp.where(lane == i1, NEG_INF, lg)
    m2 = jnp.max(lg2, axis=-1, keepdims=True)
    i2 = jnp.min(jnp.where(lg2 == m2, lane, LANES), axis=-1, keepdims=True)
    e2 = jnp.exp(m2 - m1)
    hot1 = (lane == i1) & real
    hot2 = (lane == i2) & real
    onehot = (hot1 | hot2).astype(F32)
    rank = jnp.dot(tri_ref[...], onehot.astype(BF16), preferred_element_type=F32)
    rank = jnp.where(hot1 | hot2, rank, -1.0)
    gate = jnp.where(hot1, 1.0 / (1.0 + e2), jnp.where(hot2, e2 / (1.0 + e2), 0.0))
    rank_ref[0] = rank.T[:N_EXPERTS]
    gate_ref[0] = gate.T[:N_EXPERTS]
    cnt_ref[0] = jnp.broadcast_to(jnp.sum(onehot, axis=0, keepdims=True), (8, LANES)).astype(jnp.int32)


def _route(h, g, w_router):
    m = h.shape[0]
    sb = MOE_BLOCK
    nblk = pl.cdiv(m, sb)
    t = jnp.arange(sb)
    tri = (t[None, :] < t[:, None]).astype(BF16)
    full = lambda a: pl.BlockSpec(a.shape, lambda i: (0,) * a.ndim)
    per_expert = pl.BlockSpec((1, N_EXPERTS, sb), lambda i: (i, 0, 0))
    return pl.pallas_call(
        functools.partial(_router_kernel, n_tokens=m),
        grid=(nblk,),
        in_specs=[pl.BlockSpec((sb, D_MODEL), lambda i: (i, 0)), full(g), full(w_router), full(tri)],
        out_specs=[pl.BlockSpec((D_MODEL, sb), lambda i: (0, i)), per_expert, per_expert,
                   pl.BlockSpec((1, 8, LANES), lambda i: (i, 0, 0))],
        out_shape=[jax.ShapeDtypeStruct((D_MODEL, nblk * sb), BF16), jax.ShapeDtypeStruct((nblk, N_EXPERTS, sb), F32),
                   jax.ShapeDtypeStruct((nblk, N_EXPERTS, sb), F32),
                   jax.ShapeDtypeStruct((nblk, 8, LANES), jnp.int32)],
        compiler_params=_cparams("parallel"),
        name="moe_router",
    )(h, g, w_router, tri)


def _moe_kernel(cnt_ref, x_ref, xnt_ref, rank_ref, gate_ref, wg_ref, wu_ref, wd_ref, fg_ref, o_ref,
                xs_ref, y_ref, acc_ref, *, final_norm):
    i, e, f = pl.program_id(0), pl.program_id(1), pl.program_id(2)
    last_f = f == pl.num_programs(2) - 1
    n_sub = (cnt_ref[i * N_EXPERTS + e] + (MOE_SUB - 1)) // MOE_SUB

    @pl.when((e == 0) & (f == 0))
    def _():
        acc_ref[...] = jnp.zeros_like(acc_ref)

    def selection(j):
        want = lax.broadcasted_iota(jnp.int32, (MOE_SUB, MOE_BLOCK), 0) + j * MOE_SUB
        return (rank_ref[0, pl.ds(e, 1), :] == want.astype(F32)).astype(BF16)

    def sub_body(j, carry):
        @pl.when(f == 0)
        def _():
            xs_ref[j] = lax.dot_general(xnt_ref[...], selection(j), NT, preferred_element_type=F32).astype(BF16)
            y_ref[j] = jnp.zeros((D_MODEL, MOE_SUB), F32)

        xs = xs_ref[j]
        gate = lax.dot_general(wg_ref[0], xs, TN, preferred_element_type=F32)
        up = lax.dot_general(wu_ref[0], xs, TN, preferred_element_type=F32)
        act = (gate * _sigmoid(gate) * up).astype(BF16)
        y_ref[j] += lax.dot_general(wd_ref[0], act, TN, preferred_element_type=F32)

        @pl.when(last_f)
        def _():
            back = jnp.dot(y_ref[j].astype(BF16), selection(j), preferred_element_type=F32)
            acc_ref[...] += gate_ref[0, pl.ds(e, 1), :] * back

        return carry

    lax.fori_loop(0, n_sub, sub_body, 0)

    @pl.when((e == N_EXPERTS - 1) & last_f)
    def _():
        y = x_ref[...] + acc_ref[...].T
        o_ref[...] = _rms(y, fg_ref[...]) if final_norm else y


def _moe(h, g, w_router, wg, wu, wd, final_g, final_norm):
    m = h.shape[0]
    sb, ft = MOE_BLOCK, MOE_FT
    xnt, rank, gate, cnt = _route(h, g, w_router)
    counts = cnt[:, 0, :N_EXPERTS].reshape(-1)
    row = pl.BlockSpec((sb, D_MODEL), lambda i, e, f, c: (i, 0))
    per_expert = pl.BlockSpec((1, N_EXPERTS, sb), lambda i, e, f, c: (i, 0, 0))
    grid_spec = pltpu.PrefetchScalarGridSpec(
        num_scalar_prefetch=1,
        grid=(pl.cdiv(m, sb), N_EXPERTS, FFN_EXPERT // ft),
        in_specs=[row, pl.BlockSpec((D_MODEL, sb), lambda i, e, f, c: (0, i)), per_expert, per_expert,
                  pl.BlockSpec((1, D_MODEL, ft), lambda i, e, f, c: (e, 0, f)),
                  pl.BlockSpec((1, D_MODEL, ft), lambda i, e, f, c: (e, 0, f)),
                  pl.BlockSpec((1, ft, D_MODEL), lambda i, e, f, c: (e, f, 0)),
                  pl.BlockSpec((1, D_MODEL), lambda i, e, f, c: (0, 0))],
        out_specs=row,
        scratch_shapes=[pltpu.VMEM((pl.cdiv(sb, MOE_SUB), D_MODEL, MOE_SUB), BF16),
                        pltpu.VMEM((pl.cdiv(sb, MOE_SUB), D_MODEL, MOE_SUB), F32),
                        pltpu.VMEM((D_MODEL, sb), F32)],
    )
    return pl.pallas_call(
        functools.partial(_moe_kernel, final_norm=final_norm),
        grid_spec=grid_spec,
        out_shape=jax.ShapeDtypeStruct(h.shape, F32),
        compiler_params=_cparams("parallel", "arbitrary", "arbitrary"),
        name="moe",
    )(counts, h, xnt, rank, gate, wg, wu, wd, final_g)


def kernel(x, positions, mix_norm_g, w_in, rwkv_mu, rwkv_w0, rwkv_w2, rwkv_a0, rwkv_a2, rwkv_g2, rwkv_k_k, rwkv_k_a, rwkv_r_k, rwkv_ln_g, rwkv_ln_b, rwkv_v0, rwkv_v1, rwkv_v2, gla_gate_up, gla_gate_b, gla_norm_g, w_out, ffn_norm_g, ffn_w_gate, ffn_w_up, ffn_w_down, moe_router, moe_w_gate, moe_w_up, moe_w_down, final_norm_g):
    B, S, D = x.shape
    M = B * S
    h = x.reshape(M, D)
    cos, sa, sb = _rope_tables(positions)
    v_first = None
    for layer in range(DEPTH):
        w = w_in[layer]
        wr = w[:, :RWKV_COLS].astype(BF16)
        wa = w[:, RWKV_COLS:RWKV_COLS + ATT_COLS].astype(BF16)
        wgl = w[:, RWKV_COLS + ATT_COLS:]
        wgl = jnp.concatenate([wgl[:, :3 * GLA_W], wgl[:, 3 * GLA_W + GLA_GATE_RANK:],
                               wgl[:, 3 * GLA_W:3 * GLA_W + GLA_GATE_RANK],
                               jnp.zeros((D, LANES - GLA_GATE_RANK), w.dtype)], axis=1).astype(BF16)
        y_rwkv, y_att, y_gla = _inproj(h, mix_norm_g[layer].reshape(1, D), wr, wa, wgl, cos, sa, sb)
        vres = None if layer == 0 else (rwkv_v0[layer - 1], rwkv_v1[layer - 1], rwkv_v2[layer - 1])
        o_rwkv, v_first = _rwkv_mixer(y_rwkv, S, rwkv_mu[layer], rwkv_w0[layer], rwkv_w2[layer], rwkv_a0[layer],
                                      rwkv_a2[layer], rwkv_g2[layer], rwkv_k_k[layer], rwkv_k_a[layer],
                                      rwkv_r_k[layer], rwkv_ln_g[layer], rwkv_ln_b[layer], v_first, vres)
        o_att = _att_mixer(y_att, S)
        o_gla = _gla_mixer(y_gla, S, gla_gate_up[layer], gla_gate_b[layer], gla_norm_g[layer])
        h = _outproj(o_rwkv, o_att, o_gla, w_out[layer].astype(BF16), h)
        i = layer // 2
        g = ffn_norm_g[layer].reshape(1, D)
        if layer % 2 == 0:
            h = _ffn(h, g, ffn_w_gate[i].astype(BF16), ffn_w_up[i].astype(BF16), ffn_w_down[i].astype(BF16))
        else:
            wrt = jnp.pad(moe_router[i], ((0, 0), (0, LANES - N_EXPERTS)))
            h = _moe(h, g, wrt, moe_w_gate[i].astype(BF16), moe_w_up[i].astype(BF16), moe_w_down[i].astype(BF16),
                     final_norm_g.reshape(1, D), layer == DEPTH - 1)
    return h.reshape(B, S, D)
```

```python
import functools

import jax
import jax.numpy as jnp
from jax import lax
from jax.experimental import pallas as pl
from jax.experimental.pallas import tpu as pltpu

D_MODEL = 1024
DEPTH = 2
HEAD_DIM = 64
RWKV_W = 256
RWKV_H = 4
ATT_W = 384
GLA_W = 384
GLA_H = 6
RWKV_DECAY_RANK = 32
RWKV_AAA_RANK = 32
RWKV_GATE_RANK = 64
RWKV_GN_EPS = 64e-5
GLA_GATE_RANK = 16
GLA_GATE_TAU = 16.0
CHUNK = 64
DILATIONS = (1, 4, 16)
ATT_BLOCK = 128
ROPE_THETA = 500000.0
ROPE_DIMS = 16
ROPE_HALF = 8
FFN_DENSE = 2816
N_EXPERTS = 8
FFN_EXPERT = 3584
RMS_EPS = 1e-5
NEG_INF = -1e30

RWKV_COLS = 3 * RWKV_W + RWKV_DECAY_RANK + RWKV_AAA_RANK + RWKV_GATE_RANK
ATT_COLS = 3 * ATT_W
LANES = 128
GLA_COLS_PAD = 4 * GLA_W + LANES

VMEM_LIMIT = 56 * 1024 * 1024

ROPE_ROWS = 2048
INPROJ_ROWS = 1024
RWKV_ROWS = 512
GLA_ROWS = 1024
FFN_ROWS = 1024
FFN_COLS = FFN_DENSE // 2

F32 = jnp.float32
BF16 = jnp.bfloat16
NN = (((1,), (0,)), ((), ()))
NT = (((1,), (1,)), ((), ()))
TN = (((0,), (0,)), ((), ()))
BNN = (((2,), (1,)), ((0,), (0,)))
BNT = (((2,), (2,)), ((0,), (0,)))
BTN = (((1,), (1,)), ((0,), (0,)))


def _cparams(*sem):
    return pltpu.CompilerParams(dimension_semantics=sem, vmem_limit_bytes=VMEM_LIMIT)


def _rms(x, g):
    return x * lax.rsqrt(jnp.mean(x * x, axis=-1, keepdims=True) + RMS_EPS) * g


def _sigmoid(x):
    return 1.0 / (1.0 + jnp.exp(-x))


def _log_sigmoid(x):
    return jnp.minimum(x, 0.0) - jnp.log(1.0 + jnp.exp(-jnp.abs(x)))


def _split_hi_lo(a):
    hi = a.astype(BF16)
    return hi, (a - hi.astype(F32)).astype(BF16)


def _dot3(a, b, dims=NN):
    a_hi, a_lo = _split_hi_lo(a)
    b_hi, b_lo = _split_hi_lo(b)
    d = lambda x, y: lax.dot_general(x, y, dims, preferred_element_type=F32)
    return d(a_hi, b_hi) + d(a_lo, b_hi) + d(a_hi, b_lo)


def _dot_exact_rhs(x, r_bf16):
    x_hi, x_lo = _split_hi_lo(x)
    return jnp.dot(x_hi, r_bf16, preferred_element_type=F32) + jnp.dot(x_lo, r_bf16, preferred_element_type=F32)


def _bdot(a, b, dims, passes):
    d = lambda x, y: lax.dot_general(x, y, dims, preferred_element_type=F32)
    if passes == 1:
        return d(a.astype(BF16), b.astype(BF16))
    a_hi, a_lo = _split_hi_lo(a)
    b_hi, b_lo = _split_hi_lo(b)
    return d(a_hi, b_hi) + d(a_lo, b_hi) + d(a_hi, b_lo)


def _head_ones(width):
    idx = jnp.arange(width) // HEAD_DIM
    return (idx[:, None] == idx[None, :]).astype(BF16)


def _chunk_selectors():
    t = jnp.arange(CHUNK)
    return (t[None, :] <= t[:, None]).astype(BF16), jnp.ones((CHUNK, CHUNK), BF16)


def _chunk_sums(sel, x):
    nc = x.shape[0] // CHUNK
    hi, lo = _split_hi_lo(x.reshape(nc, CHUNK, x.shape[1]))
    s3 = jnp.broadcast_to(sel[None], (nc, CHUNK, CHUNK))
    d = lambda a, b: lax.dot_general(a, b, BNN, preferred_element_type=F32)
    return (d(s3, hi) + d(s3, lo)).reshape(x.shape)


def _rope_table_kernel(pos_ref, invf_ref, cos_ref, sa_ref, sb_ref):
    ang = pos_ref[...].astype(F32) * invf_ref[...]
    lane = lax.broadcasted_iota(jnp.int32, ang.shape, 1) % HEAD_DIM
    c = jnp.cos(ang)
    s = jnp.sin(ang)
    cos_ref[...] = jnp.where(lane < ROPE_DIMS, c, 1.0)
    sa_ref[...] = jnp.where((lane >= ROPE_HALF) & (lane < ROPE_DIMS), s, 0.0)
    sb_ref[...] = jnp.where(lane < ROPE_HALF, -s, 0.0)


def _rope_tables(positions):
    m = positions.size
    tm = ROPE_ROWS
    lane = jnp.arange(LANES) % ROPE_HALF
    invf = (ROPE_THETA ** (-lane.astype(F32) / ROPE_HALF)).reshape(1, LANES)
    out = jax.ShapeDtypeStruct((m, LANES), F32)
    row = pl.BlockSpec((tm, LANES), lambda i: (i, 0))
    return pl.pallas_call(
        _rope_table_kernel,
        grid=(m // tm,),
        in_specs=[pl.BlockSpec((tm, 1), lambda i: (i, 0)), pl.BlockSpec((1, LANES), lambda i: (0, 0))],
        out_specs=[row, row, row],
        out_shape=[out, out, out],
        compiler_params=_cparams("parallel"),
        name="rope_tables",
    )(positions.reshape(m, 1), invf)


def _inproj_kernel(x_ref, g_ref, wr_ref, wa_ref, wg_ref, cos_ref, sa_ref, sb_ref, yr_ref, ya_ref, yg_ref):
    xb = _rms(x_ref[...], g_ref[...]).astype(BF16)
    yr_ref[...] = jnp.dot(xb, wr_ref[...], preferred_element_type=F32)
    yg_ref[...] = jnp.dot(xb, wg_ref[...], preferred_element_type=F32)
    ya = jnp.dot(xb, wa_ref[...], preferred_element_type=F32)
    cos, sa, sb = cos_ref[...], sa_ref[...], sb_ref[...]
    for j in range(2 * ATT_W // LANES):
        blk = ya[:, j * LANES:(j + 1) * LANES]
        rot = blk * cos + pltpu.roll(blk, ROPE_HALF, 1) * sa + pltpu.roll(blk, LANES - ROPE_HALF, 1) * sb
        if j < ATT_W // LANES:
            rot = rot * HEAD_DIM ** -0.5
        ya_ref[:, j * LANES:(j + 1) * LANES] = rot.astype(ya_ref.dtype)
    ya_ref[:, 2 * ATT_W:] = ya[:, 2 * ATT_W:].astype(ya_ref.dtype)


def _inproj(h, g, wr, wa, wg, cos, sa, sb):
    m = h.shape[0]
    tm = INPROJ_ROWS
    row = lambda w: pl.BlockSpec((tm, w), lambda i: (i, 0))
    full = lambda a: pl.BlockSpec(a.shape, lambda i: (0, 0))
    return pl.pallas_call(
        _inproj_kernel,
        grid=(m // tm,),
        in_specs=[row(D_MODEL), full(g), full(wr), full(wa), full(wg), row(LANES), row(LANES), row(LANES)],
        out_specs=[row(RWKV_COLS), row(ATT_COLS), row(GLA_COLS_PAD)],
        out_shape=[jax.ShapeDtypeStruct((m, RWKV_COLS), F32),
                   jax.ShapeDtypeStruct((m, ATT_COLS), BF16),
                   jax.ShapeDtypeStruct((m, GLA_COLS_PAD), F32)],
        compiler_params=_cparams("parallel"),
        name="inproj",
    )(h, g, wr, wa, wg, cos, sa, sb)


def _rwkv_kernel(*refs, has_vres, tile):
    nc = tile // CHUNK
    if has_vres:
        (y_ref, vf_ref, mu_ref, w0_ref, w2_ref, a0_ref, a2_ref, g2_ref, kk_ref, ka_ref, rk_ref, lng_ref, lnb_ref,
         psel_ref, tsel_ref, hones_ref, v0_ref, v1_ref, v2_ref, o_ref, *scratch) = refs
    else:
        (y_ref, mu_ref, w0_ref, w2_ref, a0_ref, a2_ref, g2_ref, kk_ref, ka_ref, rk_ref, lng_ref, lnb_ref,
         psel_ref, tsel_ref, hones_ref, o_ref, vfo_ref, *scratch) = refs
    state_ref, prev_ref, os_ref = scratch

    @pl.when(pl.program_id(1) == 0)
    def _():
        state_ref[...] = jnp.zeros_like(state_ref)
        prev_ref[...] = jnp.zeros_like(prev_ref)

    y = y_ref[...]
    row = lax.broadcasted_iota(jnp.int32, (tile, 1), 0)
    ysh = jnp.where(row == 0, prev_ref[...], pltpu.roll(y, 1, 0))
    prev_ref[...] = y[tile - 1:tile, :]
    ym = y + (ysh - y) * mu_ref[...]
    r = ym[:, 0:RWKV_W]
    k = ym[:, RWKV_W:2 * RWKV_W]
    v = ym[:, 2 * RWKV_W:3 * RWKV_W]
    x6 = ym[:, 3 * RWKV_W:]
    wl = w0_ref[...] + _dot3(jnp.tanh(x6), w2_ref[...])
    lw = -jnp.exp(_log_sigmoid(wl) - 0.5)
    a = _sigmoid(a0_ref[...] + _dot3(x6, a2_ref[...]))
    g = _dot3(_sigmoid(x6), g2_ref[...])
    if has_vres:
        v = v + (vf_ref[...] - v) * _sigmoid(v0_ref[...] + _dot3(_dot3(v, v1_ref[...]), v2_ref[...]))
    else:
        vfo_ref[...] = v

    head_ones = hones_ref[...]
    kk = k * kk_ref[...]
    kk = kk * lax.rsqrt(jnp.maximum(_dot_exact_rhs(kk * kk, head_ones), 1e-24))
    k2 = k * (1.0 + (a - 1.0) * ka_ref[...])
    bonus = _dot_exact_rhs(r * k2 * rk_ref[...], head_ones) * v

    c = _chunk_sums(psel_ref[...], lw)
    c_last = _chunk_sums(tsel_ref[...], lw)
    b = kk * a
    e_neg = jnp.exp(-c)
    e_end = jnp.exp(c_last - c)

    def units(val):
        return jnp.concatenate([val[:, h * HEAD_DIM:(h + 1) * HEAD_DIM].reshape(nc, CHUNK, HEAD_DIM)
                                for h in range(RWKV_H)], axis=0)

    rt = units(r * jnp.exp(c))
    at = units(-kk * jnp.exp(c - lw))
    bt = units(b * e_neg)
    kt = units(k2 * e_neg)
    bh = units(b * e_end)
    kh = units(k2 * e_end)
    vv = units(v)
    gam = units(jnp.exp(c_last))[:, 0:1, :]

    n = RWKV_H * nc
    qi = lax.broadcasted_iota(jnp.int32, (n, CHUNK, CHUNK), 1)
    qj = lax.broadcasted_iota(jnp.int32, (n, CHUNK, CHUNK), 2)
    strict = qj < qi
    incl = qj <= qi
    eye = qj == qi
    same_block = [qi // size == qj // size for size in (2, 4, 8, 16, 32, CHUNK)]
    a_ab = jnp.where(strict, _bdot(at, bt, BNT, 1), 0.0)
    a_ak = jnp.where(strict, _bdot(at, kt, BNT, 1), 0.0)
    p_b = jnp.where(incl, _bdot(rt, bt, BNT, 1), 0.0)
    p_k = jnp.where(incl, _bdot(rt, kt, BNT, 1), 0.0)
    t = jnp.where(eye, 1.0, jnp.where(same_block[0], a_ab, 0.0))
    for small, big in zip(same_block[:-1], same_block[1:]):
        t = t + _bdot(_bdot(t, jnp.where(big & ~small, a_ab, 0.0), BNN, 1), t, BNN, 1)
    w = _bdot(t, at, BNN, 1)
    u_loc = _bdot(t, _bdot(a_ak, vv, BNN, 1), BNN, 1)
    q_t = rt + _bdot(p_b, w, BNN, 1)
    o_loc = _bdot(p_b, u_loc, BNN, 1) + _bdot(p_k, vv, BNN, 1)
    per_chunk = lambda x: x.reshape(RWKV_H, nc, CHUNK, HEAD_DIM)
    g_mat = per_chunk(_bdot(w, bh, BTN, 1) + jnp.where(eye, gam, 0.0))
    h_mat = per_chunk(_bdot(u_loc, bh, BTN, 1) + _bdot(vv, kh, BTN, 1))

    s = state_ref[...]
    before = []
    for ci in range(nc):
        before.append(s)
        s = _bdot(s, g_mat[:, ci], BNN, 3) + h_mat[:, ci]
    state_ref[...] = s

    o = _bdot(q_t, jnp.stack(before, axis=1).reshape(n, HEAD_DIM, HEAD_DIM), BNT, 3) + o_loc
    for h in range(RWKV_H):
        os_ref[:, h * HEAD_DIM:(h + 1) * HEAD_DIM] = o[h * nc:(h + 1) * nc].reshape(tile, HEAD_DIM)

    o = os_ref[...]
    mean = _dot_exact_rhs(o, head_ones) * (1.0 / HEAD_DIM)
    oc = o - mean
    var = _dot_exact_rhs(oc * oc, head_ones) * (1.0 / HEAD_DIM)
    on = oc * lax.rsqrt(var + RWKV_GN_EPS) * lng_ref[...] + lnb_ref[...]
    o_ref[...] = ((on + bonus) * g).astype(o_ref.dtype)


def _rwkv_mixer(y, seq_len, mu, w0, w2, a0, a2, g2, k_k, k_a, r_k, ln_g, ln_b, v_first, vres):
    m = y.shape[0]
    tile = RWKV_ROWS
    nt = seq_len // tile
    has_vres = vres is not None
    row = lambda wd: pl.BlockSpec((tile, wd), lambda b, j: (b * nt + j, 0))
    full = lambda arr: pl.BlockSpec(arr.shape, lambda b, j: (0, 0))
    vec = lambda t: t.reshape(1, -1)
    pad_rows = lambda w, start: jnp.zeros((LANES, RWKV_W), F32).at[start:start + w.shape[0]].set(w)
    consts = [vec(mu), vec(w0), pad_rows(w2, 0), vec(a0), pad_rows(a2, RWKV_DECAY_RANK),
              pad_rows(g2, RWKV_DECAY_RANK + RWKV_AAA_RANK), vec(k_k), vec(k_a), vec(r_k), vec(ln_g), vec(ln_b),
              *_chunk_selectors(), _head_ones(RWKV_W)]
    args = [y] + ([v_first] if has_vres else []) + consts
    in_specs = [row(RWKV_COLS)] + ([row(RWKV_W)] if has_vres else []) + [full(c) for c in consts]
    out_shape = [jax.ShapeDtypeStruct((m, RWKV_W), BF16)]
    out_specs = [row(RWKV_W)]
    if has_vres:
        v0, v1, v2 = vres
        extra = [vec(v0), jnp.pad(v1, ((0, 0), (0, LANES - v1.shape[1]))),
                 jnp.pad(v2, ((0, LANES - v2.shape[0]), (0, 0)))]
        args += extra
        in_specs += [full(c) for c in extra]
    else:
        out_shape.append(jax.ShapeDtypeStruct((m, RWKV_W), F32))
        out_specs.append(row(RWKV_W))
    outs = pl.pallas_call(
        functools.partial(_rwkv_kernel, has_vres=has_vres, tile=tile),
        grid=(m // seq_len, nt),
        in_specs=in_specs, out_specs=out_specs, out_shape=out_shape,
        scratch_shapes=[pltpu.VMEM((RWKV_H, HEAD_DIM, HEAD_DIM), F32), pltpu.VMEM((1, RWKV_COLS), F32),
                        pltpu.VMEM((tile, RWKV_W), F32)],
        compiler_params=_cparams("parallel", "arbitrary"),
        name="rwkv7",
    )(*args)
    return (outs[0], v_first) if has_vres else (outs[0], outs[1])


ATT_PAD = ATT_BLOCK * max(DILATIONS)
ATT_UNROLL = 8


def _att_kernel(q_ref, k_ref, v_ref, o_ref, qs_ref, ks_ref, vs_ref, m_ref, n_ref, d_ref, *, seq_len):
    qs_ref[...] = q_ref[...].astype(F32)
    ks_ref[0:ATT_PAD, :] = jnp.zeros((ATT_PAD, LANES), F32)
    vs_ref[0:ATT_PAD, :] = jnp.zeros((ATT_PAD, LANES), F32)
    ks_ref[ATT_PAD:, :] = k_ref[...].astype(F32)
    vs_ref[ATT_PAD:, :] = v_ref[...].astype(F32)

    qi = lax.broadcasted_iota(jnp.int32, (ATT_BLOCK, 2 * ATT_BLOCK), 0)
    kc = lax.broadcasted_iota(jnp.int32, (ATT_BLOCK, 2 * ATT_BLOCK), 1)
    band = (kc >= qi) & (kc <= qi + ATT_BLOCK)
    head0 = lax.broadcasted_iota(jnp.int32, (ATT_BLOCK, LANES), 1) < HEAD_DIM

    def block(g, dil, first, q, k, v, rows):
        valid = band & ((kc >= ATT_BLOCK) | jnp.logical_not(first))
        v1 = jnp.concatenate([v, jnp.ones_like(v)], axis=1)
        mxs, pvs = [], []
        for hh in range(2):
            qh = jnp.where(head0 if hh == 0 else jnp.logical_not(head0), q, jnp.zeros_like(q))
            s = jnp.where(valid, lax.dot_general(qh, k, NT, preferred_element_type=F32), NEG_INF)
            mx = jnp.max(s, axis=-1, keepdims=True)
            mxs.append(mx)
            pvs.append(jnp.dot(jnp.exp(s - mx).astype(BF16), v1, preferred_element_type=F32))
        mx = jnp.where(head0, mxs[0], mxs[1])
        pv = jnp.where(head0, pvs[0][:, :LANES], pvs[1][:, :LANES])
        den = jnp.where(head0, pvs[0][:, LANES:], pvs[1][:, LANES:])
        if g > 0:
            m_old = m_ref[rows, :]
            m_new = jnp.maximum(m_old, mx)
            a_old = jnp.exp(m_old - m_new)
            a_new = jnp.exp(mx - m_new)
            pv = a_old * n_ref[rows, :] + a_new * pv
            den = a_old * d_ref[rows, :] + a_new * den
            mx = m_new
        if g == len(DILATIONS) - 1:
            o_ref[rows, :] = pv / den
        else:
            m_ref[rows, :] = mx
            n_ref[rows, :] = pv
            d_ref[rows, :] = den

    for g, dil in enumerate(DILATIONS):
        npairs = seq_len // dil // ATT_BLOCK // 2
        step = ATT_BLOCK * dil

        def body(it, carry, g=g, dil=dil, npairs=npairs, step=step):
            r = it // npairs
            jp = it % npairs
            start = r + jp * (2 * step)
            q = qs_ref[pl.ds(start, 2 * ATT_BLOCK, stride=dil), :].astype(BF16)
            k = ks_ref[pl.ds(start + (ATT_PAD - step), 3 * ATT_BLOCK, stride=dil), :].astype(BF16)
            v = vs_ref[pl.ds(start + (ATT_PAD - step), 3 * ATT_BLOCK, stride=dil), :].astype(BF16)
            for u in range(2):
                block(g, dil, (jp == 0) if u == 0 else False, q[u * ATT_BLOCK:(u + 1) * ATT_BLOCK],
                      k[u * ATT_BLOCK:(u + 2) * ATT_BLOCK], v[u * ATT_BLOCK:(u + 2) * ATT_BLOCK],
                      pl.ds(start + u * step, ATT_BLOCK, stride=dil))
            return carry

        lax.fori_loop(0, dil * npairs, body, 0, unroll=ATT_UNROLL)


def _att_mixer(y_att, seq_len):
    m = y_att.shape[0]
    npair = ATT_W // LANES
    col = lambda c: pl.BlockSpec((seq_len, LANES), lambda b, hp: (b, c * npair + hp))
    seq_buf = pltpu.VMEM((seq_len, LANES), F32)
    pad_buf = pltpu.VMEM((ATT_PAD + seq_len, LANES), F32)
    return pl.pallas_call(
        functools.partial(_att_kernel, seq_len=seq_len),
        grid=(m // seq_len, npair),
        in_specs=[col(0), col(1), col(2)],
        out_specs=pl.BlockSpec((seq_len, LANES), lambda b, hp: (b, hp)),
        out_shape=jax.ShapeDtypeStruct((m, ATT_W), F32),
        scratch_shapes=[seq_buf, pad_buf, pad_buf, seq_buf, seq_buf, seq_buf],
        compiler_params=_cparams("parallel", "parallel"),
        name="dilated_att",
    )(y_att, y_att, y_att)


def _gla_kernel(y_ref, gu_ref, gb_ref, ng_ref, psel_ref, tsel_ref, hones_ref, o_ref, state_ref, os_ref, *, tile):
    nc = tile // CHUNK
    n = GLA_H * nc

    @pl.when(pl.program_id(1) == 0)
    def _():
        state_ref[...] = jnp.zeros_like(state_ref)

    def units(val):
        return jnp.concatenate([val[:, h * HEAD_DIM:(h + 1) * HEAD_DIM].reshape(nc, CHUNK, HEAD_DIM)
                                for h in range(GLA_H)], axis=0)

    q = y_ref[:, 0:GLA_W] * HEAD_DIM ** -0.5
    k = y_ref[:, GLA_W:2 * GLA_W]
    og = y_ref[:, 3 * GLA_W:4 * GLA_W]
    lg = _log_sigmoid(_dot3(y_ref[:, 4 * GLA_W:], gu_ref[...]) + gb_ref[...]) * (1.0 / GLA_GATE_TAU)
    b = _chunk_sums(psel_ref[...], lg)
    b_last = _chunk_sums(tsel_ref[...], lg)
    q_in = units((q * jnp.exp(b)).astype(BF16))
    k_in = units((k * jnp.exp(-b)).astype(BF16))
    k_end = units((k * jnp.exp(b_last - b)).astype(BF16))
    v = units(y_ref[:, 2 * GLA_W:3 * GLA_W].astype(BF16))
    gam = units(jnp.exp(b_last))[:, 0:1, :].reshape(GLA_H, nc, 1, HEAD_DIM)

    ci = lax.broadcasted_iota(jnp.int32, (n, CHUNK, CHUNK), 1)
    cj = lax.broadcasted_iota(jnp.int32, (n, CHUNK, CHUNK), 2)
    d = lambda a, bb, dims: lax.dot_general(a, bb, dims, preferred_element_type=F32)
    att = jnp.where(cj <= ci, d(q_in, k_in, BNT), 0.0)
    o = d(att.astype(BF16), v, BNN)
    kv = d(v, k_end, BTN).reshape(GLA_H, nc, HEAD_DIM, HEAD_DIM)

    s = state_ref[...]
    before = []
    for c in range(nc):
        before.append(s)
        s = s * gam[:, c] + kv[:, c]
    state_ref[...] = s
    s_before = jnp.stack(before, axis=1).reshape(n, HEAD_DIM, HEAD_DIM)
    o = o + d(q_in, s_before.astype(BF16), BNT)
    for h in range(GLA_H):
        os_ref[:, h * HEAD_DIM:(h + 1) * HEAD_DIM] = o[h * nc:(h + 1) * nc].reshape(tile, HEAD_DIM)

    o = os_ref[...]
    ms = _dot_exact_rhs(o * o, hones_ref[...]) * (1.0 / HEAD_DIM)
    o_ref[...] = (o * lax.rsqrt(ms + RMS_EPS) * ng_ref[...] * (og * _sigmoid(og))).astype(o_ref.dtype)


def _gla_mixer(y, seq_len, gate_up, gate_b, norm_g):
    m = y.shape[0]
    tile = GLA_ROWS
    nt = seq_len // tile
    gu = jnp.pad(gate_up, ((0, LANES - gate_up.shape[0]), (0, 0)))
    gb = gate_b.reshape(1, GLA_W)
    ng = jnp.tile(norm_g, GLA_H).reshape(1, GLA_W)
    consts = [gu, gb, ng, *_chunk_selectors(), _head_ones(GLA_W)]
    full = lambda a: pl.BlockSpec(a.shape, lambda b, j: (0, 0))
    return pl.pallas_call(
        functools.partial(_gla_kernel, tile=tile),
        grid=(m // seq_len, nt),
        in_specs=[pl.BlockSpec((tile, GLA_COLS_PAD), lambda b, j: (b * nt + j, 0))] + [full(c) for c in consts],
        out_specs=pl.BlockSpec((tile, GLA_W), lambda b, j: (b * nt + j, 0)),
        out_shape=jax.ShapeDtypeStruct((m, GLA_W), BF16),
        scratch_shapes=[pltpu.VMEM((GLA_H, HEAD_DIM, HEAD_DIM), F32), pltpu.VMEM((tile, GLA_W), F32)],
        compiler_params=_cparams("parallel", "arbitrary"),
        name="gla",
    )(y, *consts)


def _mixed_residual(h_ref, or_ref, oa_ref, og_ref, w_ref):
    acc = jnp.dot(or_ref[...], w_ref[0:RWKV_W, :], preferred_element_type=F32)
    acc += jnp.dot(oa_ref[...].astype(BF16), w_ref[RWKV_W:RWKV_W + ATT_W, :], preferred_element_type=F32)
    acc += jnp.dot(og_ref[...], w_ref[RWKV_W + ATT_W:, :], preferred_element_type=F32)
    return h_ref[...] + acc


def _mixed_specs(tm, index_map, w):
    row = lambda wd: pl.BlockSpec((tm, wd), index_map)
    return [row(D_MODEL), row(RWKV_W), row(ATT_W), row(GLA_W), pl.BlockSpec(w.shape, lambda *_: (0, 0))]


def _ffn_kernel(h_ref, or_ref, oa_ref, og_ref, wo_ref, g_ref, wg_ref, wu_ref, wd_ref, o_ref, xn_ref):
    @pl.when(pl.program_id(1) == 0)
    def _():
        x = _mixed_residual(h_ref, or_ref, oa_ref, og_ref, wo_ref)
        xn_ref[...] = _rms(x, g_ref[...]).astype(xn_ref.dtype)
        o_ref[...] = x

    xn = xn_ref[...]
    gate = jnp.dot(xn, wg_ref[...], preferred_element_type=F32)
    up = jnp.dot(xn, wu_ref[...], preferred_element_type=F32)
    act = (gate * _sigmoid(gate) * up).astype(BF16)
    o_ref[...] += jnp.dot(act, wd_ref[...], preferred_element_type=F32)


def _ffn(h, mixed, w_out, g, wg, wu, wd):
    m = h.shape[0]
    tm, tf = FFN_ROWS, FFN_COLS
    return pl.pallas_call(
        _ffn_kernel,
        grid=(m // tm, FFN_DENSE // tf),
        in_specs=_mixed_specs(tm, lambda i, f: (i, 0), w_out) + [
            pl.BlockSpec(g.shape, lambda i, f: (0, 0)),
            pl.BlockSpec((D_MODEL, tf), lambda i, f: (0, f)),
            pl.BlockSpec((D_MODEL, tf), lambda i, f: (0, f)),
            pl.BlockSpec((tf, D_MODEL), lambda i, f: (f, 0))],
        out_specs=pl.BlockSpec((tm, D_MODEL), lambda i, f: (i, 0)),
        out_shape=jax.ShapeDtypeStruct(h.shape, F32),
        scratch_shapes=[pltpu.VMEM((tm, D_MODEL), BF16)],
        compiler_params=_cparams("parallel", "arbitrary"),
        name="ffn_dense",
    )(h, *mixed, w_out, g, wg, wu, wd)


MOE_BLOCK = 896
MOE_SUB = 256
MOE_FT = 896


def _router_kernel(h_ref, or_ref, oa_ref, og_ref, wo_ref, g_ref, wr_ref, tri_ref,
                   x_ref, xnt_ref, rank_ref, gate_ref, cnt_ref, *, n_tokens):
    row = lax.broadcasted_iota(jnp.int32, (MOE_BLOCK, 1), 0) + pl.program_id(0) * MOE_BLOCK
    real = row < n_tokens
    x = _mixed_residual(h_ref, or_ref, oa_ref, og_ref, wo_ref)
    x_ref[...] = x
    xn = jnp.where(real, _rms(x, g_ref[...]), 0.0)
    xnt_ref[...] = xn.T.astype(BF16)
    logits = _dot3(xn, wr_ref[...])
    lane = lax.broadcasted_iota(jnp.int32, logits.shape, 1)
    lg = jnp.where(lane < N_EXPERTS, logits, NEG_INF)
    m1 = jnp.max(lg, axis=-1, keepdims=True)
    i1 = jnp.min(jnp.where(lg == m1, lane, LANES), axis=-1, keepdims=True)
    lg2 = jnp.where(lane == i1, NEG_INF, lg)
    m2 = jnp.max(lg2, axis=-1, keepdims=True)
    i2 = jnp.min(jnp.where(lg2 == m2, lane, LANES), axis=-1, keepdims=True)
    e2 = jnp.exp(m2 - m1)
    hot1 = (lane == i1) & real
    hot2 = (lane == i2) & real
    onehot = (hot1 | hot2).astype(F32)
    rank = jnp.dot(tri_ref[...], onehot.astype(BF16), preferred_element_type=F32)
    rank = jnp.where(hot1 | hot2, rank, -1.0)
    gate = jnp.where(hot1, 1.0 / (1.0 + e2), jnp.where(hot2, e2 / (1.0 + e2), 0.0))
    rank_ref[0] = rank.T[:N_EXPERTS]
    gate_ref[0] = gate.T[:N_EXPERTS]
    cnt_ref[0] = jnp.broadcast_to(jnp.sum(onehot, axis=0, keepdims=True), (8, LANES)).astype(jnp.int32)


def _route(h, mixed, w_out, g, w_router):
    m = h.shape[0]
    sb = MOE_BLOCK
    nblk = pl.cdiv(m, sb)
    t = jnp.arange(sb)
    tri = (t[None, :] < t[:, None]).astype(BF16)
    full = lambda a: pl.BlockSpec(a.shape, lambda i: (0,) * a.ndim)
    per_expert = pl.BlockSpec((1, N_EXPERTS, sb), lambda i: (i, 0, 0))
    return pl.pallas_call(
        functools.partial(_router_kernel, n_tokens=m),
        grid=(nblk,),
        in_specs=_mixed_specs(sb, lambda i: (i, 0), w_out) + [full(g), full(w_router), full(tri)],
        out_specs=[pl.BlockSpec((sb, D_MODEL), lambda i: (i, 0)), pl.BlockSpec((D_MODEL, sb), lambda i: (0, i)),
                   per_expert, per_expert, pl.BlockSpec((1, 8, LANES), lambda i: (i, 0, 0))],
        out_shape=[jax.ShapeDtypeStruct((m, D_MODEL), F32), jax.ShapeDtypeStruct((D_MODEL, nblk * sb), BF16),
                   jax.ShapeDtypeStruct((nblk, N_EXPERTS, sb), F32), jax.ShapeDtypeStruct((nblk, N_EXPERTS, sb), F32),
                   jax.ShapeDtypeStruct((nblk, 8, LANES), jnp.int32)],
        compiler_params=_cparams("parallel"),
        name="moe_router",
    )(h, *mixed, w_out, g, w_router, tri)


def _moe_kernel(cnt_ref, x_ref, xnt_ref, rank_ref, gate_ref, wg_ref, wu_ref, wd_ref, fg_ref, o_ref,
                xs_ref, y_ref, acc_ref, *, final_norm):
    i, e, f = pl.program_id(0), pl.program_id(1), pl.program_id(2)
    last_f = f == pl.num_programs(2) - 1
    n_sub = (cnt_ref[i * N_EXPERTS + e] + (MOE_SUB - 1)) // MOE_SUB

    @pl.when((e == 0) & (f == 0))
    def _():
        acc_ref[...] = jnp.zeros_like(acc_ref)

    def selection(j):
        want = lax.broadcasted_iota(jnp.int32, (MOE_SUB, MOE_BLOCK), 0) + j * MOE_SUB
        return (rank_ref[0, pl.ds(e, 1), :] == want.astype(F32)).astype(BF16)

    def sub_body(j, carry):
        @pl.when(f == 0)
        def _():
            xs_ref[j] = lax.dot_general(xnt_ref[...], selection(j), NT, preferred_element_type=F32).astype(BF16)
            y_ref[j] = jnp.zeros((D_MODEL, MOE_SUB), F32)

        xs = xs_ref[j]
        gate = lax.dot_general(wg_ref[0], xs, TN, preferred_element_type=F32)
        up = lax.dot_general(wu_ref[0], xs, TN, preferred_element_type=F32)
        act = (gate * _sigmoid(gate) * up).astype(BF16)
        y_ref[j] += lax.dot_general(wd_ref[0], act, TN, preferred_element_type=F32)

        @pl.when(last_f)
        def _():
            back = jnp.dot(y_ref[j].astype(BF16), selection(j), preferred_element_type=F32)
            acc_ref[...] += gate_ref[0, pl.ds(e, 1), :] * back

        return carry

    lax.fori_loop(0, n_sub, sub_body, 0)

    @pl.when((e == N_EXPERTS - 1) & last_f)
    def _():
        y = x_ref[...] + acc_ref[...].T
        o_ref[...] = _rms(y, fg_ref[...]) if final_norm else y


def _moe(h, mixed, w_out, g, w_router, wg, wu, wd, final_g, final_norm):
    m = h.shape[0]
    sb, ft = MOE_BLOCK, MOE_FT
    h, xnt, rank, gate, cnt = _route(h, mixed, w_out, g, w_router)
    counts = cnt[:, 0, :N_EXPERTS].reshape(-1)
    row = pl.BlockSpec((sb, D_MODEL), lambda i, e, f, c: (i, 0))
    per_expert = pl.BlockSpec((1, N_EXPERTS, sb), lambda i, e, f, c: (i, 0, 0))
    grid_spec = pltpu.PrefetchScalarGridSpec(
        num_scalar_prefetch=1,
        grid=(pl.cdiv(m, sb), N_EXPERTS, FFN_EXPERT // ft),
        in_specs=[row, pl.BlockSpec((D_MODEL, sb), lambda i, e, f, c: (0, i)), per_expert, per_expert,
                  pl.BlockSpec((1, D_MODEL, ft), lambda i, e, f, c: (e, 0, f)),
                  pl.BlockSpec((1, D_MODEL, ft), lambda i, e, f, c: (e, 0, f)),
                  pl.BlockSpec((1, ft, D_MODEL), lambda i, e, f, c: (e, f, 0)),
                  pl.BlockSpec((1, D_MODEL), lambda i, e, f, c: (0, 0))],
        out_specs=row,
        scratch_shapes=[pltpu.VMEM((pl.cdiv(sb, MOE_SUB), D_MODEL, MOE_SUB), BF16),
                        pltpu.VMEM((pl.cdiv(sb, MOE_SUB), D_MODEL, MOE_SUB), F32),
                        pltpu.VMEM((D_MODEL, sb), F32)],
    )
    return pl.pallas_call(
        functools.partial(_moe_kernel, final_norm=final_norm),
        grid_spec=grid_spec,
        out_shape=jax.ShapeDtypeStruct(h.shape, F32),
        compiler_params=_cparams("parallel", "arbitrary", "arbitrary"),
        name="moe",
    )(counts, h, xnt, rank, gate, wg, wu, wd, final_g)


def kernel(x, positions, mix_norm_g, w_in, rwkv_mu, rwkv_w0, rwkv_w2, rwkv_a0, rwkv_a2, rwkv_g2, rwkv_k_k, rwkv_k_a, rwkv_r_k, rwkv_ln_g, rwkv_ln_b, rwkv_v0, rwkv_v1, rwkv_v2, gla_gate_up, gla_gate_b, gla_norm_g, w_out, ffn_norm_g, ffn_w_gate, ffn_w_up, ffn_w_down, moe_router, moe_w_gate, moe_w_up, moe_w_down, final_norm_g):
    B, S, D = x.shape
    M = B * S
    h = x.reshape(M, D)
    cos, sa, sb = _rope_tables(positions)
    v_first = None
    for layer in range(DEPTH):
        w = w_in[layer]
        wr = w[:, :RWKV_COLS].astype(BF16)
        wa = w[:, RWKV_COLS:RWKV_COLS + ATT_COLS].astype(BF16)
        wgl = w[:, RWKV_COLS + ATT_COLS:]
        wgl = jnp.concatenate([wgl[:, :3 * GLA_W], wgl[:, 3 * GLA_W + GLA_GATE_RANK:],
                               wgl[:, 3 * GLA_W:3 * GLA_W + GLA_GATE_RANK],
                               jnp.zeros((D, LANES - GLA_GATE_RANK), w.dtype)], axis=1).astype(BF16)
        y_rwkv, y_att, y_gla = _inproj(h, mix_norm_g[layer].reshape(1, D), wr, wa, wgl, cos, sa, sb)
        vres = None if layer == 0 else (rwkv_v0[layer - 1], rwkv_v1[layer - 1], rwkv_v2[layer - 1])
        o_rwkv, v_first = _rwkv_mixer(y_rwkv, S, rwkv_mu[layer], rwkv_w0[layer], rwkv_w2[layer], rwkv_a0[layer],
                                      rwkv_a2[layer], rwkv_g2[layer], rwkv_k_k[layer], rwkv_k_a[layer],
                                      rwkv_r_k[layer], rwkv_ln_g[layer], rwkv_ln_b[layer], v_first, vres)
        o_att = _att_mixer(y_att, S)
        o_gla = _gla_mixer(y_gla, S, gla_gate_up[layer], gla_gate_b[layer], gla_norm_g[layer])
        mixed = (o_rwkv, o_att, o_gla)
        wo = w_out[layer].astype(BF16)
        i = layer // 2
        g = ffn_norm_g[layer].reshape(1, D)
        if layer % 2 == 0:
            h = _ffn(h, mixed, wo, g, ffn_w_gate[i].astype(BF16), ffn_w_up[i].astype(BF16),
                     ffn_w_down[i].astype(BF16))
        else:
            wrt = jnp.pad(moe_router[i], ((0, 0), (0, LANES - N_EXPERTS)))
            h = _moe(h, mixed, wo, g, wrt, moe_w_gate[i].astype(BF16), moe_w_up[i].astype(BF16),
                     moe_w_down[i].astype(BF16), final_norm_g.reshape(1, D), layer == DEPTH - 1)
    return h.reshape(B, S, D)
```

```python
import functools

import jax
import jax.numpy as jnp
from jax import lax
from jax.experimental import pallas as pl
from jax.experimental.pallas import tpu as pltpu

D_MODEL = 1024
DEPTH = 2
HEAD_DIM = 64
RWKV_W = 256
RWKV_H = 4
ATT_W = 384
GLA_W = 384
GLA_H = 6
RWKV_DECAY_RANK = 32
RWKV_AAA_RANK = 32
RWKV_GATE_RANK = 64
RWKV_GN_EPS = 64e-5
GLA_GATE_RANK = 16
GLA_GATE_TAU = 16.0
CHUNK = 64
DILATIONS = (1, 4, 16)
ATT_BLOCK = 128
ROPE_THETA = 500000.0
ROPE_DIMS = 16
ROPE_HALF = 8
FFN_DENSE = 2816
N_EXPERTS = 8
FFN_EXPERT = 3584
RMS_EPS = 1e-5
NEG_INF = -1e30

RWKV_COLS = 3 * RWKV_W + RWKV_DECAY_RANK + RWKV_AAA_RANK + RWKV_GATE_RANK
ATT_COLS = 3 * ATT_W
LANES = 128
GLA_COLS_PAD = 4 * GLA_W + LANES

VMEM_LIMIT = 56 * 1024 * 1024

ROPE_ROWS = 2048
INPROJ_ROWS = 1024
RWKV_ROWS = 512
GLA_ROWS = 1024
OUTPROJ_ROWS = 1024
FFN_ROWS = 1024
FFN_COLS = FFN_DENSE // 2

F32 = jnp.float32
BF16 = jnp.bfloat16
NN = (((1,), (0,)), ((), ()))
NT = (((1,), (1,)), ((), ()))
TN = (((0,), (0,)), ((), ()))
BNN = (((2,), (1,)), ((0,), (0,)))
BNT = (((2,), (2,)), ((0,), (0,)))
BTN = (((1,), (1,)), ((0,), (0,)))


def _cparams(*sem):
    return pltpu.CompilerParams(dimension_semantics=sem, vmem_limit_bytes=VMEM_LIMIT)


def _rms(x, g):
    return x * lax.rsqrt(jnp.mean(x * x, axis=-1, keepdims=True) + RMS_EPS) * g


def _sigmoid(x):
    return 1.0 / (1.0 + jnp.exp(-x))


def _log_sigmoid(x):
    return jnp.minimum(x, 0.0) - jnp.log(1.0 + jnp.exp(-jnp.abs(x)))


def _split_hi_lo(a):
    hi = a.astype(BF16)
    return hi, (a - hi.astype(F32)).astype(BF16)


def _dot3(a, b, dims=NN):
    a_hi, a_lo = _split_hi_lo(a)
    b_hi, b_lo = _split_hi_lo(b)
    d = lambda x, y: lax.dot_general(x, y, dims, preferred_element_type=F32)
    return d(a_hi, b_hi) + d(a_lo, b_hi) + d(a_hi, b_lo)


def _dot_exact_rhs(x, r_bf16):
    x_hi, x_lo = _split_hi_lo(x)
    return jnp.dot(x_hi, r_bf16, preferred_element_type=F32) + jnp.dot(x_lo, r_bf16, preferred_element_type=F32)


def _bdot(a, b, dims, passes):
    d = lambda x, y: lax.dot_general(x, y, dims, preferred_element_type=F32)
    if passes == 1:
        return d(a.astype(BF16), b.astype(BF16))
    a_hi, a_lo = _split_hi_lo(a)
    b_hi, b_lo = _split_hi_lo(b)
    return d(a_hi, b_hi) + d(a_lo, b_hi) + d(a_hi, b_lo)


def _head_ones(width):
    idx = jnp.arange(width) // HEAD_DIM
    return (idx[:, None] == idx[None, :]).astype(BF16)


def _chunk_selectors():
    t = jnp.arange(CHUNK)
    return (t[None, :] <= t[:, None]).astype(BF16), jnp.ones((CHUNK, CHUNK), BF16)


def _chunk_sums(sel, x):
    nc = x.shape[0] // CHUNK
    hi, lo = _split_hi_lo(x.reshape(nc, CHUNK, x.shape[1]))
    s3 = jnp.broadcast_to(sel[None], (nc, CHUNK, CHUNK))
    d = lambda a, b: lax.dot_general(a, b, BNN, preferred_element_type=F32)
    return (d(s3, hi) + d(s3, lo)).reshape(x.shape)


def _rope_table_kernel(pos_ref, invf_ref, cos_ref, sa_ref, sb_ref):
    ang = pos_ref[...].astype(F32) * invf_ref[...]
    lane = lax.broadcasted_iota(jnp.int32, ang.shape, 1) % HEAD_DIM
    c = jnp.cos(ang)
    s = jnp.sin(ang)
    cos_ref[...] = jnp.where(lane < ROPE_DIMS, c, 1.0)
    sa_ref[...] = jnp.where((lane >= ROPE_HALF) & (lane < ROPE_DIMS), s, 0.0)
    sb_ref[...] = jnp.where(lane < ROPE_HALF, -s, 0.0)


def _rope_tables(positions):
    m = positions.size
    tm = ROPE_ROWS
    lane = jnp.arange(LANES) % ROPE_HALF
    invf = (ROPE_THETA ** (-lane.astype(F32) / ROPE_HALF)).reshape(1, LANES)
    out = jax.ShapeDtypeStruct((m, LANES), F32)
    row = pl.BlockSpec((tm, LANES), lambda i: (i, 0))
    return pl.pallas_call(
        _rope_table_kernel,
        grid=(m // tm,),
        in_specs=[pl.BlockSpec((tm, 1), lambda i: (i, 0)), pl.BlockSpec((1, LANES), lambda i: (0, 0))],
        out_specs=[row, row, row],
        out_shape=[out, out, out],
        compiler_params=_cparams("parallel"),
        name="rope_tables",
    )(positions.reshape(m, 1), invf)


def _inproj_kernel(x_ref, g_ref, wr_ref, wa_ref, wg_ref, cos_ref, sa_ref, sb_ref, yr_ref, ya_ref, yg_ref):
    xb = _rms(x_ref[...], g_ref[...]).astype(BF16)
    yr_ref[...] = jnp.dot(xb, wr_ref[...], preferred_element_type=F32)
    yg_ref[...] = jnp.dot(xb, wg_ref[...], preferred_element_type=F32)
    ya = jnp.dot(xb, wa_ref[...], preferred_element_type=F32)
    cos, sa, sb = cos_ref[...], sa_ref[...], sb_ref[...]
    for j in range(2 * ATT_W // LANES):
        blk = ya[:, j * LANES:(j + 1) * LANES]
        rot = blk * cos + pltpu.roll(blk, ROPE_HALF, 1) * sa + pltpu.roll(blk, LANES - ROPE_HALF, 1) * sb
        if j < ATT_W // LANES:
            rot = rot * HEAD_DIM ** -0.5
        ya_ref[:, j * LANES:(j + 1) * LANES] = rot.astype(ya_ref.dtype)
    ya_ref[:, 2 * ATT_W:] = ya[:, 2 * ATT_W:].astype(ya_ref.dtype)


def _inproj(h, g, wr, wa, wg, cos, sa, sb):
    m = h.shape[0]
    tm = INPROJ_ROWS
    row = lambda w: pl.BlockSpec((tm, w), lambda i: (i, 0))
    full = lambda a: pl.BlockSpec(a.shape, lambda i: (0, 0))
    return pl.pallas_call(
        _inproj_kernel,
        grid=(m // tm,),
        in_specs=[row(D_MODEL), full(g), full(wr), full(wa), full(wg), row(LANES), row(LANES), row(LANES)],
        out_specs=[row(RWKV_COLS), row(ATT_COLS), row(GLA_COLS_PAD)],
        out_shape=[jax.ShapeDtypeStruct((m, RWKV_COLS), F32),
                   jax.ShapeDtypeStruct((m, ATT_COLS), BF16),
                   jax.ShapeDtypeStruct((m, GLA_COLS_PAD), F32)],
        compiler_params=_cparams("parallel"),
        name="inproj",
    )(h, g, wr, wa, wg, cos, sa, sb)


def _rwkv_kernel(*refs, has_vres, tile):
    nc = tile // CHUNK
    if has_vres:
        (y_ref, vf_ref, mu_ref, w0_ref, w2_ref, a0_ref, a2_ref, g2_ref, kk_ref, ka_ref, rk_ref, lng_ref, lnb_ref,
         psel_ref, tsel_ref, hones_ref, v0_ref, v1_ref, v2_ref, o_ref, *scratch) = refs
    else:
        (y_ref, mu_ref, w0_ref, w2_ref, a0_ref, a2_ref, g2_ref, kk_ref, ka_ref, rk_ref, lng_ref, lnb_ref,
         psel_ref, tsel_ref, hones_ref, o_ref, vfo_ref, *scratch) = refs
    state_ref, prev_ref, os_ref = scratch

    @pl.when(pl.program_id(1) == 0)
    def _():
        state_ref[...] = jnp.zeros_like(state_ref)
        prev_ref[...] = jnp.zeros_like(prev_ref)

    y = y_ref[...]
    row = lax.broadcasted_iota(jnp.int32, (tile, 1), 0)
    ysh = jnp.where(row == 0, prev_ref[...], pltpu.roll(y, 1, 0))
    prev_ref[...] = y[tile - 1:tile, :]
    ym = y + (ysh - y) * mu_ref[...]
    r = ym[:, 0:RWKV_W]
    k = ym[:, RWKV_W:2 * RWKV_W]
    v = ym[:, 2 * RWKV_W:3 * RWKV_W]
    x6 = ym[:, 3 * RWKV_W:]
    wl = w0_ref[...] + _dot3(jnp.tanh(x6), w2_ref[...])
    lw = -jnp.exp(_log_sigmoid(wl) - 0.5)
    a = _sigmoid(a0_ref[...] + _dot3(x6, a2_ref[...]))
    g = _dot3(_sigmoid(x6), g2_ref[...])
    if has_vres:
        v = v + (vf_ref[...] - v) * _sigmoid(v0_ref[...] + _dot3(_dot3(v, v1_ref[...]), v2_ref[...]))
    else:
        vfo_ref[...] = v

    head_ones = hones_ref[...]
    kk = k * kk_ref[...]
    kk = kk * lax.rsqrt(jnp.maximum(_dot_exact_rhs(kk * kk, head_ones), 1e-24))
    k2 = k * (1.0 + (a - 1.0) * ka_ref[...])
    bonus = _dot_exact_rhs(r * k2 * rk_ref[...], head_ones) * v

    c = _chunk_sums(psel_ref[...], lw)
    c_last = _chunk_sums(tsel_ref[...], lw)
    b = kk * a
    e_neg = jnp.exp(-c)
    e_end = jnp.exp(c_last - c)

    def units(val):
        return jnp.concatenate([val[:, h * HEAD_DIM:(h + 1) * HEAD_DIM].reshape(nc, CHUNK, HEAD_DIM)
                                for h in range(RWKV_H)], axis=0)

    rt = units(r * jnp.exp(c))
    at = units(-kk * jnp.exp(c - lw))
    bt = units(b * e_neg)
    kt = units(k2 * e_neg)
    bh = units(b * e_end)
    kh = units(k2 * e_end)
    vv = units(v)
    gam = units(jnp.exp(c_last))[:, 0:1, :]

    n = RWKV_H * nc
    qi = lax.broadcasted_iota(jnp.int32, (n, CHUNK, CHUNK), 1)
    qj = lax.broadcasted_iota(jnp.int32, (n, CHUNK, CHUNK), 2)
    strict = qj < qi
    incl = qj <= qi
    eye = qj == qi
    same_block = [qi // size == qj // size for size in (2, 4, 8, 16, 32, CHUNK)]
    a_ab = jnp.where(strict, _bdot(at, bt, BNT, 1), 0.0)
    a_ak = jnp.where(strict, _bdot(at, kt, BNT, 1), 0.0)
    p_b = jnp.where(incl, _bdot(rt, bt, BNT, 1), 0.0)
    p_k = jnp.where(incl, _bdot(rt, kt, BNT, 1), 0.0)
    t = jnp.where(eye, 1.0, jnp.where(same_block[0], a_ab, 0.0))
    for small, big in zip(same_block[:-1], same_block[1:]):
        t = t + _bdot(_bdot(t, jnp.where(big & ~small, a_ab, 0.0), BNN, 1), t, BNN, 1)
    w = _bdot(t, at, BNN, 1)
    u_loc = _bdot(t, _bdot(a_ak, vv, BNN, 1), BNN, 1)
    q_t = rt + _bdot(p_b, w, BNN, 1)
    o_loc = _bdot(p_b, u_loc, BNN, 1) + _bdot(p_k, vv, BNN, 1)
    per_chunk = lambda x: x.reshape(RWKV_H, nc, CHUNK, HEAD_DIM)
    g_mat = per_chunk(_bdot(w, bh, BTN, 1) + jnp.where(eye, gam, 0.0))
    h_mat = per_chunk(_bdot(u_loc, bh, BTN, 1) + _bdot(vv, kh, BTN, 1))

    s = state_ref[...]
    before = []
    for ci in range(nc):
        before.append(s)
        s = _bdot(s, g_mat[:, ci], BNN, 3) + h_mat[:, ci]
    state_ref[...] = s

    o = _bdot(q_t, jnp.stack(before, axis=1).reshape(n, HEAD_DIM, HEAD_DIM), BNT, 3) + o_loc
    for h in range(RWKV_H):
        os_ref[:, h * HEAD_DIM:(h + 1) * HEAD_DIM] = o[h * nc:(h + 1) * nc].reshape(tile, HEAD_DIM)

    o = os_ref[...]
    mean = _dot_exact_rhs(o, head_ones) * (1.0 / HEAD_DIM)
    oc = o - mean
    var = _dot_exact_rhs(oc * oc, head_ones) * (1.0 / HEAD_DIM)
    on = oc * lax.rsqrt(var + RWKV_GN_EPS) * lng_ref[...] + lnb_ref[...]
    o_ref[...] = ((on + bonus) * g).astype(o_ref.dtype)


def _rwkv_mixer(y, seq_len, mu, w0, w2, a0, a2, g2, k_k, k_a, r_k, ln_g, ln_b, v_first, vres):
    m = y.shape[0]
    tile = RWKV_ROWS
    nt = seq_len // tile
    has_vres = vres is not None
    row = lambda wd: pl.BlockSpec((tile, wd), lambda b, j: (b * nt + j, 0))
    full = lambda arr: pl.BlockSpec(arr.shape, lambda b, j: (0, 0))
    vec = lambda t: t.reshape(1, -1)
    pad_rows = lambda w, start: jnp.zeros((LANES, RWKV_W), F32).at[start:start + w.shape[0]].set(w)
    consts = [vec(mu), vec(w0), pad_rows(w2, 0), vec(a0), pad_rows(a2, RWKV_DECAY_RANK),
              pad_rows(g2, RWKV_DECAY_RANK + RWKV_AAA_RANK), vec(k_k), vec(k_a), vec(r_k), vec(ln_g), vec(ln_b),
              *_chunk_selectors(), _head_ones(RWKV_W)]
    args = [y] + ([v_first] if has_vres else []) + consts
    in_specs = [row(RWKV_COLS)] + ([row(RWKV_W)] if has_vres else []) + [full(c) for c in consts]
    out_shape = [jax.ShapeDtypeStruct((m, RWKV_W), BF16)]
    out_specs = [row(RWKV_W)]
    if has_vres:
        v0, v1, v2 = vres
        extra = [vec(v0), jnp.pad(v1, ((0, 0), (0, LANES - v1.shape[1]))),
                 jnp.pad(v2, ((0, LANES - v2.shape[0]), (0, 0)))]
        args += extra
        in_specs += [full(c) for c in extra]
    else:
        out_shape.append(jax.ShapeDtypeStruct((m, RWKV_W), F32))
        out_specs.append(row(RWKV_W))
    outs = pl.pallas_call(
        functools.partial(_rwkv_kernel, has_vres=has_vres, tile=tile),
        grid=(m // seq_len, nt),
        in_specs=in_specs, out_specs=out_specs, out_shape=out_shape,
        scratch_shapes=[pltpu.VMEM((RWKV_H, HEAD_DIM, HEAD_DIM), F32), pltpu.VMEM((1, RWKV_COLS), F32),
                        pltpu.VMEM((tile, RWKV_W), F32)],
        compiler_params=_cparams("parallel", "arbitrary"),
        name="rwkv7",
    )(*args)
    return (outs[0], v_first) if has_vres else (outs[0], outs[1])


ATT_PAD = ATT_BLOCK * max(DILATIONS)
ATT_UNROLL = 8


def _att_kernel(q_ref, k_ref, v_ref, o_ref, qs_ref, ks_ref, vs_ref, m_ref, n_ref, d_ref, *, seq_len):
    qs_ref[...] = q_ref[...].astype(F32)
    ks_ref[0:ATT_PAD, :] = jnp.zeros((ATT_PAD, LANES), F32)
    vs_ref[0:ATT_PAD, :] = jnp.zeros((ATT_PAD, LANES), F32)
    ks_ref[ATT_PAD:, :] = k_ref[...].astype(F32)
    vs_ref[ATT_PAD:, :] = v_ref[...].astype(F32)

    qi = lax.broadcasted_iota(jnp.int32, (ATT_BLOCK, 2 * ATT_BLOCK), 0)
    kc = lax.broadcasted_iota(jnp.int32, (ATT_BLOCK, 2 * ATT_BLOCK), 1)
    band = (kc >= qi) & (kc <= qi + ATT_BLOCK)
    head0 = lax.broadcasted_iota(jnp.int32, (ATT_BLOCK, LANES), 1) < HEAD_DIM

    def block(g, dil, first, q, k, v, rows):
        valid = band & ((kc >= ATT_BLOCK) | jnp.logical_not(first))
        v1 = jnp.concatenate([v, jnp.ones_like(v)], axis=1)
        mxs, pvs = [], []
        for hh in range(2):
            qh = jnp.where(head0 if hh == 0 else jnp.logical_not(head0), q, jnp.zeros_like(q))
            s = jnp.where(valid, lax.dot_general(qh, k, NT, preferred_element_type=F32), NEG_INF)
            mx = jnp.max(s, axis=-1, keepdims=True)
            mxs.append(mx)
            pvs.append(jnp.dot(jnp.exp(s - mx).astype(BF16), v1, preferred_element_type=F32))
        mx = jnp.where(head0, mxs[0], mxs[1])
        pv = jnp.where(head0, pvs[0][:, :LANES], pvs[1][:, :LANES])
        den = jnp.where(head0, pvs[0][:, LANES:], pvs[1][:, LANES:])
        if g > 0:
            m_old = m_ref[rows, :]
            m_new = jnp.maximum(m_old, mx)
            a_old = jnp.exp(m_old - m_new)
            a_new = jnp.exp(mx - m_new)
            pv = a_old * n_ref[rows, :] + a_new * pv
            den = a_old * d_ref[rows, :] + a_new * den
            mx = m_new
        if g == len(DILATIONS) - 1:
            o_ref[rows, :] = pv / den
        else:
            m_ref[rows, :] = mx
            n_ref[rows, :] = pv
            d_ref[rows, :] = den

    for g, dil in enumerate(DILATIONS):
        npairs = seq_len // dil // ATT_BLOCK // 2
        step = ATT_BLOCK * dil

        def body(it, carry, g=g, dil=dil, npairs=npairs, step=step):
            r = it // npairs
            jp = it % npairs
            start = r + jp * (2 * step)
            q = qs_ref[pl.ds(start, 2 * ATT_BLOCK, stride=dil), :].astype(BF16)
            k = ks_ref[pl.ds(start + (ATT_PAD - step), 3 * ATT_BLOCK, stride=dil), :].astype(BF16)
            v = vs_ref[pl.ds(start + (ATT_PAD - step), 3 * ATT_BLOCK, stride=dil), :].astype(BF16)
            for u in range(2):
                block(g, dil, (jp == 0) if u == 0 else False, q[u * ATT_BLOCK:(u + 1) * ATT_BLOCK],
                      k[u * ATT_BLOCK:(u + 2) * ATT_BLOCK], v[u * ATT_BLOCK:(u + 2) * ATT_BLOCK],
                      pl.ds(start + u * step, ATT_BLOCK, stride=dil))
            return carry

        lax.fori_loop(0, dil * npairs, body, 0, unroll=ATT_UNROLL)


def _att_mixer(y_att, seq_len):
    m = y_att.shape[0]
    npair = ATT_W // LANES
    col = lambda c: pl.BlockSpec((seq_len, LANES), lambda b, hp: (b, c * npair + hp))
    seq_buf = pltpu.VMEM((seq_len, LANES), F32)
    pad_buf = pltpu.VMEM((ATT_PAD + seq_len, LANES), F32)
    return pl.pallas_call(
        functools.partial(_att_kernel, seq_len=seq_len),
        grid=(m // seq_len, npair),
        in_specs=[col(0), col(1), col(2)],
        out_specs=pl.BlockSpec((seq_len, LANES), lambda b, hp: (b, hp)),
        out_shape=jax.ShapeDtypeStruct((m, ATT_W), F32),
        scratch_shapes=[seq_buf, pad_buf, pad_buf, seq_buf, seq_buf, seq_buf],
        compiler_params=_cparams("parallel", "parallel"),
        name="dilated_att",
    )(y_att, y_att, y_att)


def _gla_kernel(y_ref, gu_ref, gb_ref, ng_ref, psel_ref, tsel_ref, hones_ref, o_ref, state_ref, os_ref, *, tile):
    nc = tile // CHUNK
    n = GLA_H * nc

    @pl.when(pl.program_id(1) == 0)
    def _():
        state_ref[...] = jnp.zeros_like(state_ref)

    def units(val):
        return jnp.concatenate([val[:, h * HEAD_DIM:(h + 1) * HEAD_DIM].reshape(nc, CHUNK, HEAD_DIM)
                                for h in range(GLA_H)], axis=0)

    q = y_ref[:, 0:GLA_W] * HEAD_DIM ** -0.5
    k = y_ref[:, GLA_W:2 * GLA_W]
    og = y_ref[:, 3 * GLA_W:4 * GLA_W]
    lg = _log_sigmoid(_dot3(y_ref[:, 4 * GLA_W:], gu_ref[...]) + gb_ref[...]) * (1.0 / GLA_GATE_TAU)
    b = _chunk_sums(psel_ref[...], lg)
    b_last = _chunk_sums(tsel_ref[...], lg)
    q_in = units((q * jnp.exp(b)).astype(BF16))
    k_in = units((k * jnp.exp(-b)).astype(BF16))
    k_end = units((k * jnp.exp(b_last - b)).astype(BF16))
    v = units(y_ref[:, 2 * GLA_W:3 * GLA_W].astype(BF16))
    gam = units(jnp.exp(b_last))[:, 0:1, :].reshape(GLA_H, nc, 1, HEAD_DIM)

    ci = lax.broadcasted_iota(jnp.int32, (n, CHUNK, CHUNK), 1)
    cj = lax.broadcasted_iota(jnp.int32, (n, CHUNK, CHUNK), 2)
    d = lambda a, bb, dims: lax.dot_general(a, bb, dims, preferred_element_type=F32)
    att = jnp.where(cj <= ci, d(q_in, k_in, BNT), 0.0)
    o = d(att.astype(BF16), v, BNN)
    kv = d(v, k_end, BTN).reshape(GLA_H, nc, HEAD_DIM, HEAD_DIM)

    s = state_ref[...]
    before = []
    for c in range(nc):
        before.append(s)
        s = s * gam[:, c] + kv[:, c]
    state_ref[...] = s
    s_before = jnp.stack(before, axis=1).reshape(n, HEAD_DIM, HEAD_DIM)
    o = o + d(q_in, s_before.astype(BF16), BNT)
    for h in range(GLA_H):
        os_ref[:, h * HEAD_DIM:(h + 1) * HEAD_DIM] = o[h * nc:(h + 1) * nc].reshape(tile, HEAD_DIM)

    o = os_ref[...]
    ms = _dot_exact_rhs(o * o, hones_ref[...]) * (1.0 / HEAD_DIM)
    o_ref[...] = (o * lax.rsqrt(ms + RMS_EPS) * ng_ref[...] * (og * _sigmoid(og))).astype(o_ref.dtype)


def _gla_mixer(y, seq_len, gate_up, gate_b, norm_g):
    m = y.shape[0]
    tile = GLA_ROWS
    nt = seq_len // tile
    gu = jnp.pad(gate_up, ((0, LANES - gate_up.shape[0]), (0, 0)))
    gb = gate_b.reshape(1, GLA_W)
    ng = jnp.tile(norm_g, GLA_H).reshape(1, GLA_W)
    consts = [gu, gb, ng, *_chunk_selectors(), _head_ones(GLA_W)]
    full = lambda a: pl.BlockSpec(a.shape, lambda b, j: (0, 0))
    return pl.pallas_call(
        functools.partial(_gla_kernel, tile=tile),
        grid=(m // seq_len, nt),
        in_specs=[pl.BlockSpec((tile, GLA_COLS_PAD), lambda b, j: (b * nt + j, 0))] + [full(c) for c in consts],
        out_specs=pl.BlockSpec((tile, GLA_W), lambda b, j: (b * nt + j, 0)),
        out_shape=jax.ShapeDtypeStruct((m, GLA_W), BF16),
        scratch_shapes=[pltpu.VMEM((GLA_H, HEAD_DIM, HEAD_DIM), F32), pltpu.VMEM((tile, GLA_W), F32)],
        compiler_params=_cparams("parallel", "arbitrary"),
        name="gla",
    )(y, *consts)


def _mixed_residual(h_ref, or_ref, oa_ref, og_ref, w_ref):
    acc = jnp.dot(or_ref[...], w_ref[0:RWKV_W, :], preferred_element_type=F32)
    acc += jnp.dot(oa_ref[...].astype(BF16), w_ref[RWKV_W:RWKV_W + ATT_W, :], preferred_element_type=F32)
    acc += jnp.dot(og_ref[...], w_ref[RWKV_W + ATT_W:, :], preferred_element_type=F32)
    return h_ref[...] + acc


def _mixed_specs(tm, index_map, w):
    row = lambda wd: pl.BlockSpec((tm, wd), index_map)
    return [row(D_MODEL), row(RWKV_W), row(ATT_W), row(GLA_W), pl.BlockSpec(w.shape, lambda *_: (0, 0))]


def _outproj_kernel(h_ref, or_ref, oa_ref, og_ref, w_ref, o_ref):
    o_ref[...] = _mixed_residual(h_ref, or_ref, oa_ref, og_ref, w_ref)


def _outproj(h, mixed, w_out):
    tm = OUTPROJ_ROWS
    return pl.pallas_call(
        _outproj_kernel,
        grid=(h.shape[0] // tm,),
        in_specs=_mixed_specs(tm, lambda i: (i, 0), w_out),
        out_specs=pl.BlockSpec((tm, D_MODEL), lambda i: (i, 0)),
        out_shape=jax.ShapeDtypeStruct(h.shape, F32),
        compiler_params=_cparams("parallel"),
        name="outproj",
    )(h, *mixed, w_out)


def _ffn_kernel(h_ref, or_ref, oa_ref, og_ref, wo_ref, g_ref, wg_ref, wu_ref, wd_ref, o_ref, xn_ref):
    @pl.when(pl.program_id(1) == 0)
    def _():
        x = _mixed_residual(h_ref, or_ref, oa_ref, og_ref, wo_ref)
        xn_ref[...] = _rms(x, g_ref[...]).astype(xn_ref.dtype)
        o_ref[...] = x

    xn = xn_ref[...]
    gate = jnp.dot(xn, wg_ref[...], preferred_element_type=F32)
    up = jnp.dot(xn, wu_ref[...], preferred_element_type=F32)
    act = (gate * _sigmoid(gate) * up).astype(BF16)
    o_ref[...] += jnp.dot(act, wd_ref[...], preferred_element_type=F32)


def _ffn(h, mixed, w_out, g, wg, wu, wd):
    m = h.shape[0]
    tm, tf = FFN_ROWS, FFN_COLS
    return pl.pallas_call(
        _ffn_kernel,
        grid=(m // tm, FFN_DENSE // tf),
        in_specs=_mixed_specs(tm, lambda i, f: (i, 0), w_out) + [
            pl.BlockSpec(g.shape, lambda i, f: (0, 0)),
            pl.BlockSpec((D_MODEL, tf), lambda i, f: (0, f)),
            pl.BlockSpec((D_MODEL, tf), lambda i, f: (0, f)),
            pl.BlockSpec((tf, D_MODEL), lambda i, f: (f, 0))],
        out_specs=pl.BlockSpec((tm, D_MODEL), lambda i, f: (i, 0)),
        out_shape=jax.ShapeDtypeStruct(h.shape, F32),
        scratch_shapes=[pltpu.VMEM((tm, D_MODEL), BF16)],
        compiler_params=_cparams("parallel", "arbitrary"),
        name="ffn_dense",
    )(h, *mixed, w_out, g, wg, wu, wd)


MOE_BLOCK = 896
MOE_SUB = 256
MOE_FT = 896


def _router_kernel(x_ref, g_ref, wr_ref, tri_ref, xnt_ref, rank_ref, gate_ref, cnt_ref, *, n_tokens):
    row = lax.broadcasted_iota(jnp.int32, (MOE_BLOCK, 1), 0) + pl.program_id(0) * MOE_BLOCK
    real = row < n_tokens
    xn = jnp.where(real, _rms(x_ref[...], g_ref[...]), 0.0)
    xnt_ref[...] = xn.T.astype(BF16)
    logits = _dot3(xn, wr_ref[...])
    lane = lax.broadcasted_iota(jnp.int32, logits.shape, 1)
    lg = jnp.where(lane < N_EXPERTS, logits, NEG_INF)
    m1 = jnp.max(lg, axis=-1, keepdims=True)
    i1 = jnp.min(jnp.where(lg == m1, lane, LANES), axis=-1, keepdims=True)
    lg2 = jnp.where(lane == i1, NEG_INF, lg)
    m2 = jnp.max(lg2, axis=-1, keepdims=True)
    i2 = jnp.min(jnp.where(lg2 == m2, lane, LANES), axis=-1, keepdims=True)
    e2 = jnp.exp(m2 - m1)
    hot1 = (lane == i1) & real
    hot2 = (lane == i2) & real
    onehot = (hot1 | hot2).astype(F32)
    rank = jnp.dot(tri_ref[...], onehot.astype(BF16), preferred_element_type=F32)
    rank = jnp.where(hot1 | hot2, rank, -1.0)
    gate = jnp.where(hot1, 1.0 / (1.0 + e2), jnp.where(hot2, e2 / (1.0 + e2), 0.0))
    rank_ref[0] = rank.T[:N_EXPERTS]
    gate_ref[0] = gate.T[:N_EXPERTS]
    cnt_ref[0] = jnp.broadcast_to(jnp.sum(onehot, axis=0, keepdims=True), (8, LANES)).astype(jnp.int32)


def _route(h, g, w_router):
    m = h.shape[0]
    sb = MOE_BLOCK
    nblk = pl.cdiv(m, sb)
    t = jnp.arange(sb)
    tri = (t[None, :] < t[:, None]).astype(BF16)
    full = lambda a: pl.BlockSpec(a.shape, lambda i: (0,) * a.ndim)
    per_expert = pl.BlockSpec((1, N_EXPERTS, sb), lambda i: (i, 0, 0))
    return pl.pallas_call(
        functools.partial(_router_kernel, n_tokens=m),
        grid=(nblk,),
        in_specs=[pl.BlockSpec((sb, D_MODEL), lambda i: (i, 0)), full(g), full(w_router), full(tri)],
        out_specs=[pl.BlockSpec((D_MODEL, sb), lambda i: (0, i)), per_expert, per_expert,
                   pl.BlockSpec((1, 8, LANES), lambda i: (i, 0, 0))],
        out_shape=[jax.ShapeDtypeStruct((D_MODEL, nblk * sb), BF16), jax.ShapeDtypeStruct((nblk, N_EXPERTS, sb), F32),
                   jax.ShapeDtypeStruct((nblk, N_EXPERTS, sb), F32),
                   jax.ShapeDtypeStruct((nblk, 8, LANES), jnp.int32)],
        compiler_params=_cparams("parallel"),
        name="moe_router",
    )(h, g, w_router, tri)


def _moe_kernel(cnt_ref, x_ref, xnt_ref, rank_ref, gate_ref, wg_ref, wu_ref, wd_ref, fg_ref, o_ref,
                xs_ref, y_ref, acc_ref, *, final_norm):
    i, e, f = pl.program_id(0), pl.program_id(1), pl.program_id(2)
    last_f = f == pl.num_programs(2) - 1
    n_sub = (cnt_ref[i * N_EXPERTS + e] + (MOE_SUB - 1)) // MOE_SUB

    @pl.when((e == 0) & (f == 0))
    def _():
        acc_ref[...] = jnp.zeros_like(acc_ref)

    def selection(j):
        want = lax.broadcasted_iota(jnp.int32, (MOE_SUB, MOE_BLOCK), 0) + j * MOE_SUB
        return (rank_ref[0, pl.ds(e, 1), :] == want.astype(F32)).astype(BF16)

    def sub_body(j, carry):
        @pl.when(f == 0)
        def _():
            xs_ref[j] = lax.dot_general(xnt_ref[...], selection(j), NT, preferred_element_type=F32).astype(BF16)
            y_ref[j] = jnp.zeros((D_MODEL, MOE_SUB), F32)

        xs = xs_ref[j]
        gate = lax.dot_general(wg_ref[0], xs, TN, preferred_element_type=F32)
        up = lax.dot_general(wu_ref[0], xs, TN, preferred_element_type=F32)
        act = (gate * _sigmoid(gate) * up).astype(BF16)
        y_ref[j] += lax.dot_general(wd_ref[0], act, TN, preferred_element_type=F32)

        @pl.when(last_f)
        def _():
            back = jnp.dot(y_ref[j].astype(BF16), selection(j), preferred_element_type=F32)
            acc_ref[...] += gate_ref[0, pl.ds(e, 1), :] * back

        return carry

    lax.fori_loop(0, n_sub, sub_body, 0)

    @pl.when((e == N_EXPERTS - 1) & last_f)
    def _():
        y = x_ref[...] + acc_ref[...].T
        o_ref[...] = _rms(y, fg_ref[...]) if final_norm else y


def _moe(h, g, w_router, wg, wu, wd, final_g, final_norm):
    m = h.shape[0]
    sb, ft = MOE_BLOCK, MOE_FT
    xnt, rank, gate, cnt = _route(h, g, w_router)
    counts = cnt[:, 0, :N_EXPERTS].reshape(-1)
    row = pl.BlockSpec((sb, D_MODEL), lambda i, e, f, c: (i, 0))
    per_expert = pl.BlockSpec((1, N_EXPERTS, sb), lambda i, e, f, c: (i, 0, 0))
    grid_spec = pltpu.PrefetchScalarGridSpec(
        num_scalar_prefetch=1,
        grid=(pl.cdiv(m, sb), N_EXPERTS, FFN_EXPERT // ft),
        in_specs=[row, pl.BlockSpec((D_MODEL, sb), lambda i, e, f, c: (0, i)), per_expert, per_expert,
                  pl.BlockSpec((1, D_MODEL, ft), lambda i, e, f, c: (e, 0, f)),
                  pl.BlockSpec((1, D_MODEL, ft), lambda i, e, f, c: (e, 0, f)),
                  pl.BlockSpec((1, ft, D_MODEL), lambda i, e, f, c: (e, f, 0)),
                  pl.BlockSpec((1, D_MODEL), lambda i, e, f, c: (0, 0))],
        out_specs=row,
        scratch_shapes=[pltpu.VMEM((pl.cdiv(sb, MOE_SUB), D_MODEL, MOE_SUB), BF16),
                        pltpu.VMEM((pl.cdiv(sb, MOE_SUB), D_MODEL, MOE_SUB), F32),
                        pltpu.VMEM((D_MODEL, sb), F32)],
    )
    return pl.pallas_call(
        functools.partial(_moe_kernel, final_norm=final_norm),
        grid_spec=grid_spec,
        out_shape=jax.ShapeDtypeStruct(h.shape, F32),
        compiler_params=_cparams("parallel", "arbitrary", "arbitrary"),
        name="moe",
    )(counts, h, xnt, rank, gate, wg, wu, wd, final_g)


def kernel(x, positions, mix_norm_g, w_in, rwkv_mu, rwkv_w0, rwkv_w2, rwkv_a0, rwkv_a2, rwkv_g2, rwkv_k_k, rwkv_k_a, rwkv_r_k, rwkv_ln_g, rwkv_ln_b, rwkv_v0, rwkv_v1, rwkv_v2, gla_gate_up, gla_gate_b, gla_norm_g, w_out, ffn_norm_g, ffn_w_gate, ffn_w_up, ffn_w_down, moe_router, moe_w_gate, moe_w_up, moe_w_down, final_norm_g):
    B, S, D = x.shape
    M = B * S
    h = x.reshape(M, D)
    cos, sa, sb = _rope_tables(positions)
    v_first = None
    for layer in range(DEPTH):
        w = w_in[layer]
        wr = w[:, :RWKV_COLS].astype(BF16)
        wa = w[:, RWKV_COLS:RWKV_COLS + ATT_COLS].astype(BF16)
        wgl = w[:, RWKV_COLS + ATT_COLS:]
        wgl = jnp.concatenate([wgl[:, :3 * GLA_W], wgl[:, 3 * GLA_W + GLA_GATE_RANK:],
                               wgl[:, 3 * GLA_W:3 * GLA_W + GLA_GATE_RANK],
                               jnp.zeros((D, LANES - GLA_GATE_RANK), w.dtype)], axis=1).astype(BF16)
        y_rwkv, y_att, y_gla = _inproj(h, mix_norm_g[layer].reshape(1, D), wr, wa, wgl, cos, sa, sb)
        vres = None if layer == 0 else (rwkv_v0[layer - 1], rwkv_v1[layer - 1], rwkv_v2[layer - 1])
        o_rwkv, v_first = _rwkv_mixer(y_rwkv, S, rwkv_mu[layer], rwkv_w0[layer], rwkv_w2[layer], rwkv_a0[layer],
                                      rwkv_a2[layer], rwkv_g2[layer], rwkv_k_k[layer], rwkv_k_a[layer],
                                      rwkv_r_k[layer], rwkv_ln_g[layer], rwkv_ln_b[layer], v_first, vres)
        o_att = _att_mixer(y_att, S)
        o_gla = _gla_mixer(y_gla, S, gla_gate_up[layer], gla_gate_b[layer], gla_norm_g[layer])
        mixed = (o_rwkv, o_att, o_gla)
        wo = w_out[layer].astype(BF16)
        i = layer // 2
        g = ffn_norm_g[layer].reshape(1, D)
        if layer % 2 == 0:
            h = _ffn(h, mixed, wo, g, ffn_w_gate[i].astype(BF16), ffn_w_up[i].astype(BF16),
                     ffn_w_down[i].astype(BF16))
        else:
            wrt = jnp.pad(moe_router[i], ((0, 0), (0, LANES - N_EXPERTS)))
            h = _moe(_outproj(h, mixed, wo), g, wrt, moe_w_gate[i].astype(BF16), moe_w_up[i].astype(BF16),
                     moe_w_down[i].astype(BF16), final_norm_g.reshape(1, D), layer == DEPTH - 1)
    return h.reshape(B, S, D)
```

```python
import functools

import jax
import jax.numpy as jnp
from jax import lax
from jax.experimental import pallas as pl
from jax.experimental.pallas import tpu as pltpu

D_MODEL = 1024
DEPTH = 2
HEAD_DIM = 64
RWKV_W = 256
RWKV_H = 4
ATT_W = 384
GLA_W = 384
GLA_H = 6
RWKV_DECAY_RANK = 32
RWKV_AAA_RANK = 32
RWKV_GATE_RANK = 64
RWKV_GN_EPS = 64e-5
GLA_GATE_RANK = 16
GLA_GATE_TAU = 16.0
CHUNK = 64
DILATIONS = (1, 4, 16)
ATT_BLOCK = 128
ROPE_THETA = 500000.0
ROPE_DIMS = 16
ROPE_HALF = 8
FFN_DENSE = 2816
N_EXPERTS = 8
FFN_EXPERT = 3584
RMS_EPS = 1e-5
NEG_INF = -1e30

RWKV_COLS = 3 * RWKV_W + RWKV_DECAY_RANK + RWKV_AAA_RANK + RWKV_GATE_RANK
ATT_COLS = 3 * ATT_W
LANES = 128
GLA_COLS_PAD = 4 * GLA_W + LANES

VMEM_LIMIT = 56 * 1024 * 1024

ROPE_ROWS = 2048
INPROJ_ROWS = 1024
RWKV_ROWS = 512
GLA_ROWS = 2048
OUTPROJ_ROWS = 1024
FFN_ROWS = 1024
FFN_COLS = FFN_DENSE // 2

F32 = jnp.float32
BF16 = jnp.bfloat16
NN = (((1,), (0,)), ((), ()))
NT = (((1,), (1,)), ((), ()))
TN = (((0,), (0,)), ((), ()))
BNN = (((2,), (1,)), ((0,), (0,)))
BNT = (((2,), (2,)), ((0,), (0,)))
BTN = (((1,), (1,)), ((0,), (0,)))


def _cparams(*sem):
    return pltpu.CompilerParams(dimension_semantics=sem, vmem_limit_bytes=VMEM_LIMIT)


def _rms(x, g):
    return x * lax.rsqrt(jnp.mean(x * x, axis=-1, keepdims=True) + RMS_EPS) * g


def _sigmoid(x):
    return 1.0 / (1.0 + jnp.exp(-x))


def _log_sigmoid(x):
    return jnp.minimum(x, 0.0) - jnp.log(1.0 + jnp.exp(-jnp.abs(x)))


def _split_hi_lo(a):
    hi = a.astype(BF16)
    return hi, (a - hi.astype(F32)).astype(BF16)


def _dot3(a, b, dims=NN):
    a_hi, a_lo = _split_hi_lo(a)
    b_hi, b_lo = _split_hi_lo(b)
    d = lambda x, y: lax.dot_general(x, y, dims, preferred_element_type=F32)
    return d(a_hi, b_hi) + d(a_lo, b_hi) + d(a_hi, b_lo)


def _dot_exact_rhs(x, r_bf16):
    x_hi, x_lo = _split_hi_lo(x)
    return jnp.dot(x_hi, r_bf16, preferred_element_type=F32) + jnp.dot(x_lo, r_bf16, preferred_element_type=F32)


def _bdot(a, b, dims, passes):
    d = lambda x, y: lax.dot_general(x, y, dims, preferred_element_type=F32)
    if passes == 1:
        return d(a.astype(BF16), b.astype(BF16))
    a_hi, a_lo = _split_hi_lo(a)
    b_hi, b_lo = _split_hi_lo(b)
    return d(a_hi, b_hi) + d(a_lo, b_hi) + d(a_hi, b_lo)


def _head_ones(width):
    idx = jnp.arange(width) // HEAD_DIM
    return (idx[:, None] == idx[None, :]).astype(BF16)


def _chunk_selectors():
    t = jnp.arange(CHUNK)
    return (t[None, :] <= t[:, None]).astype(BF16), jnp.ones((CHUNK, CHUNK), BF16)


def _chunk_sums(sel, x):
    nc = x.shape[0] // CHUNK
    hi, lo = _split_hi_lo(x.reshape(nc, CHUNK, x.shape[1]))
    s3 = jnp.broadcast_to(sel[None], (nc, CHUNK, CHUNK))
    d = lambda a, b: lax.dot_general(a, b, BNN, preferred_element_type=F32)
    return (d(s3, hi) + d(s3, lo)).reshape(x.shape)


def _rope_table_kernel(pos_ref, invf_ref, cos_ref, sa_ref, sb_ref):
    ang = pos_ref[...].astype(F32) * invf_ref[...]
    lane = lax.broadcasted_iota(jnp.int32, ang.shape, 1) % HEAD_DIM
    c = jnp.cos(ang)
    s = jnp.sin(ang)
    cos_ref[...] = jnp.where(lane < ROPE_DIMS, c, 1.0)
    sa_ref[...] = jnp.where((lane >= ROPE_HALF) & (lane < ROPE_DIMS), s, 0.0)
    sb_ref[...] = jnp.where(lane < ROPE_HALF, -s, 0.0)


def _rope_tables(positions):
    m = positions.size
    tm = ROPE_ROWS
    lane = jnp.arange(LANES) % ROPE_HALF
    invf = (ROPE_THETA ** (-lane.astype(F32) / ROPE_HALF)).reshape(1, LANES)
    out = jax.ShapeDtypeStruct((m, LANES), F32)
    row = pl.BlockSpec((tm, LANES), lambda i: (i, 0))
    return pl.pallas_call(
        _rope_table_kernel,
        grid=(m // tm,),
        in_specs=[pl.BlockSpec((tm, 1), lambda i: (i, 0)), pl.BlockSpec((1, LANES), lambda i: (0, 0))],
        out_specs=[row, row, row],
        out_shape=[out, out, out],
        compiler_params=_cparams("parallel"),
        name="rope_tables",
    )(positions.reshape(m, 1), invf)


def _inproj_kernel(x_ref, g_ref, wr_ref, wa_ref, wg_ref, cos_ref, sa_ref, sb_ref, yr_ref, ya_ref, yg_ref):
    xb = _rms(x_ref[...], g_ref[...]).astype(BF16)
    yr_ref[...] = jnp.dot(xb, wr_ref[...], preferred_element_type=F32)
    yg_ref[...] = jnp.dot(xb, wg_ref[...], preferred_element_type=F32)
    ya = jnp.dot(xb, wa_ref[...], preferred_element_type=F32)
    cos, sa, sb = cos_ref[...], sa_ref[...], sb_ref[...]
    for j in range(2 * ATT_W // LANES):
        blk = ya[:, j * LANES:(j + 1) * LANES]
        rot = blk * cos + pltpu.roll(blk, ROPE_HALF, 1) * sa + pltpu.roll(blk, LANES - ROPE_HALF, 1) * sb
        if j < ATT_W // LANES:
            rot = rot * HEAD_DIM ** -0.5
        ya_ref[:, j * LANES:(j + 1) * LANES] = rot.astype(ya_ref.dtype)
    ya_ref[:, 2 * ATT_W:] = ya[:, 2 * ATT_W:].astype(ya_ref.dtype)


def _inproj(h, g, wr, wa, wg, cos, sa, sb):
    m = h.shape[0]
    tm = INPROJ_ROWS
    row = lambda w: pl.BlockSpec((tm, w), lambda i: (i, 0))
    full = lambda a: pl.BlockSpec(a.shape, lambda i: (0, 0))
    return pl.pallas_call(
        _inproj_kernel,
        grid=(m // tm,),
        in_specs=[row(D_MODEL), full(g), full(wr), full(wa), full(wg), row(LANES), row(LANES), row(LANES)],
        out_specs=[row(RWKV_COLS), row(ATT_COLS), row(GLA_COLS_PAD)],
        out_shape=[jax.ShapeDtypeStruct((m, RWKV_COLS), F32),
                   jax.ShapeDtypeStruct((m, ATT_COLS), BF16),
                   jax.ShapeDtypeStruct((m, GLA_COLS_PAD), F32)],
        compiler_params=_cparams("parallel"),
        name="inproj",
    )(h, g, wr, wa, wg, cos, sa, sb)


def _rwkv_kernel(*refs, has_vres, tile):
    nc = tile // CHUNK
    if has_vres:
        (y_ref, vf_ref, mu_ref, w0_ref, w2_ref, a0_ref, a2_ref, g2_ref, kk_ref, ka_ref, rk_ref, lng_ref, lnb_ref,
         psel_ref, tsel_ref, hones_ref, v0_ref, v1_ref, v2_ref, o_ref, *scratch) = refs
    else:
        (y_ref, mu_ref, w0_ref, w2_ref, a0_ref, a2_ref, g2_ref, kk_ref, ka_ref, rk_ref, lng_ref, lnb_ref,
         psel_ref, tsel_ref, hones_ref, o_ref, vfo_ref, *scratch) = refs
    state_ref, prev_ref, os_ref = scratch

    @pl.when(pl.program_id(1) == 0)
    def _():
        state_ref[...] = jnp.zeros_like(state_ref)
        prev_ref[...] = jnp.zeros_like(prev_ref)

    y = y_ref[...]
    row = lax.broadcasted_iota(jnp.int32, (tile, 1), 0)
    ysh = jnp.where(row == 0, prev_ref[...], pltpu.roll(y, 1, 0))
    prev_ref[...] = y[tile - 1:tile, :]
    ym = y + (ysh - y) * mu_ref[...]
    r = ym[:, 0:RWKV_W]
    k = ym[:, RWKV_W:2 * RWKV_W]
    v = ym[:, 2 * RWKV_W:3 * RWKV_W]
    x6 = ym[:, 3 * RWKV_W:]
    wl = w0_ref[...] + _dot3(jnp.tanh(x6), w2_ref[...])
    lw = -jnp.exp(_log_sigmoid(wl) - 0.5)
    dot1 = lambda x, w: jnp.dot(x.astype(BF16), w.astype(BF16), preferred_element_type=F32)
    a = _sigmoid(a0_ref[...] + dot1(x6, a2_ref[...]))
    g = dot1(_sigmoid(x6), g2_ref[...])
    if has_vres:
        v = v + (vf_ref[...] - v) * _sigmoid(v0_ref[...] + dot1(dot1(v, v1_ref[...]), v2_ref[...]))
    else:
        vfo_ref[...] = v

    head_ones = hones_ref[...]
    kk = k * kk_ref[...]
    kk = kk * lax.rsqrt(jnp.maximum(_dot_exact_rhs(kk * kk, head_ones), 1e-24))
    k2 = k * (1.0 + (a - 1.0) * ka_ref[...])
    bonus = _dot_exact_rhs(r * k2 * rk_ref[...], head_ones) * v

    c = _chunk_sums(psel_ref[...], lw)
    c_last = _chunk_sums(tsel_ref[...], lw)
    b = kk * a
    e_neg = jnp.exp(-c)
    e_end = jnp.exp(c_last - c)

    def units(val):
        return jnp.concatenate([val[:, h * HEAD_DIM:(h + 1) * HEAD_DIM].reshape(nc, CHUNK, HEAD_DIM)
                                for h in range(RWKV_H)], axis=0)

    rt = units(r * jnp.exp(c))
    at = units(-kk * jnp.exp(c - lw))
    bt = units(b * e_neg)
    kt = units(k2 * e_neg)
    bh = units(b * e_end)
    kh = units(k2 * e_end)
    vv = units(v)
    gam = units(jnp.exp(c_last))[:, 0:1, :]

    n = RWKV_H * nc
    qi = lax.broadcasted_iota(jnp.int32, (n, CHUNK, CHUNK), 1)
    qj = lax.broadcasted_iota(jnp.int32, (n, CHUNK, CHUNK), 2)
    strict = qj < qi
    incl = qj <= qi
    eye = qj == qi
    same_block = [qi // size == qj // size for size in (2, 4, 8, 16, 32, CHUNK)]
    a_ab = jnp.where(strict, _bdot(at, bt, BNT, 1), 0.0)
    a_ak = jnp.where(strict, _bdot(at, kt, BNT, 1), 0.0)
    p_b = jnp.where(incl, _bdot(rt, bt, BNT, 1), 0.0)
    p_k = jnp.where(incl, _bdot(rt, kt, BNT, 1), 0.0)
    t = jnp.where(eye, 1.0, jnp.where(same_block[0], a_ab, 0.0))
    for small, big in zip(same_block[:-1], same_block[1:]):
        t = t + _bdot(_bdot(t, jnp.where(big & ~small, a_ab, 0.0), BNN, 1), t, BNN, 1)
    w = _bdot(t, at, BNN, 1)
    u_loc = _bdot(t, _bdot(a_ak, vv, BNN, 1), BNN, 1)
    q_t = rt + _bdot(p_b, w, BNN, 1)
    o_loc = _bdot(p_b, u_loc, BNN, 1) + _bdot(p_k, vv, BNN, 1)
    per_chunk = lambda x: x.reshape(RWKV_H, nc, CHUNK, HEAD_DIM)
    g_mat = per_chunk(_bdot(w, bh, BTN, 1) + jnp.where(eye, gam, 0.0))
    h_mat = per_chunk(_bdot(u_loc, bh, BTN, 1) + _bdot(vv, kh, BTN, 1))

    s = state_ref[...]
    before = []
    for ci in range(nc):
        before.append(s)
        s = _bdot(s, g_mat[:, ci], BNN, 3) + h_mat[:, ci]
    state_ref[...] = s

    o = _bdot(q_t, jnp.stack(before, axis=1).reshape(n, HEAD_DIM, HEAD_DIM), BNT, 3) + o_loc
    for h in range(RWKV_H):
        os_ref[:, h * HEAD_DIM:(h + 1) * HEAD_DIM] = o[h * nc:(h + 1) * nc].reshape(tile, HEAD_DIM)

    o = os_ref[...]
    mean = _dot_exact_rhs(o, head_ones) * (1.0 / HEAD_DIM)
    oc = o - mean
    var = _dot_exact_rhs(oc * oc, head_ones) * (1.0 / HEAD_DIM)
    on = oc * lax.rsqrt(var + RWKV_GN_EPS) * lng_ref[...] + lnb_ref[...]
    o_ref[...] = ((on + bonus) * g).astype(o_ref.dtype)


def _rwkv_mixer(y, seq_len, mu, w0, w2, a0, a2, g2, k_k, k_a, r_k, ln_g, ln_b, v_first, vres):
    m = y.shape[0]
    tile = RWKV_ROWS
    nt = seq_len // tile
    has_vres = vres is not None
    row = lambda wd: pl.BlockSpec((tile, wd), lambda b, j: (b * nt + j, 0))
    full = lambda arr: pl.BlockSpec(arr.shape, lambda b, j: (0, 0))
    vec = lambda t: t.reshape(1, -1)
    pad_rows = lambda w, start: jnp.zeros((LANES, RWKV_W), F32).at[start:start + w.shape[0]].set(w)
    consts = [vec(mu), vec(w0), pad_rows(w2, 0), vec(a0), pad_rows(a2, RWKV_DECAY_RANK),
              pad_rows(g2, RWKV_DECAY_RANK + RWKV_AAA_RANK), vec(k_k), vec(k_a), vec(r_k), vec(ln_g), vec(ln_b),
              *_chunk_selectors(), _head_ones(RWKV_W)]
    args = [y] + ([v_first] if has_vres else []) + consts
    in_specs = [row(RWKV_COLS)] + ([row(RWKV_W)] if has_vres else []) + [full(c) for c in consts]
    out_shape = [jax.ShapeDtypeStruct((m, RWKV_W), BF16)]
    out_specs = [row(RWKV_W)]
    if has_vres:
        v0, v1, v2 = vres
        extra = [vec(v0), jnp.pad(v1, ((0, 0), (0, LANES - v1.shape[1]))),
                 jnp.pad(v2, ((0, LANES - v2.shape[0]), (0, 0)))]
        args += extra
        in_specs += [full(c) for c in extra]
    else:
        out_shape.append(jax.ShapeDtypeStruct((m, RWKV_W), F32))
        out_specs.append(row(RWKV_W))
    outs = pl.pallas_call(
        functools.partial(_rwkv_kernel, has_vres=has_vres, tile=tile),
        grid=(m // seq_len, nt),
        in_specs=in_specs, out_specs=out_specs, out_shape=out_shape,
        scratch_shapes=[pltpu.VMEM((RWKV_H, HEAD_DIM, HEAD_DIM), F32), pltpu.VMEM((1, RWKV_COLS), F32),
                        pltpu.VMEM((tile, RWKV_W), F32)],
        compiler_params=_cparams("parallel", "arbitrary"),
        name="rwkv7",
    )(*args)
    return (outs[0], v_first) if has_vres else (outs[0], outs[1])


ATT_PAD = ATT_BLOCK * max(DILATIONS)
ATT_UNROLL = 8


def _att_kernel(q_ref, k_ref, v_ref, o_ref, qs_ref, ks_ref, vs_ref, m_ref, n_ref, d_ref, *, seq_len):
    qs_ref[...] = q_ref[...].astype(F32)
    ks_ref[0:ATT_PAD, :] = jnp.zeros((ATT_PAD, LANES), F32)
    vs_ref[0:ATT_PAD, :] = jnp.zeros((ATT_PAD, LANES), F32)
    ks_ref[ATT_PAD:, :] = k_ref[...].astype(F32)
    vs_ref[ATT_PAD:, :] = v_ref[...].astype(F32)

    qi = lax.broadcasted_iota(jnp.int32, (ATT_BLOCK, 2 * ATT_BLOCK), 0)
    kc = lax.broadcasted_iota(jnp.int32, (ATT_BLOCK, 2 * ATT_BLOCK), 1)
    band = (kc >= qi) & (kc <= qi + ATT_BLOCK)
    head0 = lax.broadcasted_iota(jnp.int32, (ATT_BLOCK, LANES), 1) < HEAD_DIM

    def block(g, dil, first, q, k, v, rows):
        valid = band & ((kc >= ATT_BLOCK) | jnp.logical_not(first))
        v1 = jnp.concatenate([v, jnp.ones_like(v)], axis=1)
        mxs, pvs = [], []
        for hh in range(2):
            qh = jnp.where(head0 if hh == 0 else jnp.logical_not(head0), q, jnp.zeros_like(q))
            s = jnp.where(valid, lax.dot_general(qh, k, NT, preferred_element_type=F32), NEG_INF)
            mx = jnp.max(s, axis=-1, keepdims=True)
            mxs.append(mx)
            pvs.append(jnp.dot(jnp.exp(s - mx).astype(BF16), v1, preferred_element_type=F32))
        mx = jnp.where(head0, mxs[0], mxs[1])
        pv = jnp.where(head0, pvs[0][:, :LANES], pvs[1][:, :LANES])
        den = jnp.where(head0, pvs[0][:, LANES:], pvs[1][:, LANES:])
        if g > 0:
            m_old = m_ref[rows, :]
            m_new = jnp.maximum(m_old, mx)
            a_old = jnp.exp(m_old - m_new)
            a_new = jnp.exp(mx - m_new)
            pv = a_old * n_ref[rows, :] + a_new * pv
            den = a_old * d_ref[rows, :] + a_new * den
            mx = m_new
        if g == len(DILATIONS) - 1:
            o_ref[rows, :] = pv / den
        else:
            m_ref[rows, :] = mx
            n_ref[rows, :] = pv
            d_ref[rows, :] = den

    for g, dil in enumerate(DILATIONS):
        npairs = seq_len // dil // ATT_BLOCK // 2
        step = ATT_BLOCK * dil

        def body(it, carry, g=g, dil=dil, npairs=npairs, step=step):
            r = it // npairs
            jp = it % npairs
            start = r + jp * (2 * step)
            q = qs_ref[pl.ds(start, 2 * ATT_BLOCK, stride=dil), :].astype(BF16)
            k = ks_ref[pl.ds(start + (ATT_PAD - step), 3 * ATT_BLOCK, stride=dil), :].astype(BF16)
            v = vs_ref[pl.ds(start + (ATT_PAD - step), 3 * ATT_BLOCK, stride=dil), :].astype(BF16)
            for u in range(2):
                block(g, dil, (jp == 0) if u == 0 else False, q[u * ATT_BLOCK:(u + 1) * ATT_BLOCK],
                      k[u * ATT_BLOCK:(u + 2) * ATT_BLOCK], v[u * ATT_BLOCK:(u + 2) * ATT_BLOCK],
                      pl.ds(start + u * step, ATT_BLOCK, stride=dil))
            return carry

        lax.fori_loop(0, dil * npairs, body, 0, unroll=ATT_UNROLL)


def _att_mixer(y_att, seq_len):
    m = y_att.shape[0]
    npair = ATT_W // LANES
    col = lambda c: pl.BlockSpec((seq_len, LANES), lambda b, hp: (b, c * npair + hp))
    seq_buf = pltpu.VMEM((seq_len, LANES), F32)
    pad_buf = pltpu.VMEM((ATT_PAD + seq_len, LANES), F32)
    return pl.pallas_call(
        functools.partial(_att_kernel, seq_len=seq_len),
        grid=(m // seq_len, npair),
        in_specs=[col(0), col(1), col(2)],
        out_specs=pl.BlockSpec((seq_len, LANES), lambda b, hp: (b, hp)),
        out_shape=jax.ShapeDtypeStruct((m, ATT_W), F32),
        scratch_shapes=[seq_buf, pad_buf, pad_buf, seq_buf, seq_buf, seq_buf],
        compiler_params=_cparams("parallel", "parallel"),
        name="dilated_att",
    )(y_att, y_att, y_att)


def _gla_kernel(y_ref, gu_ref, gb_ref, ng_ref, psel_ref, tsel_ref, hones_ref, o_ref, state_ref, os_ref, *, tile):
    nc = tile // CHUNK
    n = GLA_H * nc

    @pl.when(pl.program_id(1) == 0)
    def _():
        state_ref[...] = jnp.zeros_like(state_ref)

    def units(val):
        return jnp.concatenate([val[:, h * HEAD_DIM:(h + 1) * HEAD_DIM].reshape(nc, CHUNK, HEAD_DIM)
                                for h in range(GLA_H)], axis=0)

    q = y_ref[:, 0:GLA_W] * HEAD_DIM ** -0.5
    k = y_ref[:, GLA_W:2 * GLA_W]
    og = y_ref[:, 3 * GLA_W:4 * GLA_W]
    lg = _log_sigmoid(_dot3(y_ref[:, 4 * GLA_W:], gu_ref[...]) + gb_ref[...]) * (1.0 / GLA_GATE_TAU)
    b = _chunk_sums(psel_ref[...], lg)
    b_last = _chunk_sums(tsel_ref[...], lg)
    q_in = units((q * jnp.exp(b)).astype(BF16))
    k_in = units((k * jnp.exp(-b)).astype(BF16))
    k_end = units((k * jnp.exp(b_last - b)).astype(BF16))
    v = units(y_ref[:, 2 * GLA_W:3 * GLA_W].astype(BF16))
    gam = units(jnp.exp(b_last))[:, 0:1, :].reshape(GLA_H, nc, 1, HEAD_DIM)

    ci = lax.broadcasted_iota(jnp.int32, (n, CHUNK, CHUNK), 1)
    cj = lax.broadcasted_iota(jnp.int32, (n, CHUNK, CHUNK), 2)
    d = lambda a, bb, dims: lax.dot_general(a, bb, dims, preferred_element_type=F32)
    att = jnp.where(cj <= ci, d(q_in, k_in, BNT), 0.0)
    o = d(att.astype(BF16), v, BNN)
    kv = d(v, k_end, BTN).reshape(GLA_H, nc, HEAD_DIM, HEAD_DIM)

    s = state_ref[...]
    before = []
    for c in range(nc):
        before.append(s)
        s = s * gam[:, c] + kv[:, c]
    state_ref[...] = s
    s_before = jnp.stack(before, axis=1).reshape(n, HEAD_DIM, HEAD_DIM)
    o = o + d(q_in, s_before.astype(BF16), BNT)
    for h in range(GLA_H):
        os_ref[:, h * HEAD_DIM:(h + 1) * HEAD_DIM] = o[h * nc:(h + 1) * nc].reshape(tile, HEAD_DIM)

    o = os_ref[...]
    ms = _dot_exact_rhs(o * o, hones_ref[...]) * (1.0 / HEAD_DIM)
    o_ref[...] = (o * lax.rsqrt(ms + RMS_EPS) * ng_ref[...] * (og * _sigmoid(og))).astype(o_ref.dtype)


def _gla_mixer(y, seq_len, gate_up, gate_b, norm_g):
    m = y.shape[0]
    tile = GLA_ROWS
    nt = seq_len // tile
    gu = jnp.pad(gate_up, ((0, LANES - gate_up.shape[0]), (0, 0)))
    gb = gate_b.reshape(1, GLA_W)
    ng = jnp.tile(norm_g, GLA_H).reshape(1, GLA_W)
    consts = [gu, gb, ng, *_chunk_selectors(), _head_ones(GLA_W)]
    full = lambda a: pl.BlockSpec(a.shape, lambda b, j: (0, 0))
    return pl.pallas_call(
        functools.partial(_gla_kernel, tile=tile),
        grid=(m // seq_len, nt),
        in_specs=[pl.BlockSpec((tile, GLA_COLS_PAD), lambda b, j: (b * nt + j, 0))] + [full(c) for c in consts],
        out_specs=pl.BlockSpec((tile, GLA_W), lambda b, j: (b * nt + j, 0)),
        out_shape=jax.ShapeDtypeStruct((m, GLA_W), BF16),
        scratch_shapes=[pltpu.VMEM((GLA_H, HEAD_DIM, HEAD_DIM), F32), pltpu.VMEM((tile, GLA_W), F32)],
        compiler_params=_cparams("parallel", "arbitrary"),
        name="gla",
    )(y, *consts)


def _mixed_residual(h_ref, or_ref, oa_ref, og_ref, w_ref):
    acc = jnp.dot(or_ref[...], w_ref[0:RWKV_W, :], preferred_element_type=F32)
    acc += jnp.dot(oa_ref[...].astype(BF16), w_ref[RWKV_W:RWKV_W + ATT_W, :], preferred_element_type=F32)
    acc += jnp.dot(og_ref[...], w_ref[RWKV_W + ATT_W:, :], preferred_element_type=F32)
    return h_ref[...] + acc


def _mixed_specs(tm, index_map, w):
    row = lambda wd: pl.BlockSpec((tm, wd), index_map)
    return [row(D_MODEL), row(RWKV_W), row(ATT_W), row(GLA_W), pl.BlockSpec(w.shape, lambda *_: (0, 0))]


def _outproj_kernel(h_ref, or_ref, oa_ref, og_ref, w_ref, o_ref):
    o_ref[...] = _mixed_residual(h_ref, or_ref, oa_ref, og_ref, w_ref)


def _outproj(h, mixed, w_out):
    tm = OUTPROJ_ROWS
    return pl.pallas_call(
        _outproj_kernel,
        grid=(h.shape[0] // tm,),
        in_specs=_mixed_specs(tm, lambda i: (i, 0), w_out),
        out_specs=pl.BlockSpec((tm, D_MODEL), lambda i: (i, 0)),
        out_shape=jax.ShapeDtypeStruct(h.shape, F32),
        compiler_params=_cparams("parallel"),
        name="outproj",
    )(h, *mixed, w_out)


def _ffn_kernel(h_ref, or_ref, oa_ref, og_ref, wo_ref, g_ref, wg_ref, wu_ref, wd_ref, o_ref, xn_ref):
    @pl.when(pl.program_id(1) == 0)
    def _():
        x = _mixed_residual(h_ref, or_ref, oa_ref, og_ref, wo_ref)
        xn_ref[...] = _rms(x, g_ref[...]).astype(xn_ref.dtype)
        o_ref[...] = x

    xn = xn_ref[...]
    gate = jnp.dot(xn, wg_ref[...], preferred_element_type=F32)
    up = jnp.dot(xn, wu_ref[...], preferred_element_type=F32)
    act = (gate * _sigmoid(gate) * up).astype(BF16)
    o_ref[...] += jnp.dot(act, wd_ref[...], preferred_element_type=F32)


def _ffn(h, mixed, w_out, g, wg, wu, wd):
    m = h.shape[0]
    tm, tf = FFN_ROWS, FFN_COLS
    return pl.pallas_call(
        _ffn_kernel,
        grid=(m // tm, FFN_DENSE // tf),
        in_specs=_mixed_specs(tm, lambda i, f: (i, 0), w_out) + [
            pl.BlockSpec(g.shape, lambda i, f: (0, 0)),
            pl.BlockSpec((D_MODEL, tf), lambda i, f: (0, f)),
            pl.BlockSpec((D_MODEL, tf), lambda i, f: (0, f)),
            pl.BlockSpec((tf, D_MODEL), lambda i, f: (f, 0))],
        out_specs=pl.BlockSpec((tm, D_MODEL), lambda i, f: (i, 0)),
        out_shape=jax.ShapeDtypeStruct(h.shape, F32),
        scratch_shapes=[pltpu.VMEM((tm, D_MODEL), BF16)],
        compiler_params=_cparams("parallel", "arbitrary"),
        name="ffn_dense",
    )(h, *mixed, w_out, g, wg, wu, wd)


MOE_BLOCK = 896
MOE_SUB = 256
MOE_FT = 896


def _router_kernel(x_ref, g_ref, wr_ref, tri_ref, xnt_ref, rank_ref, gate_ref, cnt_ref, *, n_tokens):
    row = lax.broadcasted_iota(jnp.int32, (MOE_BLOCK, 1), 0) + pl.program_id(0) * MOE_BLOCK
    real = row < n_tokens
    xn = jnp.where(real, _rms(x_ref[...], g_ref[...]), 0.0)
    xnt_ref[...] = xn.T.astype(BF16)
    logits = _dot3(xn, wr_ref[...])
    lane = lax.broadcasted_iota(jnp.int32, logits.shape, 1)
    lg = jnp.where(lane < N_EXPERTS, logits, NEG_INF)
    m1 = jnp.max(lg, axis=-1, keepdims=True)
    i1 = jnp.min(jnp.where(lg == m1, lane, LANES), axis=-1, keepdims=True)
    lg2 = jnp.where(lane == i1, NEG_INF, lg)
    m2 = jnp.max(lg2, axis=-1, keepdims=True)
    i2 = jnp.min(jnp.where(lg2 == m2, lane, LANES), axis=-1, keepdims=True)
    e2 = jnp.exp(m2 - m1)
    hot1 = (lane == i1) & real
    hot2 = (lane == i2) & real
    onehot = (hot1 | hot2).astype(F32)
    rank = jnp.dot(tri_ref[...], onehot.astype(BF16), preferred_element_type=F32)
    rank = jnp.where(hot1 | hot2, rank, -1.0)
    gate = jnp.where(hot1, 1.0 / (1.0 + e2), jnp.where(hot2, e2 / (1.0 + e2), 0.0))
    rank_ref[0] = rank.T[:N_EXPERTS]
    gate_ref[0] = gate.T[:N_EXPERTS]
    cnt_ref[0] = jnp.broadcast_to(jnp.sum(onehot, axis=0, keepdims=True), (8, LANES)).astype(jnp.int32)


def _route(h, g, w_router):
    m = h.shape[0]
    sb = MOE_BLOCK
    nblk = pl.cdiv(m, sb)
    t = jnp.arange(sb)
    tri = (t[None, :] < t[:, None]).astype(BF16)
    full = lambda a: pl.BlockSpec(a.shape, lambda i: (0,) * a.ndim)
    per_expert = pl.BlockSpec((1, N_EXPERTS, sb), lambda i: (i, 0, 0))
    return pl.pallas_call(
        functools.partial(_router_kernel, n_tokens=m),
        grid=(nblk,),
        in_specs=[pl.BlockSpec((sb, D_MODEL), lambda i: (i, 0)), full(g), full(w_router), full(tri)],
        out_specs=[pl.BlockSpec((D_MODEL, sb), lambda i: (0, i)), per_expert, per_expert,
                   pl.BlockSpec((1, 8, LANES), lambda i: (i, 0, 0))],
        out_shape=[jax.ShapeDtypeStruct((D_MODEL, nblk * sb), BF16), jax.ShapeDtypeStruct((nblk, N_EXPERTS, sb), F32),
                   jax.ShapeDtypeStruct((nblk, N_EXPERTS, sb), F32),
                   jax.ShapeDtypeStruct((nblk, 8, LANES), jnp.int32)],
        compiler_params=_cparams("parallel"),
        name="moe_router",
    )(h, g, w_router, tri)


def _moe_kernel(cnt_ref, x_ref, xnt_ref, rank_ref, gate_ref, wg_ref, wu_ref, wd_ref, fg_ref, o_ref,
                xs_ref, y_ref, acc_ref, *, final_norm):
    i, e, f = pl.program_id(0), pl.program_id(1), pl.program_id(2)
    last_f = f == pl.num_programs(2) - 1
    n_sub = (cnt_ref[i * N_EXPERTS + e] + (MOE_SUB - 1)) // MOE_SUB

    @pl.when((e == 0) & (f == 0))
    def _():
        acc_ref[...] = jnp.zeros_like(acc_ref)

    def selection(j):
        want = lax.broadcasted_iota(jnp.int32, (MOE_SUB, MOE_BLOCK), 0) + j * MOE_SUB
        return (rank_ref[0, pl.ds(e, 1), :] == want.astype(F32)).astype(BF16)

    def sub_body(j, carry):
        @pl.when(f == 0)
        def _():
            xs_ref[j] = lax.dot_general(xnt_ref[...], selection(j), NT, preferred_element_type=F32).astype(BF16)
            y_ref[j] = jnp.zeros((D_MODEL, MOE_SUB), F32)

        xs = xs_ref[j]
        gate = lax.dot_general(wg_ref[0], xs, TN, preferred_element_type=F32)
        up = lax.dot_general(wu_ref[0], xs, TN, preferred_element_type=F32)
        act = (gate * _sigmoid(gate) * up).astype(BF16)
        y_ref[j] += lax.dot_general(wd_ref[0], act, TN, preferred_element_type=F32)

        @pl.when(last_f)
        def _():
            back = jnp.dot(y_ref[j].astype(BF16), selection(j), preferred_element_type=F32)
            acc_ref[...] += gate_ref[0, pl.ds(e, 1), :] * back

        return carry

    lax.fori_loop(0, n_sub, sub_body, 0)

    @pl.when((e == N_EXPERTS - 1) & last_f)
    def _():
        y = x_ref[...] + acc_ref[...].T
        o_ref[...] = _rms(y, fg_ref[...]) if final_norm else y


def _moe(h, g, w_router, wg, wu, wd, final_g, final_norm):
    m = h.shape[0]
    sb, ft = MOE_BLOCK, MOE_FT
    xnt, rank, gate, cnt = _route(h, g, w_router)
    counts = cnt[:, 0, :N_EXPERTS].reshape(-1)
    row = pl.BlockSpec((sb, D_MODEL), lambda i, e, f, c: (i, 0))
    per_expert = pl.BlockSpec((1, N_EXPERTS, sb), lambda i, e, f, c: (i, 0, 0))
    grid_spec = pltpu.PrefetchScalarGridSpec(
        num_scalar_prefetch=1,
        grid=(pl.cdiv(m, sb), N_EXPERTS, FFN_EXPERT // ft),
        in_specs=[row, pl.BlockSpec((D_MODEL, sb), lambda i, e, f, c: (0, i)), per_expert, per_expert,
                  pl.BlockSpec((1, D_MODEL, ft), lambda i, e, f, c: (e, 0, f)),
                  pl.BlockSpec((1, D_MODEL, ft), lambda i, e, f, c: (e, 0, f)),
                  pl.BlockSpec((1, ft, D_MODEL), lambda i, e, f, c: (e, f, 0)),
                  pl.BlockSpec((1, D_MODEL), lambda i, e, f, c: (0, 0))],
        out_specs=row,
        scratch_shapes=[pltpu.VMEM((pl.cdiv(sb, MOE_SUB), D_MODEL, MOE_SUB), BF16),
                        pltpu.VMEM((pl.cdiv(sb, MOE_SUB), D_MODEL, MOE_SUB), F32),
                        pltpu.VMEM((D_MODEL, sb), F32)],
    )
    return pl.pallas_call(
        functools.partial(_moe_kernel, final_norm=final_norm),
        grid_spec=grid_spec,
        out_shape=jax.ShapeDtypeStruct(h.shape, F32),
        compiler_params=_cparams("parallel", "arbitrary", "arbitrary"),
        name="moe",
    )(counts, h, xnt, rank, gate, wg, wu, wd, final_g)


def kernel(x, positions, mix_norm_g, w_in, rwkv_mu, rwkv_w0, rwkv_w2, rwkv_a0, rwkv_a2, rwkv_g2, rwkv_k_k, rwkv_k_a, rwkv_r_k, rwkv_ln_g, rwkv_ln_b, rwkv_v0, rwkv_v1, rwkv_v2, gla_gate_up, gla_gate_b, gla_norm_g, w_out, ffn_norm_g, ffn_w_gate, ffn_w_up, ffn_w_down, moe_router, moe_w_gate, moe_w_up, moe_w_down, final_norm_g):
    B, S, D = x.shape
    M = B * S
    h = x.reshape(M, D)
    cos, sa, sb = _rope_tables(positions)
    v_first = None
    for layer in range(DEPTH):
        w = w_in[layer]
        wr = w[:, :RWKV_COLS].astype(BF16)
        wa = w[:, RWKV_COLS:RWKV_COLS + ATT_COLS].astype(BF16)
        wgl = w[:, RWKV_COLS + ATT_COLS:]
        wgl = jnp.concatenate([wgl[:, :3 * GLA_W], wgl[:, 3 * GLA_W + GLA_GATE_RANK:],
                               wgl[:, 3 * GLA_W:3 * GLA_W + GLA_GATE_RANK],
                               jnp.zeros((D, LANES - GLA_GATE_RANK), w.dtype)], axis=1).astype(BF16)
        y_rwkv, y_att, y_gla = _inproj(h, mix_norm_g[layer].reshape(1, D), wr, wa, wgl, cos, sa, sb)
        vres = None if layer == 0 else (rwkv_v0[layer - 1], rwkv_v1[layer - 1], rwkv_v2[layer - 1])
        o_rwkv, v_first = _rwkv_mixer(y_rwkv, S, rwkv_mu[layer], rwkv_w0[layer], rwkv_w2[layer], rwkv_a0[layer],
                                      rwkv_a2[layer], rwkv_g2[layer], rwkv_k_k[layer], rwkv_k_a[layer],
                                      rwkv_r_k[layer], rwkv_ln_g[layer], rwkv_ln_b[layer], v_first, vres)
        o_att = _att_mixer(y_att, S)
        o_gla = _gla_mixer(y_gla, S, gla_gate_up[layer], gla_gate_b[layer], gla_norm_g[layer])
        mixed = (o_rwkv, o_att, o_gla)
        wo = w_out[layer].astype(BF16)
        i = layer // 2
        g = ffn_norm_g[layer].reshape(1, D)
        if layer % 2 == 0:
            h = _ffn(h, mixed, wo, g, ffn_w_gate[i].astype(BF16), ffn_w_up[i].astype(BF16),
                     ffn_w_down[i].astype(BF16))
        else:
            wrt = jnp.pad(moe_router[i], ((0, 0), (0, LANES - N_EXPERTS)))
            h = _moe(_outproj(h, mixed, wo), g, wrt, moe_w_gate[i].astype(BF16), moe_w_up[i].astype(BF16),
                     moe_w_down[i].astype(BF16), final_norm_g.reshape(1, D), layer == DEPTH - 1)
    return h.reshape(B, S, D)
```
